```python
import jax, jax.numpy as jnp
from jax import lax
import numpy as np

D_MODEL = 1024
BATCH = 16
SEQ = 2048
DEPTH = 1
DEC_BATCH = 8
DEC_SEQ = 16
PAST_LEN = 2048

CHUNK = 64
Q_BLOCK = 128
N_MEM = 256
H_FOX = 8
DH_FOX = 64
H_RET = 4
DK_RET = 128
DV_RET = 256
H_MEM = 4
DH_MEM = 128
W_FOX = H_FOX * DH_FOX
W_RET_QK = H_RET * DK_RET
W_RET_V = H_RET * DV_RET
W_MEM = H_MEM * DH_MEM
N_BRANCH = 3
D_IN = 4 * W_FOX + H_FOX + 2 * W_RET_QK + 2 * W_RET_V + 2 * W_MEM + N_BRANCH * D_MODEL
FORGET_BIAS = 3.0
ROPE_BASE = 10000.0
EPS = 1e-6

kernel_name = 'hybrid_fox_retention_memory_stream_step'


def rms_norm(x, g):
    xf = x.astype(jnp.float32)
    y = xf * lax.rsqrt(jnp.mean(xf * xf, axis=-1, keepdims=True) + EPS)
    return (y * g.astype(jnp.float32)).astype(x.dtype)


def head_group_norm(x):
    xf = x.astype(jnp.float32)
    mu = jnp.mean(xf, axis=-1, keepdims=True)
    var = jnp.mean(jnp.square(xf - mu), axis=-1, keepdims=True)
    return (xf - mu) * lax.rsqrt(var + EPS)


def rope(x, pos):
    half = x.shape[-1] // 2
    inv = ROPE_BASE ** (-jnp.arange(half, dtype=jnp.float32) / half)
    ang = pos.astype(jnp.float32)[:, None] * inv[None, :]
    cos = jnp.cos(ang)[None, :, None, :]
    sin = jnp.sin(ang)[None, :, None, :]
    xf = x.astype(jnp.float32)
    x1, x2 = xf[..., :half], xf[..., half:]
    return jnp.concatenate([x1 * cos - x2 * sin, x1 * sin + x2 * cos], axis=-1).astype(x.dtype)


def ret_log_gamma():
    return jnp.log1p(-jnp.exp2(-5.0 - jnp.arange(H_RET, dtype=jnp.float32)))


def split_proj(z):
    sizes = (W_FOX, W_FOX, W_FOX, H_FOX, W_FOX, W_RET_QK, W_RET_QK, W_RET_V, W_RET_V, W_MEM, W_MEM, N_BRANCH * D_MODEL)
    idx = np.cumsum(sizes)[:-1].tolist()
    return jnp.split(z, idx, axis=-1)


def mixer_inputs(x, pos, g_norm, w_in, b_f, b_merge, g_fox_q, g_fox_k, g_mem_q):
    B, T = x.shape[0], x.shape[1]
    h = rms_norm(x, g_norm)
    z = jnp.einsum('btd,de->bte', h, w_in)
    fq, fk, fv, ff, fg, rq, rk, rv, rg, mq, mg, gl = split_proj(z)
    fq = rms_norm(fq.reshape(B, T, H_FOX, DH_FOX), g_fox_q)
    fk = rms_norm(fk.reshape(B, T, H_FOX, DH_FOX), g_fox_k)
    fv = fv.reshape(B, T, H_FOX, DH_FOX)
    logf = jax.nn.log_sigmoid(ff.astype(jnp.float32) + b_f.astype(jnp.float32))
    rq = rope(rq.reshape(B, T, H_RET, DK_RET), pos)
    rk = rope(rk.reshape(B, T, H_RET, DK_RET), pos) * (DK_RET ** -0.5)
    rv = rv.reshape(B, T, H_RET, DV_RET)
    mq = rms_norm(mq.reshape(B, T, H_MEM, DH_MEM), g_mem_q)
    gates = jax.nn.sigmoid((gl + b_merge).astype(jnp.float32)).reshape(B, T, N_BRANCH, D_MODEL)
    return fq, fk, fv, logf, rq, rk, rv, mq, fg, rg, mg, gates


def fox_attend(q, c_q, q_pos, k, v, c_k):
    s = jnp.einsum('bqhd,bkhd->bhqk', q, k, preferred_element_type=jnp.float32) * (DH_FOX ** -0.5)
    s = s + jnp.swapaxes(c_q, 1, 2)[..., :, None] - jnp.swapaxes(c_k, 1, 2)[..., None, :]
    k_pos = jnp.arange(k.shape[1])
    s = jnp.where(k_pos[None, :] <= q_pos[:, None], s, -jnp.inf)
    p = jax.nn.softmax(s, axis=-1).astype(v.dtype)
    return jnp.einsum('bhqk,bkhd->bqhd', p, v)


def fox_prompt(q, k, v, c):
    B, T, H, D = q.shape

    def one_block(bi):
        start = bi * Q_BLOCK
        qb = lax.dynamic_slice_in_dim(q, start, Q_BLOCK, axis=1)
        cb = lax.dynamic_slice_in_dim(c, start, Q_BLOCK, axis=1)
        q_pos = start + jnp.arange(Q_BLOCK)
        return fox_attend(qb, cb, q_pos, k, v, c)

    o = lax.map(one_block, jnp.arange(T // Q_BLOCK))
    return jnp.swapaxes(o, 0, 1).reshape(B, T, H, D)


def retention_block(q, k, v, s, log_g):
    L = q.shape[1]
    i = jnp.arange(L, dtype=jnp.float32)
    dec_q = jnp.exp(log_g[:, None] * (i[None, :] + 1.0))
    diff = i[:, None] - i[None, :]
    dmask = jnp.where(diff >= 0, jnp.exp(log_g[:, None, None] * jnp.maximum(diff, 0.0)), 0.0)
    inter = jnp.einsum('blhk,bhkv->blhv', q, s) * jnp.swapaxes(dec_q, 0, 1)[None, :, :, None]
    scores = jnp.einsum('blhk,bmhk->bhlm', q, k) * dmask[None]
    intra = jnp.einsum('bhlm,bmhv->blhv', scores, v)
    dec_k = jnp.exp(log_g[:, None] * (L - 1.0 - i[None, :]))
    s_new = jnp.exp(log_g * L)[None, :, None, None] * s + jnp.einsum('blhk,blhv,hl->bhkv', k, v, dec_k)
    return inter + intra, s_new


def retention_prompt(q, k, v, log_g):
    B, T, H, dk = q.shape
    dv = v.shape[-1]
    nc = T // CHUNK

    def to_chunks(a):
        return jnp.swapaxes(a.astype(jnp.float32).reshape(B, nc, CHUNK, H, a.shape[-1]), 0, 1)

    def step(s, blk):
        qc, kc, vc = blk
        o, s = retention_block(qc, kc, vc, s, log_g)
        return s, o

    s0 = jnp.zeros((B, H, dk, dv), jnp.float32)
    s_fin, o = lax.scan(step, s0, (to_chunks(q), to_chunks(k), to_chunks(v)))
    return jnp.swapaxes(o, 0, 1).reshape(B, T, H, dv), s_fin


def memory_kv(mem, g_mem_norm, w_mem_kv, g_mem_k):
    B, N = mem.shape[0], mem.shape[1]
    hm = rms_norm(mem, g_mem_norm)
    mk, mv = jnp.split(jnp.einsum('bnd,de->bne', hm, w_mem_kv), 2, axis=-1)
    mk = rms_norm(mk.reshape(B, N, H_MEM, DH_MEM), g_mem_k)
    return mk, mv.reshape(B, N, H_MEM, DH_MEM)


def memory_attend(q, mk, mv):
    s = jnp.einsum('bthd,bmhd->bhtm', q, mk.astype(q.dtype), preferred_element_type=jnp.float32) * (DH_MEM ** -0.5)
    p = jax.nn.softmax(s, axis=-1).astype(q.dtype)
    return jnp.einsum('bhtm,bmhd->bthd', p, mv.astype(q.dtype))


def mixer_output(x, fox_o, fg, ret_o, rg, mem_o, mg, gates, w_p_fox, w_p_ret, w_p_mem, w_out):
    B, T = x.shape[0], x.shape[1]
    a = jnp.einsum('btc,cd->btd', fox_o.reshape(B, T, W_FOX) * jax.nn.silu(fg), w_p_fox)
    r_in = head_group_norm(ret_o).reshape(B, T, W_RET_V).astype(x.dtype) * jax.nn.silu(rg)
    r = jnp.einsum('btc,cd->btd', r_in, w_p_ret)
    m = jnp.einsum('btc,cd->btd', mem_o.reshape(B, T, W_MEM) * jax.nn.silu(mg), w_p_mem)
    merged = (gates[:, :, 0] * a + gates[:, :, 1] * r + gates[:, :, 2] * m).astype(x.dtype)
    return x + jnp.einsum('btd,de->bte', merged, w_out)


def setup_inputs(seed: int = 0) -> dict:
    key = jax.random.key(seed)
    ks = jax.random.split(key, 24)
    f32 = jnp.float32

    def nrm(k, shape, scale):
        return jax.random.normal(k, shape, f32) * scale

    return {
        'x_prompt': nrm(ks[0], (BATCH, SEQ, D_MODEL), 1.0),
        'x_sample': nrm(ks[1], (DEC_BATCH, DEC_SEQ, D_MODEL), 1.0),
        'mem_prompt': nrm(ks[2], (BATCH, N_MEM, D_MODEL), 1.0),
        'cache_fox_k': nrm(ks[3], (DEPTH, DEC_BATCH, PAST_LEN, H_FOX, DH_FOX), 1.0),
        'cache_fox_v': nrm(ks[4], (DEPTH, DEC_BATCH, PAST_LEN, H_FOX, DH_FOX), 1.0),
        'cache_fox_logf': jax.nn.log_sigmoid(FORGET_BIAS + nrm(ks[5], (DEPTH, DEC_BATCH, PAST_LEN, H_FOX), 1.0)),
        'state_ret': nrm(ks[6], (DEPTH, DEC_BATCH, H_RET, DK_RET, DV_RET), 0.5),
        'cache_mem_k': nrm(ks[7], (DEPTH, DEC_BATCH, N_MEM, H_MEM, DH_MEM), 1.0),
        'cache_mem_v': nrm(ks[8], (DEPTH, DEC_BATCH, N_MEM, H_MEM, DH_MEM), 1.0),
        'g_norm': 1.0 + nrm(ks[9], (DEPTH, D_MODEL), 0.02),
        'g_mem_norm': 1.0 + nrm(ks[10], (DEPTH, D_MODEL), 0.02),
        'w_in': nrm(ks[11], (DEPTH, D_MODEL, D_IN), D_MODEL ** -0.5),
        'b_f': FORGET_BIAS + nrm(ks[12], (DEPTH, H_FOX), 0.1),
        'b_merge': nrm(ks[13], (DEPTH, N_BRANCH * D_MODEL), 0.02),
        'g_fox_q': 1.0 + nrm(ks[14], (DEPTH, DH_FOX), 0.02),
        'g_fox_k': 1.0 + nrm(ks[15], (DEPTH, DH_FOX), 0.02),
        'g_mem_q': 1.0 + nrm(ks[16], (DEPTH, DH_MEM), 0.02),
        'g_mem_k': 1.0 + nrm(ks[17], (DEPTH, DH_MEM), 0.02),
        'w_mem_kv': nrm(ks[18], (DEPTH, D_MODEL, 2 * W_MEM), D_MODEL ** -0.5),
        'w_p_fox': nrm(ks[19], (DEPTH, W_FOX, D_MODEL), W_FOX ** -0.5),
        'w_p_ret': nrm(ks[20], (DEPTH, W_RET_V, D_MODEL), W_RET_V ** -0.5),
        'w_p_mem': nrm(ks[21], (DEPTH, W_MEM, D_MODEL), W_MEM ** -0.5),
        'w_out': nrm(ks[22], (DEPTH, D_MODEL, D_MODEL), D_MODEL ** -0.5),
    }


def reference(x_prompt, x_sample, mem_prompt, cache_fox_k, cache_fox_v, cache_fox_logf, state_ret, cache_mem_k, cache_mem_v,
              g_norm, g_mem_norm, w_in, b_f, b_merge, g_fox_q, g_fox_k, g_mem_q, g_mem_k, w_mem_kv,
              w_p_fox, w_p_ret, w_p_mem, w_out):
    log_g = ret_log_gamma()
    t_p = x_prompt.shape[1]
    t_s = x_sample.shape[1]
    past = cache_fox_k.shape[2]
    pos_p = jnp.arange(t_p, dtype=jnp.int32)
    pos_s = past + jnp.arange(t_s, dtype=jnp.int32)
    y_p, y_s = x_prompt, x_sample
    fk_p, fv_p, lf_p, sr_p, mk_p, mv_p = [], [], [], [], [], []
    fk_s, fv_s, lf_s, sr_s = [], [], [], []
    for l in range(DEPTH):
        w_mix = (g_norm[l], w_in[l], b_f[l], b_merge[l], g_fox_q[l], g_fox_k[l], g_mem_q[l])
        w_outs = (w_p_fox[l], w_p_ret[l], w_p_mem[l], w_out[l])
        fq, fk, fv, logf, rq, rk, rv, mq, fg, rg, mg, gates = mixer_inputs(y_p, pos_p, *w_mix)
        c = jnp.cumsum(logf, axis=1)
        fox_o = fox_prompt(fq, fk, fv, c)
        ret_o, s_fin = retention_prompt(rq, rk, rv, log_g)
        mk, mv = memory_kv(mem_prompt, g_mem_norm[l], w_mem_kv[l], g_mem_k[l])
        mem_o = memory_attend(mq, mk, mv)
        y_p = mixer_output(y_p, fox_o, fg, ret_o, rg, mem_o, mg, gates, *w_outs)
        fk_p.append(fk)
        fv_p.append(fv)
        lf_p.append(logf)
        sr_p.append(s_fin.astype(x_prompt.dtype))
        mk_p.append(mk)
        mv_p.append(mv)
        fq, fk, fv, logf, rq, rk, rv, mq, fg, rg, mg, gates = mixer_inputs(y_s, pos_s, *w_mix)
        k_all = jnp.concatenate([cache_fox_k[l].astype(fk.dtype), fk], axis=1)
        v_all = jnp.concatenate([cache_fox_v[l].astype(fv.dtype), fv], axis=1)
        c_all = jnp.cumsum(jnp.concatenate([cache_fox_logf[l].astype(jnp.float32), logf], axis=1), axis=1)
        fox_o = fox_attend(fq, c_all[:, past:], pos_s, k_all, v_all, c_all)
        ret_o, s_new = retention_block(rq.astype(jnp.float32), rk.astype(jnp.float32), rv.astype(jnp.float32),
                                       state_ret[l].astype(jnp.float32), log_g)
        mem_o = memory_attend(mq, cache_mem_k[l], cache_mem_v[l])
        y_s = mixer_output(y_s, fox_o, fg, ret_o, rg, mem_o, mg, gates, *w_outs)
        fk_s.append(fk)
        fv_s.append(fv)
        lf_s.append(logf)
        sr_s.append(s_new.astype(x_sample.dtype))
    return (y_p, y_s,
            jnp.stack(fk_p, 0), jnp.stack(fv_p, 0), jnp.stack(lf_p, 0), jnp.stack(sr_p, 0), jnp.stack(mk_p, 0), jnp.stack(mv_p, 0),
            jnp.stack(fk_s, 0), jnp.stack(fv_s, 0), jnp.stack(lf_s, 0), jnp.stack(sr_s, 0))
```

```python
import functools

import jax
import jax.numpy as jnp
from jax import lax
from jax.experimental import pallas as pl
from jax.experimental.pallas import tpu as pltpu

D_MODEL = 1024
N_MEM = 256
H_FOX = 8
DH_FOX = 64
H_RET = 4
DK_RET = 128
DV_RET = 256
H_MEM = 4
DH_MEM = 128
W_FOX = H_FOX * DH_FOX
W_RET_QK = H_RET * DK_RET
W_RET_V = H_RET * DV_RET
W_MEM = H_MEM * DH_MEM
N_BRANCH = 3
ROPE_BASE = 10000.0
EPS = 1e-6

LANES = 128

OFF_FQ = 0
OFF_FK = OFF_FQ + W_FOX
OFF_FV = OFF_FK + W_FOX
OFF_FG = OFF_FV + W_FOX
OFF_RQ = OFF_FG + W_FOX
OFF_RK = OFF_RQ + W_RET_QK
OFF_RV = OFF_RK + W_RET_QK
OFF_RG = OFF_RV + W_RET_V
OFF_MQ = OFF_RG + W_RET_V
OFF_MG = OFF_MQ + W_MEM
OFF_GL = OFF_MG + W_MEM
OFF_FF = OFF_GL + N_BRANCH * D_MODEL
W_MAIN = OFF_FF + LANES

PROMPT_BLOCK = 256
VMEM_LIMIT_BYTES = 60 * 1024 * 1024

F32 = jnp.float32
BF16 = jnp.bfloat16


def _mm(a, b):
    return jnp.dot(a, b, preferred_element_type=F32)


def _mm_nt(a, b):
    return lax.dot_general(a, b, (((1,), (1,)), ((), ())), preferred_element_type=F32)


def _idiv(x, d):
    assert d & (d - 1) == 0
    return lax.shift_right_logical(x, d.bit_length() - 1)


def _imod(x, d):
    assert d & (d - 1) == 0
    return jnp.bitwise_and(x, d - 1)


def _rms_rows(x, g):
    ms = jnp.mean(x * x, axis=-1, keepdims=True)
    return x * lax.rsqrt(ms + EPS) * g


def _head_norm64(z, g, seg_ref):
    ss = _mm((z * z).astype(BF16), seg_ref[...])
    return z * lax.rsqrt(ss * (1.0 / DH_FOX) + EPS) * g


def _log_sigmoid(u):
    return jnp.minimum(u, 0.0) - jnp.log1p(jnp.exp(-jnp.abs(u)))


def _silu(u):
    return u * jax.nn.sigmoid(u)


def _split3(a):
    hi = a.astype(BF16)
    r1 = a - hi.astype(F32)
    mid = r1.astype(BF16)
    lo = (r1 - mid.astype(F32)).astype(BF16)
    return hi, mid, lo


def _tri_cumsum(tri, a):
    hi, mid, lo = _split3(a)
    return _mm(tri, hi) + _mm(tri, mid) + _mm(tri, lo)


def _cumsum_lanes(triu, a):
    hi, mid, lo = _split3(a)
    return _mm(hi, triu) + _mm(mid, triu) + _mm(lo, triu)


def _rope(x, cos, sin_signed):
    return x * cos + pltpu.roll(x, DK_RET // 2, 1) * sin_signed


def _group_norm(o):
    mu = jnp.mean(o, axis=-1, keepdims=True)
    d = o - mu
    var = jnp.mean(d * d, axis=-1, keepdims=True)
    return d * lax.rsqrt(var + EPS)


def _gate(h, w_ref, bm_ref, i):
    z = _mm(h, w_ref[:, OFF_GL + i * D_MODEL:OFF_GL + (i + 1) * D_MODEL])
    return jax.nn.sigmoid(z + bm_ref[:, i * D_MODEL:(i + 1) * D_MODEL])


def _memkv_kernel(mem_ref, gmn_ref, w_ref, gmk_ref, mk_ref, mv_ref, mkt_ref, mvb_ref):
    h = _rms_rows(mem_ref[0], gmn_ref[...]).astype(BF16)
    z = _mm(h, w_ref[...])
    ks = []
    for hh in range(H_MEM):
        ks.append(_rms_rows(z[:, hh * DH_MEM:(hh + 1) * DH_MEM], gmk_ref[...]))
    mk = jnp.concatenate(ks, axis=1)
    mv = z[:, W_MEM:]
    mk_ref[0] = mk
    mv_ref[0] = mv
    mkt_ref[0] = mk.T.astype(BF16)
    mvb_ref[0] = mv.astype(BF16)


def _memkv_call(mem, g_mem_norm, w_mem_kv, g_mem_k):
    B = mem.shape[0]
    const = lambda b: (0, 0)
    per_b = lambda b: (b, 0, 0)
    return pl.pallas_call(
        _memkv_kernel,
        grid=(B,),
        in_specs=[
            pl.BlockSpec((1, N_MEM, D_MODEL), per_b),
            pl.BlockSpec((1, D_MODEL), const),
            pl.BlockSpec((D_MODEL, 2 * W_MEM), const),
            pl.BlockSpec((1, DH_MEM), const),
        ],
        out_specs=[
            pl.BlockSpec((1, N_MEM, W_MEM), per_b),
            pl.BlockSpec((1, N_MEM, W_MEM), per_b),
            pl.BlockSpec((1, W_MEM, N_MEM), per_b),
            pl.BlockSpec((1, N_MEM, W_MEM), per_b),
        ],
        out_shape=[
            jax.ShapeDtypeStruct((B, N_MEM, W_MEM), F32),
            jax.ShapeDtypeStruct((B, N_MEM, W_MEM), F32),
            jax.ShapeDtypeStruct((B, W_MEM, N_MEM), BF16),
            jax.ShapeDtypeStruct((B, N_MEM, W_MEM), BF16),
        ],
        compiler_params=pltpu.CompilerParams(dimension_semantics=("arbitrary",)),
        name="memkv",
    )(mem, g_mem_norm.reshape(1, D_MODEL), w_mem_kv.astype(BF16), g_mem_k.reshape(1, DH_MEM))


def _prompt_kernel(lg_ref, x_ref, cos_ref, sin_ref, mkt_ref, mvb_ref, gn_ref, bf_ref, bm_ref, gfq_ref, gfk_ref,
                   gmq_ref, seg_ref, w_ref, wpf_ref, wpr_ref, wpm_ref, wo_ref,
                   y_ref, fk_ref, fv_ref, lf_ref, st_ref,
                   kt_scr, v_scr, nct_scr, ccar_scr, dmask_scr, decq_scr, deck_scr, ssc_scr, *, tb):
    b = pl.program_id(0)
    j = pl.program_id(1)
    t0 = pl.multiple_of(j * tb, tb)

    @pl.when(jnp.logical_and(b == 0, j == 0))
    def _init_tables():
        ri = lax.broadcasted_iota(jnp.int32, (tb, tb), 0)
        ci = lax.broadcasted_iota(jnp.int32, (tb, tb), 1)
        diff = (ri - ci).astype(F32)
        rowi = lax.broadcasted_iota(jnp.int32, (tb, 1), 0).astype(F32)
        coli = lax.broadcasted_iota(jnp.int32, (1, tb), 1).astype(F32)
        for hh in range(H_RET):
            lg = lg_ref[hh]
            dmask_scr[hh] = jnp.where(diff >= 0.0, jnp.exp(lg * jnp.maximum(diff, 0.0)), 0.0)
            decq_scr[hh] = jnp.exp(lg * (rowi + 1.0))
            deck_scr[hh] = jnp.exp(lg * (tb - 1.0 - coli))
            ssc_scr[hh] = jnp.exp(jnp.full((1, DV_RET), lg * tb, F32))

    @pl.when(j == 0)
    def _init_batch():
        ccar_scr[...] = jnp.zeros_like(ccar_scr)
        st_ref[...] = jnp.zeros_like(st_ref)

    x = x_ref[0]
    h = _rms_rows(x, gn_ref[...]).astype(BF16)

    fq = _head_norm64(_mm(h, w_ref[:, OFF_FQ:OFF_FQ + W_FOX]), gfq_ref[...], seg_ref) * (DH_FOX ** -0.5)
    fk = _head_norm64(_mm(h, w_ref[:, OFF_FK:OFF_FK + W_FOX]), gfk_ref[...], seg_ref)
    fv = _mm(h, w_ref[:, OFF_FV:OFF_FV + W_FOX])
    logf = _log_sigmoid(_mm(h, w_ref[:, OFF_FF:OFF_FF + LANES]) + bf_ref[...])
    fk_ref[0] = fk
    fv_ref[0] = fv
    lf_ref[0] = logf[:, :H_FOX]

    ri = lax.broadcasted_iota(jnp.int32, (tb, tb), 0)
    ci = lax.broadcasted_iota(jnp.int32, (tb, tb), 1)
    causal = ci <= ri
    tril = jnp.where(causal, 1.0, 0.0).astype(BF16)
    cblk = _tri_cumsum(tril, logf) + ccar_scr[...]
    ccar_scr[...] = cblk[tb - 1:tb, :]
    nct_scr[:, pl.ds(t0, tb)] = -(cblk.T[:H_FOX, :])
    kt_scr[:, pl.ds(t0, tb)] = fk.T.astype(BF16)
    for hh in range(H_FOX):
        v_scr[hh, pl.ds(t0, tb), :] = fv[:, hh * DH_FOX:(hh + 1) * DH_FOX].astype(BF16)

    fo_parts = []
    for hh in range(H_FOX):
        q_h = fq[:, hh * DH_FOX:(hh + 1) * DH_FOX].astype(BF16)
        rows = slice(hh * DH_FOX, (hh + 1) * DH_FOX)
        s = _mm(q_h, kt_scr[rows, pl.ds(t0, tb)]) + nct_scr[hh:hh + 1, pl.ds(t0, tb)]
        s = jnp.where(causal, s, -jnp.inf)
        m0 = jnp.max(s, axis=-1, keepdims=True)
        p = jnp.exp(s - m0)
        l0 = jnp.sum(p, axis=-1, keepdims=True)
        a0 = _mm(p.astype(BF16), v_scr[hh, pl.ds(t0, tb), :])

        def body(kb, carry, q_h=q_h, rows=rows, hh=hh):
            m, l, acc = carry
            off = pl.multiple_of(kb * tb, tb)
            s = _mm(q_h, kt_scr[rows, pl.ds(off, tb)]) + nct_scr[hh:hh + 1, pl.ds(off, tb)]
            m_new = jnp.maximum(m, jnp.max(s, axis=-1, keepdims=True))
            alpha = jnp.exp(m - m_new)
            p = jnp.exp(s - m_new)
            l = alpha * l + jnp.sum(p, axis=-1, keepdims=True)
            acc = alpha * acc + _mm(p.astype(BF16), v_scr[hh, pl.ds(off, tb), :])
            return m_new, l, acc

        _, l, acc = lax.fori_loop(0, j, body, (m0, l0, a0))
        fo_parts.append(acc / l)
    fo = jnp.concatenate(fo_parts, axis=1)
    a_in = (fo * _silu(_mm(h, w_ref[:, OFF_FG:OFF_FG + W_FOX]))).astype(BF16)
    merged = _gate(h, w_ref, bm_ref, 0) * _mm(a_in, wpf_ref[...])

    zrq = _mm(h, w_ref[:, OFF_RQ:OFF_RQ + W_RET_QK])
    zrk = _mm(h, w_ref[:, OFF_RK:OFF_RK + W_RET_QK])
    zrv = _mm(h, w_ref[:, OFF_RV:OFF_RV + W_RET_V])
    zrg = _mm(h, w_ref[:, OFF_RG:OFF_RG + W_RET_V])
    cos = cos_ref[...]
    sin = sin_ref[...]
    r_parts = []
    for hh in range(H_RET):
        q = _rope(zrq[:, hh * DK_RET:(hh + 1) * DK_RET], cos, sin).astype(BF16)
        k = _rope(zrk[:, hh * DK_RET:(hh + 1) * DK_RET], cos, sin) * (DK_RET ** -0.5)
        kt = k.T
        v = zrv[:, hh * DV_RET:(hh + 1) * DV_RET].astype(BF16)
        state = st_ref[0, hh]
        inter = _mm(q, state.astype(BF16)) * decq_scr[hh]
        scores = _mm(q, kt.astype(BF16)) * dmask_scr[hh]
        o = inter + _mm(scores.astype(BF16), v)
        st_ref[0, hh] = ssc_scr[hh] * state + _mm((kt * deck_scr[hh]).astype(BF16), v)
        r_parts.append(_group_norm(o) * _silu(zrg[:, hh * DV_RET:(hh + 1) * DV_RET]))
    r_in = jnp.concatenate(r_parts, axis=1).astype(BF16)
    merged = merged + _gate(h, w_ref, bm_ref, 1) * _mm(r_in, wpr_ref[...])

    zmq = _mm(h, w_ref[:, OFF_MQ:OFF_MQ + W_MEM])
    m_parts = []
    for hh in range(H_MEM):
        cols = slice(hh * DH_MEM, (hh + 1) * DH_MEM)
        q = _rms_rows(zmq[:, cols], gmq_ref[...]).astype(BF16)
        s = _mm(q, mkt_ref[0, cols, :]) * (DH_MEM ** -0.5)
        p = jnp.exp(s - jnp.max(s, axis=-1, keepdims=True))
        l = jnp.sum(p, axis=-1, keepdims=True)
        m_parts.append(_mm(p.astype(BF16), mvb_ref[0, :, cols]) / l)
    mo = jnp.concatenate(m_parts, axis=1)
    m_in = (mo * _silu(_mm(h, w_ref[:, OFF_MG:OFF_MG + W_MEM]))).astype(BF16)
    merged = merged + _gate(h, w_ref, bm_ref, 2) * _mm(m_in, wpm_ref[...])

    y_ref[0] = x + _mm(merged.astype(BF16), wo_ref[...])


def _resident(shape):
    nd = len(shape)
    return pl.BlockSpec(shape, lambda b, j: (0,) * nd, pipeline_mode=pl.Buffered(1))


def _prompt_call(x, cos, sin, mkt, mvb, log_g, g_norm, bf_pad, b_merge, gfq, gfk, gmq, seg, w_main, wpf, wpr, wpm, wo,
                 tb=PROMPT_BLOCK):
    B, T, _ = x.shape
    nt = T // tb
    kernel = functools.partial(_prompt_kernel, tb=tb)
    in_specs = [
        pl.BlockSpec(memory_space=pltpu.SMEM),
        pl.BlockSpec((1, tb, D_MODEL), lambda b, j: (b, j, 0)),
        pl.BlockSpec((tb, DK_RET), lambda b, j: (j, 0)),
        pl.BlockSpec((tb, DK_RET), lambda b, j: (j, 0)),
        pl.BlockSpec((1, W_MEM, N_MEM), lambda b, j: (b, 0, 0)),
        pl.BlockSpec((1, N_MEM, W_MEM), lambda b, j: (b, 0, 0)),
        _resident((1, D_MODEL)),
        _resident((1, LANES)),
        _resident((1, N_BRANCH * D_MODEL)),
        _resident((1, W_FOX)),
        _resident((1, W_FOX)),
        _resident((1, DH_MEM)),
        _resident((W_FOX, W_FOX)),
        _resident((D_MODEL, W_MAIN)),
        _resident((W_FOX, D_MODEL)),
        _resident((W_RET_V, D_MODEL)),
        _resident((W_MEM, D_MODEL)),
        _resident((D_MODEL, D_MODEL)),
    ]
    out_specs = [
        pl.BlockSpec((1, tb, D_MODEL), lambda b, j: (b, j, 0)),
        pl.BlockSpec((1, tb, W_FOX), lambda b, j: (b, j, 0)),
        pl.BlockSpec((1, tb, W_FOX), lambda b, j: (b, j, 0)),
        pl.BlockSpec((1, tb, H_FOX), lambda b, j: (b, j, 0)),
        pl.BlockSpec((1, H_RET, DK_RET, DV_RET), lambda b, j: (b, 0, 0, 0)),
    ]
    out_shape = [
        jax.ShapeDtypeStruct((B, T, D_MODEL), F32),
        jax.ShapeDtypeStruct((B, T, W_FOX), F32),
        jax.ShapeDtypeStruct((B, T, W_FOX), F32),
        jax.ShapeDtypeStruct((B, T, H_FOX), F32),
        jax.ShapeDtypeStruct((B, H_RET, DK_RET, DV_RET), F32),
    ]
    scratch = [
        pltpu.VMEM((W_FOX, T), BF16),
        pltpu.VMEM((H_FOX, T, DH_FOX), BF16),
        pltpu.VMEM((H_FOX, T), F32),
        pltpu.VMEM((1, LANES), F32),
        pltpu.VMEM((H_RET, tb, tb), F32),
        pltpu.VMEM((H_RET, tb, 1), F32),
        pltpu.VMEM((H_RET, 1, tb), F32),
        pltpu.VMEM((H_RET, 1, DV_RET), F32),
    ]
    return pl.pallas_call(
        kernel,
        grid=(B, nt),
        in_specs=in_specs,
        out_specs=out_specs,
        out_shape=out_shape,
        scratch_shapes=scratch,
        compiler_params=pltpu.CompilerParams(
            dimension_semantics=("arbitrary", "arbitrary"), vmem_limit_bytes=VMEM_LIMIT_BYTES),
        name="prompt",
    )(log_g, x, cos, sin, mkt, mvb, g_norm, bf_pad, b_merge, gfq, gfk, gmq, seg, w_main, wpf, wpr, wpm, wo)


def _sample_kernel(lg_ref, x_ref, cos_ref, sin_ref, ck_ref, cv_ref, clf_ref, st0_ref, cmk_ref, cmv_ref,
                   gn_ref, bf_ref, bm_ref, gfq_ref, gfk_ref, gmq_ref, seg_ref, w_ref, wpf_ref, wpr_ref, wpm_ref,
                   wo_ref,
                   y_ref, fk_ref, fv_ref, lf_ref, st_ref,
                   h_scr, fq_scr, knt_scr, vn_scr, ncn_scr, rq_scr, rkt_scr, rv_scr, mq_scr, fo_scr, ro_scr, mo_scr,
                   *, nb, ts, past):
    b = pl.program_id(0)
    n = nb * ts
    r0 = pl.multiple_of(b * ts, ts)

    @pl.when(b == 0)
    def _project_all():
        h = _rms_rows(x_ref[...], gn_ref[...]).astype(BF16)
        h_scr[...] = h
        fq_scr[...] = _head_norm64(_mm(h, w_ref[:, OFF_FQ:OFF_FQ + W_FOX]), gfq_ref[...], seg_ref) * (DH_FOX ** -0.5)
        fk = _head_norm64(_mm(h, w_ref[:, OFF_FK:OFF_FK + W_FOX]), gfk_ref[...], seg_ref)
        fv = _mm(h, w_ref[:, OFF_FV:OFF_FV + W_FOX])
        logf = _log_sigmoid(_mm(h, w_ref[:, OFF_FF:OFF_FF + LANES]) + bf_ref[...])
        fk_ref[...] = fk
        fv_ref[...] = fv
        lf_ref[...] = logf[:, :H_FOX]
        knt_scr[...] = fk.T.astype(BF16)
        vn_scr[...] = fv.astype(BF16)
        ri = lax.broadcasted_iota(jnp.int32, (n, n), 0)
        ci = lax.broadcasted_iota(jnp.int32, (n, n), 1)
        same = _idiv(ri, ts) == _idiv(ci, ts)
        tril = jnp.where(jnp.logical_and(same, ci <= ri), 1.0, 0.0).astype(BF16)
        ncn_scr[...] = -(_tri_cumsum(tril, logf).T)
        cos = cos_ref[...]
        sin = sin_ref[...]
        zrq = _mm(h, w_ref[:, OFF_RQ:OFF_RQ + W_RET_QK])
        zrk = _mm(h, w_ref[:, OFF_RK:OFF_RK + W_RET_QK])
        for hh in range(H_RET):
            cols = slice(hh * DK_RET, (hh + 1) * DK_RET)
            rq_scr[:, cols] = _rope(zrq[:, cols], cos, sin)
            rkt_scr[cols, :] = (_rope(zrk[:, cols], cos, sin) * (DK_RET ** -0.5)).T
        rv_scr[...] = _mm(h, w_ref[:, OFF_RV:OFF_RV + W_RET_V]).astype(BF16)
        zmq = _mm(h, w_ref[:, OFF_MQ:OFF_MQ + W_MEM])
        for hh in range(H_MEM):
            cols = slice(hh * DH_MEM, (hh + 1) * DH_MEM)
            mq_scr[:, cols] = _rms_rows(zmq[:, cols], gmq_ref[...])

    nr = H_FOX * ts
    fq_b = fq_scr[pl.ds(r0, ts), :]
    row_head = _idiv(lax.broadcasted_iota(jnp.int32, (nr, W_FOX), 0), ts)
    col_head = _idiv(lax.broadcasted_iota(jnp.int32, (nr, W_FOX), 1), DH_FOX)
    head_sel = row_head == col_head
    q_bd = jnp.where(head_sel, jnp.concatenate([fq_b] * H_FOX, axis=0), 0.0).astype(BF16)

    cc = 256
    ri = lax.broadcasted_iota(jnp.int32, (cc, cc), 0)
    ci = lax.broadcasted_iota(jnp.int32, (cc, cc), 1)
    triu = jnp.where(ri <= ci, 1.0, 0.0).astype(BF16)
    carry = jnp.zeros((H_FOX, 1), F32)
    c_chunks = []
    for c in range(past // cc):
        cch = _cumsum_lanes(triu, clf_ref[0, :, c * cc:(c + 1) * cc]) + carry
        carry = cch[:, cc - 1:cc]
        c_chunks.append(cch)
    c_past = jnp.concatenate(c_chunks, axis=1)
    ncp = jnp.concatenate([jnp.broadcast_to(-c_past[hh:hh + 1, :], (ts, past)) for hh in range(H_FOX)], axis=0)
    ncn = ncn_scr[:H_FOX, :] - carry
    ncn = jnp.concatenate([jnp.broadcast_to(ncn[hh:hh + 1, :], (ts, n)) for hh in range(H_FOX)], axis=0)

    s_p = _mm_nt(q_bd, ck_ref[0].astype(BF16)) + ncp
    s_n = _mm(q_bd, knt_scr[...]) + ncn
    tok = lax.broadcasted_iota(jnp.int32, (nr, n), 1)
    qt = _imod(lax.broadcasted_iota(jnp.int32, (nr, n), 0), ts)
    valid = jnp.logical_and(_idiv(tok, ts) == b, _imod(tok, ts) <= qt)
    s_n = jnp.where(valid, s_n, -jnp.inf)
    m = jnp.maximum(jnp.max(s_p, axis=-1, keepdims=True), jnp.max(s_n, axis=-1, keepdims=True))
    p_p = jnp.exp(s_p - m)
    p_n = jnp.exp(s_n - m)
    l = jnp.sum(p_p, axis=-1, keepdims=True) + jnp.sum(p_n, axis=-1, keepdims=True)
    o_bd = (_mm(p_p.astype(BF16), cv_ref[0].astype(BF16)) + _mm(p_n.astype(BF16), vn_scr[...])) / l
    o_bd = jnp.where(head_sel, o_bd, 0.0)
    fo = o_bd[0:ts, :]
    for hh in range(1, H_FOX):
        fo = fo + o_bd[hh * ts:(hh + 1) * ts, :]
    fo_scr[pl.ds(r0, ts), :] = fo

    tokr = lax.broadcasted_iota(jnp.int32, (ts, n), 1)
    qtr = lax.broadcasted_iota(jnp.int32, (ts, n), 0)
    in_b = _idiv(tokr, ts) == b
    dt = (qtr - _imod(tokr, ts)).astype(F32)
    tokc = lax.broadcasted_iota(jnp.int32, (1, n), 1)
    kpos = _imod(tokc, ts).astype(F32)
    rowi = lax.broadcasted_iota(jnp.int32, (ts, 1), 0).astype(F32)
    r_parts = []
    for hh in range(H_RET):
        lg = lg_ref[hh]
        q = rq_scr[pl.ds(r0, ts), hh * DK_RET:(hh + 1) * DK_RET].astype(BF16)
        kt = rkt_scr[hh * DK_RET:(hh + 1) * DK_RET, :]
        v = rv_scr[:, hh * DV_RET:(hh + 1) * DV_RET]
        state = st0_ref[0, hh]
        dmask = jnp.where(jnp.logical_and(in_b, dt >= 0.0), jnp.exp(lg * jnp.maximum(dt, 0.0)), 0.0)
        inter = _mm(q, state.astype(BF16)) * jnp.exp(lg * (rowi + 1.0))
        scores = _mm(q, kt.astype(BF16)) * dmask
        r_parts.append(inter + _mm(scores.astype(BF16), v))
        deck = jnp.where(_idiv(tokc, ts) == b,jnp.exp(lg * (ts - 1.0 - kpos)), 0.0)
        st_ref[0, hh] = jnp.exp(jnp.full((1, DV_RET), lg * ts, F32)) * state + _mm((kt * deck).astype(BF16), v)
    ro_scr[pl.ds(r0, ts), :] = jnp.concatenate(r_parts, axis=1)

    m_parts = []
    for hh in range(H_MEM):
        cols = slice(hh * DH_MEM, (hh + 1) * DH_MEM)
        q = mq_scr[pl.ds(r0, ts), cols].astype(BF16)
        s = _mm_nt(q, cmk_ref[0, :, cols].astype(BF16)) * (DH_MEM ** -0.5)
        p = jnp.exp(s - jnp.max(s, axis=-1, keepdims=True))
        l = jnp.sum(p, axis=-1, keepdims=True)
        m_parts.append(_mm(p.astype(BF16), cmv_ref[0, :, cols].astype(BF16)) / l)
    mo_scr[pl.ds(r0, ts), :] = jnp.concatenate(m_parts, axis=1)

    @pl.when(b == nb - 1)
    def _output_all():
        h = h_scr[...]
        a_in = (fo_scr[...] * _silu(_mm(h, w_ref[:, OFF_FG:OFF_FG + W_FOX]))).astype(BF16)
        merged = _gate(h, w_ref, bm_ref, 0) * _mm(a_in, wpf_ref[...])
        zrg = _mm(h, w_ref[:, OFF_RG:OFF_RG + W_RET_V])
        ro = ro_scr[...]
        r_parts = []
        for hh in range(H_RET):
            cols = slice(hh * DV_RET, (hh + 1) * DV_RET)
            r_parts.append(_group_norm(ro[:, cols]) * _silu(zrg[:, cols]))
        r_in = jnp.concatenate(r_parts, axis=1).astype(BF16)
        merged = merged + _gate(h, w_ref, bm_ref, 1) * _mm(r_in, wpr_ref[...])
        m_in = (mo_scr[...] * _silu(_mm(h, w_ref[:, OFF_MG:OFF_MG + W_MEM]))).astype(BF16)
        merged = merged + _gate(h, w_ref, bm_ref, 2) * _mm(m_in, wpm_ref[...])
        y_ref[...] = x_ref[...] + _mm(merged.astype(BF16), wo_ref[...])


def _sample_call(x2d, cos, sin, ck, cv, clf_t, st0, cmk, cmv, log_g, g_norm, bf_pad, b_merge, gfq, gfk, gmq, seg,
                 w_main, wpf, wpr, wpm, wo, nb, ts):
    n = nb * ts
    past = ck.shape[1]
    kernel = functools.partial(_sample_kernel, nb=nb, ts=ts, past=past)

    def res(shape):
        nd = len(shape)
        return pl.BlockSpec(shape, lambda b: (0,) * nd, pipeline_mode=pl.Buffered(1))

    in_specs = [
        pl.BlockSpec(memory_space=pltpu.SMEM),
        res((n, D_MODEL)),
        res((n, DK_RET)),
        res((n, DK_RET)),
        pl.BlockSpec((1, past, W_FOX), lambda b: (b, 0, 0)),
        pl.BlockSpec((1, past, W_FOX), lambda b: (b, 0, 0)),
        pl.BlockSpec((1, H_FOX, past), lambda b: (b, 0, 0)),
        pl.BlockSpec((1, H_RET, DK_RET, DV_RET), lambda b: (b, 0, 0, 0)),
        pl.BlockSpec((1, N_MEM, W_MEM), lambda b: (b, 0, 0)),
        pl.BlockSpec((1, N_MEM, W_MEM), lambda b: (b, 0, 0)),
        res((1, D_MODEL)),
        res((1, LANES)),
        res((1, N_BRANCH * D_MODEL)),
        res((1, W_FOX)),
        res((1, W_FOX)),
        res((1, DH_MEM)),
        res((W_FOX, W_FOX)),
        res((D_MODEL, W_MAIN)),
        res((W_FOX, D_MODEL)),
        res((W_RET_V, D_MODEL)),
        res((W_MEM, D_MODEL)),
        res((D_MODEL, D_MODEL)),
    ]
    full = lambda shape: pl.BlockSpec(shape, lambda b: (0,) * len(shape))
    out_specs = [
        full((n, D_MODEL)),
        full((n, W_FOX)),
        full((n, W_FOX)),
        full((n, H_FOX)),
        pl.BlockSpec((1, H_RET, DK_RET, DV_RET), lambda b: (b, 0, 0, 0)),
    ]
    out_shape = [
        jax.ShapeDtypeStruct((n, D_MODEL), F32),
        jax.ShapeDtypeStruct((n, W_FOX), F32),
        jax.ShapeDtypeStruct((n, W_FOX), F32),
        jax.ShapeDtypeStruct((n, H_FOX), F32),
        jax.ShapeDtypeStruct((nb, H_RET, DK_RET, DV_RET), F32),
    ]
    scratch = [
        pltpu.VMEM((n, D_MODEL), BF16),
        pltpu.VMEM((n, W_FOX), F32),
        pltpu.VMEM((W_FOX, n), BF16),
        pltpu.VMEM((n, W_FOX), BF16),
        pltpu.VMEM((LANES, n), F32),
        pltpu.VMEM((n, W_RET_QK), F32),
        pltpu.VMEM((W_RET_QK, n), F32),
        pltpu.VMEM((n, W_RET_V), BF16),
        pltpu.VMEM((n, W_MEM), F32),
        pltpu.VMEM((n, W_FOX), F32),
        pltpu.VMEM((n, W_RET_V), F32),
        pltpu.VMEM((n, W_MEM), F32),
    ]
    return pl.pallas_call(
        kernel,
        grid=(nb,),
        in_specs=in_specs,
        out_specs=out_specs,
        out_shape=out_shape,
        scratch_shapes=scratch,
        compiler_params=pltpu.CompilerParams(
            dimension_semantics=("arbitrary",), vmem_limit_bytes=VMEM_LIMIT_BYTES),
        name="sample",
    )(log_g, x2d, cos, sin, ck, cv, clf_t, st0, cmk, cmv, g_norm, bf_pad, b_merge, gfq, gfk, gmq, seg,
      w_main, wpf, wpr, wpm, wo)


def _rope_tables(pos):
    half = DK_RET // 2
    inv = ROPE_BASE ** (-jnp.arange(half, dtype=F32) / half)
    ang = pos.astype(F32)[:, None] * inv[None, :]
    cos = jnp.cos(ang)
    sin = jnp.sin(ang)
    return jnp.concatenate([cos, cos], axis=1), jnp.concatenate([-sin, sin], axis=1)


def _reorder_w_in(w):
    f0 = 3 * W_FOX
    pad = jnp.zeros((D_MODEL, LANES - H_FOX), w.dtype)
    return jnp.concatenate([w[:, :f0], w[:, f0 + H_FOX:], w[:, f0:f0 + H_FOX], pad], axis=1).astype(BF16)


def kernel(x_prompt, x_sample, mem_prompt, cache_fox_k, cache_fox_v, cache_fox_logf, state_ret, cache_mem_k, cache_mem_v,
           g_norm, g_mem_norm, w_in, b_f, b_merge, g_fox_q, g_fox_k, g_mem_q, g_mem_k, w_mem_kv,
           w_p_fox, w_p_ret, w_p_mem, w_out):
    depth = w_in.shape[0]
    assert depth == 1, "single-layer kernel"
    B, T, _ = x_prompt.shape
    nb, ts, _ = x_sample.shape
    past = cache_fox_k.shape[2]

    log_g = jnp.log1p(-jnp.exp2(-5.0 - jnp.arange(H_RET, dtype=F32)))
    head_of_lane = jnp.arange(W_FOX) // DH_FOX
    seg = (head_of_lane[:, None] == head_of_lane[None, :]).astype(BF16)
    w_main = _reorder_w_in(w_in[0])
    gn = g_norm[0].reshape(1, D_MODEL)
    bf_pad = jnp.concatenate([b_f[0], jnp.zeros((LANES - H_FOX,), F32)]).reshape(1, LANES)
    bm = b_merge[0].reshape(1, N_BRANCH * D_MODEL)
    gfq = jnp.tile(g_fox_q[0], H_FOX).reshape(1, W_FOX)
    gfk = jnp.tile(g_fox_k[0], H_FOX).reshape(1, W_FOX)
    gmq = g_mem_q[0].reshape(1, DH_MEM)
    wpf = w_p_fox[0].astype(BF16)
    wpr = w_p_ret[0].astype(BF16)
    wpm = w_p_mem[0].astype(BF16)
    wo = w_out[0].astype(BF16)
    shared = (gn, bf_pad, bm, gfq, gfk, gmq, seg, w_main, wpf, wpr, wpm, wo)

    mk, mv, mkt, mvb = _memkv_call(mem_prompt, g_mem_norm[0], w_mem_kv[0], g_mem_k[0])

    cos_p, sin_p = _rope_tables(jnp.arange(T, dtype=jnp.int32))
    y_p, fk_p, fv_p, lf_p, st_p = _prompt_call(x_prompt, cos_p, sin_p, mkt, mvb, log_g, *shared)

    pos_s = past + jnp.arange(ts, dtype=jnp.int32)
    cos_s, sin_s = _rope_tables(jnp.tile(pos_s, nb))
    y_s, fk_s, fv_s, lf_s, st_s = _sample_call(
        x_sample.reshape(nb * ts, D_MODEL), cos_s, sin_s,
        cache_fox_k[0].reshape(nb, past, W_FOX), cache_fox_v[0].reshape(nb, past, W_FOX),
        jnp.swapaxes(cache_fox_logf[0], 1, 2), state_ret[0],
        cache_mem_k[0].reshape(nb, N_MEM, W_MEM), cache_mem_v[0].reshape(nb, N_MEM, W_MEM),
        log_g, *shared, nb=nb, ts=ts)

    return (y_p, y_s.reshape(nb, ts, D_MODEL),
            fk_p.reshape(1, B, T, H_FOX, DH_FOX), fv_p.reshape(1, B, T, H_FOX, DH_FOX), lf_p[None],
            st_p[None], mk.reshape(1, B, N_MEM, H_MEM, DH_MEM), mv.reshape(1, B, N_MEM, H_MEM, DH_MEM),
            fk_s.reshape(1, nb, ts, H_FOX, DH_FOX), fv_s.reshape(1, nb, ts, H_FOX, DH_FOX),
            lf_s.reshape(1, nb, ts, H_FOX), st_s[None])
```

```python
import functools

import jax
import jax.numpy as jnp
from jax import lax
from jax.experimental import pallas as pl
from jax.experimental.pallas import tpu as pltpu

D_MODEL = 1024
N_MEM = 256
H_FOX = 8
DH_FOX = 64
H_RET = 4
DK_RET = 128
DV_RET = 256
H_MEM = 4
DH_MEM = 128
W_FOX = H_FOX * DH_FOX
W_RET_QK = H_RET * DK_RET
W_RET_V = H_RET * DV_RET
W_MEM = H_MEM * DH_MEM
N_BRANCH = 3
ROPE_BASE = 10000.0
EPS = 1e-6
LOG2E = 1.4426950408889634

LANES = 128

OFF_FQ = 0
OFF_FK = OFF_FQ + W_FOX
OFF_FV = OFF_FK + W_FOX
OFF_FG = OFF_FV + W_FOX
OFF_RQ = OFF_FG + W_FOX
OFF_RK = OFF_RQ + W_RET_QK
OFF_RV = OFF_RK + W_RET_QK
OFF_RG = OFF_RV + W_RET_V
OFF_MQ = OFF_RG + W_RET_V
OFF_MG = OFF_MQ + W_MEM
OFF_GL = OFF_MG + W_MEM
OFF_FF = OFF_GL + N_BRANCH * D_MODEL
W_MAIN = OFF_FF + LANES

PROMPT_BLOCK = 256
VMEM_LIMIT_BYTES = 60 * 1024 * 1024

F32 = jnp.float32
BF16 = jnp.bfloat16


def _mm(a, b):
    return jnp.dot(a, b, preferred_element_type=F32)


def _mm_nt(a, b):
    return lax.dot_general(a, b, (((1,), (1,)), ((), ())), preferred_element_type=F32)


def _idiv(x, d):
    assert d & (d - 1) == 0
    return lax.shift_right_logical(x, d.bit_length() - 1)


def _imod(x, d):
    assert d & (d - 1) == 0
    return jnp.bitwise_and(x, d - 1)


def _rms_rows(x, g):
    ms = jnp.mean(x * x, axis=-1, keepdims=True)
    return x * lax.rsqrt(ms + EPS) * g


def _head_norm64(z, g, seg_ref):
    ss = _mm((z * z).astype(BF16), seg_ref[...])
    return z * lax.rsqrt(ss * (1.0 / DH_FOX) + EPS) * g


def _log_sigmoid(u):
    return jnp.minimum(u, 0.0) - jnp.log1p(jnp.exp(-jnp.abs(u)))


def _silu(u):
    return u * jax.nn.sigmoid(u)


def _split3(a):
    hi = a.astype(BF16)
    r1 = a - hi.astype(F32)
    mid = r1.astype(BF16)
    lo = (r1 - mid.astype(F32)).astype(BF16)
    return hi, mid, lo


def _tri_cumsum(tri, a):
    hi, mid, lo = _split3(a)
    return _mm(tri, hi) + _mm(tri, mid) + _mm(tri, lo)


def _cumsum_lanes(triu, a):
    hi, mid, lo = _split3(a)
    return _mm(hi, triu) + _mm(mid, triu) + _mm(lo, triu)


def _rope(x, cos, sin_signed):
    return x * cos + pltpu.roll(x, DK_RET // 2, 1) * sin_signed


def _group_norm(o):
    mu = jnp.mean(o, axis=-1, keepdims=True)
    d = o - mu
    var = jnp.mean(d * d, axis=-1, keepdims=True)
    return d * lax.rsqrt(var + EPS)


def _gate(h, w_ref, bm_ref, i):
    z = _mm(h, w_ref[:, OFF_GL + i * D_MODEL:OFF_GL + (i + 1) * D_MODEL])
    return jax.nn.sigmoid(z + bm_ref[:, i * D_MODEL:(i + 1) * D_MODEL])


def _memkv_kernel(mem_ref, gmn_ref, w_ref, gmk_ref, mk_ref, mv_ref, mkt_ref, mvb_ref):
    h = _rms_rows(mem_ref[0], gmn_ref[...]).astype(BF16)
    z = _mm(h, w_ref[...])
    ks = []
    for hh in range(H_MEM):
        ks.append(_rms_rows(z[:, hh * DH_MEM:(hh + 1) * DH_MEM], gmk_ref[...]))
    mk = jnp.concatenate(ks, axis=1)
    mv = z[:, W_MEM:]
    mk_ref[0] = mk
    mv_ref[0] = mv
    mkt_ref[0] = mk.T.astype(BF16)
    mvb_ref[0] = mv.astype(BF16)


def _memkv_call(mem, g_mem_norm, w_mem_kv, g_mem_k):
    B = mem.shape[0]
    const = lambda b: (0, 0)
    per_b = lambda b: (b, 0, 0)
    return pl.pallas_call(
        _memkv_kernel,
        grid=(B,),
        in_specs=[
            pl.BlockSpec((1, N_MEM, D_MODEL), per_b),
            pl.BlockSpec((1, D_MODEL), const),
            pl.BlockSpec((D_MODEL, 2 * W_MEM), const),
            pl.BlockSpec((1, DH_MEM), const),
        ],
        out_specs=[
            pl.BlockSpec((1, N_MEM, W_MEM), per_b),
            pl.BlockSpec((1, N_MEM, W_MEM), per_b),
            pl.BlockSpec((1, W_MEM, N_MEM), per_b),
            pl.BlockSpec((1, N_MEM, W_MEM), per_b),
        ],
        out_shape=[
            jax.ShapeDtypeStruct((B, N_MEM, W_MEM), F32),
            jax.ShapeDtypeStruct((B, N_MEM, W_MEM), F32),
            jax.ShapeDtypeStruct((B, W_MEM, N_MEM), BF16),
            jax.ShapeDtypeStruct((B, N_MEM, W_MEM), BF16),
        ],
        compiler_params=pltpu.CompilerParams(dimension_semantics=("arbitrary",)),
        name="memkv",
    )(mem, g_mem_norm.reshape(1, D_MODEL), w_mem_kv.astype(BF16), g_mem_k.reshape(1, DH_MEM))


def _prompt_kernel(lg_ref, x_ref, cos_ref, sin_ref, mkt_ref, mvb_ref, gn_ref, bf_ref, bm_ref, gfq_ref, gfk_ref,
                   gmq_ref, seg_ref, w_ref, wpf_ref, wpr_ref, wpm_ref, wo_ref,
                   y_ref, fk_ref, fv_ref, lf_ref, st_ref,
                   k_scr, vt_scr, qt_scr, m_scr, l_scr, acc_scr, ccar_scr, dmask_scr, decq_scr, deck_scr, ssc_scr,
                   *, tb):
    b = pl.program_id(0)
    j = pl.program_id(1)
    t0 = pl.multiple_of(j * tb, tb)

    @pl.when(jnp.logical_and(b == 0, j == 0))
    def _init_tables():
        ri = lax.broadcasted_iota(jnp.int32, (tb, tb), 0)
        ci = lax.broadcasted_iota(jnp.int32, (tb, tb), 1)
        diff = (ri - ci).astype(F32)
        rowi = lax.broadcasted_iota(jnp.int32, (tb, 1), 0).astype(F32)
        coli = lax.broadcasted_iota(jnp.int32, (1, tb), 1).astype(F32)
        for hh in range(H_RET):
            lg = lg_ref[hh]
            dmask_scr[hh] = jnp.where(diff >= 0.0, jnp.exp(lg * jnp.maximum(diff, 0.0)), 0.0)
            decq_scr[hh] = jnp.exp(lg * (rowi + 1.0))
            deck_scr[hh] = jnp.exp(lg * (tb - 1.0 - coli))
            ssc_scr[hh] = jnp.exp(jnp.full((1, DV_RET), lg * tb, F32))

    @pl.when(j == 0)
    def _init_batch():
        ccar_scr[...] = jnp.zeros_like(ccar_scr)
        st_ref[...] = jnp.zeros_like(st_ref)

    x = x_ref[0]
    h = _rms_rows(x, gn_ref[...]).astype(BF16)

    fq = _head_norm64(_mm(h, w_ref[:, OFF_FQ:OFF_FQ + W_FOX]), gfq_ref[...], seg_ref) * (DH_FOX ** -0.5 * LOG2E)
    fk = _head_norm64(_mm(h, w_ref[:, OFF_FK:OFF_FK + W_FOX]), gfk_ref[...], seg_ref)
    fv = _mm(h, w_ref[:, OFF_FV:OFF_FV + W_FOX])
    logf = _log_sigmoid(_mm(h, w_ref[:, OFF_FF:OFF_FF + LANES]) + bf_ref[...])
    fk_ref[0] = fk
    fv_ref[0] = fv
    lf_ref[0] = logf[:, :H_FOX]

    ri = lax.broadcasted_iota(jnp.int32, (tb, tb), 0)
    ci = lax.broadcasted_iota(jnp.int32, (tb, tb), 1)
    tril = jnp.where(ci <= ri, 1.0, 0.0).astype(BF16)
    cblk = _tri_cumsum(tril, logf) + ccar_scr[...]
    ccar_scr[...] = cblk[tb - 1:tb, :]
    hi, mid, lo = _split3(cblk * (-LOG2E))
    lane = lax.broadcasted_iota(jnp.int32, (tb, LANES), 1)
    bias = jnp.where(
        jnp.logical_and(lane >= DH_FOX, lane < DH_FOX + H_FOX), pltpu.roll(hi.astype(F32), DH_FOX, 1),
        jnp.where(jnp.logical_and(lane >= DH_FOX + H_FOX, lane < DH_FOX + 2 * H_FOX),
                  pltpu.roll(mid.astype(F32), DH_FOX + H_FOX, 1),
                  jnp.where(jnp.logical_and(lane >= DH_FOX + 2 * H_FOX, lane < DH_FOX + 3 * H_FOX),
                            pltpu.roll(lo.astype(F32), DH_FOX + 2 * H_FOX, 1), 0.0)))
    fq_t = fq.T
    vt_scr[:, pl.ds(t0, tb)] = fv.T.astype(BF16)
    qrow = lax.broadcasted_iota(jnp.int32, (LANES - DH_FOX, tb), 0)
    k_blk = []
    qt_blk = []
    for hh in range(H_FOX):
        pair = fk[:, (hh // 2) * LANES:(hh // 2 + 1) * LANES]
        if hh % 2:
            pair = pltpu.roll(pair, DH_FOX, 1)
        k_aug = jnp.where(lane < DH_FOX, pair, bias).astype(BF16)
        k_scr[hh, pl.ds(t0, tb), :] = k_aug
        k_blk.append(k_aug)
        ones = jnp.where(
            jnp.logical_or(qrow == hh, jnp.logical_or(qrow == H_FOX + hh, qrow == 2 * H_FOX + hh)), 1.0, 0.0)
        qt_aug = jnp.concatenate([fq_t[hh * DH_FOX:(hh + 1) * DH_FOX, :], ones], axis=0).astype(BF16)
        qt_scr[hh] = qt_aug
        qt_blk.append(qt_aug)

    key_le_query = ri <= ci
    s_all = [_mm(k_blk[hh], qt_blk[hh]) for hh in range(H_FOX)]
    p_all = []
    for hh in range(H_FOX):
        s_t = jnp.where(key_le_query, s_all[hh], -jnp.inf)
        m = jnp.max(s_t, axis=0, keepdims=True)
        p_t = jnp.exp2(s_t - m)
        m_scr[hh] = m
        l_scr[hh] = jnp.sum(p_t, axis=0, keepdims=True)
        p_all.append(p_t.astype(BF16))
    for hh in range(H_FOX):
        acc_scr[hh] = _mm(vt_scr[hh * DH_FOX:(hh + 1) * DH_FOX, pl.ds(t0, tb)], p_all[hh])

    def kv_block(kb, carry):
        off = pl.multiple_of(kb * tb, tb)
        s_all = [_mm(k_scr[hh, pl.ds(off, tb), :], qt_scr[hh]) for hh in range(H_FOX)]
        p_all = []
        alphas = []
        for hh in range(H_FOX):
            m_old = m_scr[hh]
            m_new = jnp.maximum(m_old, jnp.max(s_all[hh], axis=0, keepdims=True))
            alpha = jnp.exp2(m_old - m_new)
            p_t = jnp.exp2(s_all[hh] - m_new)
            m_scr[hh] = m_new
            l_scr[hh] = alpha * l_scr[hh] + jnp.sum(p_t, axis=0, keepdims=True)
            p_all.append(p_t.astype(BF16))
            alphas.append(alpha)
        for hh in range(H_FOX):
            acc_scr[hh] = alphas[hh] * acc_scr[hh] + _mm(
                vt_scr[hh * DH_FOX:(hh + 1) * DH_FOX, pl.ds(off, tb)], p_all[hh])
        return carry

    lax.fori_loop(0, j, kv_block, 0)
    fo = jnp.concatenate([acc_scr[hh] / l_scr[hh] for hh in range(H_FOX)], axis=0).T
    a_in = (fo * _silu(_mm(h, w_ref[:, OFF_FG:OFF_FG + W_FOX]))).astype(BF16)
    merged = _gate(h, w_ref, bm_ref, 0) * _mm(a_in, wpf_ref[...])

    zrq = _mm(h, w_ref[:, OFF_RQ:OFF_RQ + W_RET_QK])
    zrk = _mm(h, w_ref[:, OFF_RK:OFF_RK + W_RET_QK])
    zrv = _mm(h, w_ref[:, OFF_RV:OFF_RV + W_RET_V])
    zrg = _mm(h, w_ref[:, OFF_RG:OFF_RG + W_RET_V])
    cos = cos_ref[...]
    sin = sin_ref[...]
    r_parts = []
    for hh in range(H_RET):
        q = _rope(zrq[:, hh * DK_RET:(hh + 1) * DK_RET], cos, sin).astype(BF16)
        k = _rope(zrk[:, hh * DK_RET:(hh + 1) * DK_RET], cos, sin) * (DK_RET ** -0.5)
        kt = k.T
        v = zrv[:, hh * DV_RET:(hh + 1) * DV_RET].astype(BF16)
        state = st_ref[0, hh]
        inter = _mm(q, state.astype(BF16)) * decq_scr[hh]
        scores = _mm(q, kt.astype(BF16)) * dmask_scr[hh]
        o = inter + _mm(scores.astype(BF16), v)
        st_ref[0, hh] = ssc_scr[hh] * state + _mm((kt * deck_scr[hh]).astype(BF16), v)
        r_parts.append(_group_norm(o) * _silu(zrg[:, hh * DV_RET:(hh + 1) * DV_RET]))
    r_in = jnp.concatenate(r_parts, axis=1).astype(BF16)
    merged = merged + _gate(h, w_ref, bm_ref, 1) * _mm(r_in, wpr_ref[...])

    zmq = _mm(h, w_ref[:, OFF_MQ:OFF_MQ + W_MEM])
    m_parts = []
    for hh in range(H_MEM):
        cols = slice(hh * DH_MEM, (hh + 1) * DH_MEM)
        q = _rms_rows(zmq[:, cols], gmq_ref[...]).astype(BF16)
        s = _mm(q, mkt_ref[0, cols, :]) * (DH_MEM ** -0.5)
        p = jnp.exp(s - jnp.max(s, axis=-1, keepdims=True))
        l = jnp.sum(p, axis=-1, keepdims=True)
        m_parts.append(_mm(p.astype(BF16), mvb_ref[0, :, cols]) / l)
    mo = jnp.concatenate(m_parts, axis=1)
    m_in = (mo * _silu(_mm(h, w_ref[:, OFF_MG:OFF_MG + W_MEM]))).astype(BF16)
    merged = merged + _gate(h, w_ref, bm_ref, 2) * _mm(m_in, wpm_ref[...])

    y_ref[0] = x + _mm(merged.astype(BF16), wo_ref[...])


def _resident(shape):
    nd = len(shape)
    return pl.BlockSpec(shape, lambda b, j: (0,) * nd, pipeline_mode=pl.Buffered(1))


def _prompt_call(x, cos, sin, mkt, mvb, log_g, g_norm, bf_pad, b_merge, gfq, gfk, gmq, seg, w_main, wpf, wpr, wpm, wo,
                 tb=PROMPT_BLOCK):
    B, T, _ = x.shape
    nt = T // tb
    kernel = functools.partial(_prompt_kernel, tb=tb)
    in_specs = [
        pl.BlockSpec(memory_space=pltpu.SMEM),
        pl.BlockSpec((1, tb, D_MODEL), lambda b, j: (b, j, 0)),
        pl.BlockSpec((tb, DK_RET), lambda b, j: (j, 0)),
        pl.BlockSpec((tb, DK_RET), lambda b, j: (j, 0)),
        pl.BlockSpec((1, W_MEM, N_MEM), lambda b, j: (b, 0, 0)),
        pl.BlockSpec((1, N_MEM, W_MEM), lambda b, j: (b, 0, 0)),
        _resident((1, D_MODEL)),
        _resident((1, LANES)),
        _resident((1, N_BRANCH * D_MODEL)),
        _resident((1, W_FOX)),
        _resident((1, W_FOX)),
        _resident((1, DH_MEM)),
        _resident((W_FOX, W_FOX)),
        _resident((D_MODEL, W_MAIN)),
        _resident((W_FOX, D_MODEL)),
        _resident((W_RET_V, D_MODEL)),
        _resident((W_MEM, D_MODEL)),
        _resident((D_MODEL, D_MODEL)),
    ]
    out_specs = [
        pl.BlockSpec((1, tb, D_MODEL), lambda b, j: (b, j, 0)),
        pl.BlockSpec((1, tb, W_FOX), lambda b, j: (b, j, 0)),
        pl.BlockSpec((1, tb, W_FOX), lambda b, j: (b, j, 0)),
        pl.BlockSpec((1, tb, H_FOX), lambda b, j: (b, j, 0)),
        pl.BlockSpec((1, H_RET, DK_RET, DV_RET), lambda b, j: (b, 0, 0, 0)),
    ]
    out_shape = [
        jax.ShapeDtypeStruct((B, T, D_MODEL), F32),
        jax.ShapeDtypeStruct((B, T, W_FOX), F32),
        jax.ShapeDtypeStruct((B, T, W_FOX), F32),
        jax.ShapeDtypeStruct((B, T, H_FOX), F32),
        jax.ShapeDtypeStruct((B, H_RET, DK_RET, DV_RET), F32),
    ]
    scratch = [
        pltpu.VMEM((H_FOX, T, LANES), BF16),
        pltpu.VMEM((W_FOX, T), BF16),
        pltpu.VMEM((H_FOX, LANES, tb), BF16),
        pltpu.VMEM((H_FOX, 1, tb), F32),
        pltpu.VMEM((H_FOX, 1, tb), F32),
        pltpu.VMEM((H_FOX, DH_FOX, tb), F32),
        pltpu.VMEM((1, LANES), F32),
        pltpu.VMEM((H_RET, tb, tb), F32),
        pltpu.VMEM((H_RET, tb, 1), F32),
        pltpu.VMEM((H_RET, 1, tb), F32),
        pltpu.VMEM((H_RET, 1, DV_RET), F32),
    ]
    return pl.pallas_call(
        kernel,
        grid=(B, nt),
        in_specs=in_specs,
        out_specs=out_specs,
        out_shape=out_shape,
        scratch_shapes=scratch,
        compiler_params=pltpu.CompilerParams(
            dimension_semantics=("arbitrary", "arbitrary"), vmem_limit_bytes=VMEM_LIMIT_BYTES),
        name="prompt",
    )(log_g, x, cos, sin, mkt, mvb, g_norm, bf_pad, b_merge, gfq, gfk, gmq, seg, w_main, wpf, wpr, wpm, wo)


def _sample_kernel(lg_ref, x_ref, cos_ref, sin_ref, ck_ref, cv_ref, clf_ref, st0_ref, cmk_ref, cmv_ref,
                   gn_ref, bf_ref, bm_ref, gfq_ref, gfk_ref, gmq_ref, seg_ref, w_ref, wpf_ref, wpr_ref, wpm_ref,
                   wo_ref,
                   y_ref, fk_ref, fv_ref, lf_ref, st_ref,
                   h_scr, fq_scr, knt_scr, vn_scr, ncn_scr, rq_scr, rkt_scr, rv_scr, mq_scr, fo_scr, ro_scr, mo_scr,
                   *, nb, ts, past):
    b = pl.program_id(0)
    n = nb * ts
    r0 = pl.multiple_of(b * ts, ts)

    @pl.when(b == 0)
    def _project_all():
        h = _rms_rows(x_ref[...], gn_ref[...]).astype(BF16)
        h_scr[...] = h
        fq_scr[...] = _head_norm64(_mm(h, w_ref[:, OFF_FQ:OFF_FQ + W_FOX]), gfq_ref[...], seg_ref) * (DH_FOX ** -0.5)
        fk = _head_norm64(_mm(h, w_ref[:, OFF_FK:OFF_FK + W_FOX]), gfk_ref[...], seg_ref)
        fv = _mm(h, w_ref[:, OFF_FV:OFF_FV + W_FOX])
        logf = _log_sigmoid(_mm(h, w_ref[:, OFF_FF:OFF_FF + LANES]) + bf_ref[...])
        fk_ref[...] = fk
        fv_ref[...] = fv
        lf_ref[...] = logf[:, :H_FOX]
        knt_scr[...] = fk.T.astype(BF16)
        vn_scr[...] = fv.astype(BF16)
        ri = lax.broadcasted_iota(jnp.int32, (n, n), 0)
        ci = lax.broadcasted_iota(jnp.int32, (n, n), 1)
        same = _idiv(ri, ts) == _idiv(ci, ts)
        tril = jnp.where(jnp.logical_and(same, ci <= ri), 1.0, 0.0).astype(BF16)
        ncn_scr[...] = -(_tri_cumsum(tril, logf).T)
        cos = cos_ref[...]
        sin = sin_ref[...]
        zrq = _mm(h, w_ref[:, OFF_RQ:OFF_RQ + W_RET_QK])
        zrk = _mm(h, w_ref[:, OFF_RK:OFF_RK + W_RET_QK])
        for hh in range(H_RET):
            cols = slice(hh * DK_RET, (hh + 1) * DK_RET)
            rq_scr[:, cols] = _rope(zrq[:, cols], cos, sin)
            rkt_scr[cols, :] = (_rope(zrk[:, cols], cos, sin) * (DK_RET ** -0.5)).T
        rv_scr[...] = _mm(h, w_ref[:, OFF_RV:OFF_RV + W_RET_V]).astype(BF16)
        zmq = _mm(h, w_ref[:, OFF_MQ:OFF_MQ + W_MEM])
        for hh in range(H_MEM):
            cols = slice(hh * DH_MEM, (hh + 1) * DH_MEM)
            mq_scr[:, cols] = _rms_rows(zmq[:, cols], gmq_ref[...])

    nr = H_FOX * ts
    fq_b = fq_scr[pl.ds(r0, ts), :]
    row_head = _idiv(lax.broadcasted_iota(jnp.int32, (nr, W_FOX), 0), ts)
    col_head = _idiv(lax.broadcasted_iota(jnp.int32, (nr, W_FOX), 1), DH_FOX)
    head_sel = row_head == col_head
    q_bd = jnp.where(head_sel, jnp.concatenate([fq_b] * H_FOX, axis=0), 0.0).astype(BF16)

    cc = 256
    ri = lax.broadcasted_iota(jnp.int32, (cc, cc), 0)
    ci = lax.broadcasted_iota(jnp.int32, (cc, cc), 1)
    triu = jnp.where(ri <= ci, 1.0, 0.0).astype(BF16)
    carry = jnp.zeros((H_FOX, 1), F32)
    c_chunks = []
    for c in range(past // cc):
        cch = _cumsum_lanes(triu, clf_ref[0, :, c * cc:(c + 1) * cc]) + carry
        carry = cch[:, cc - 1:cc]
        c_chunks.append(cch)
    c_past = jnp.concatenate(c_chunks, axis=1)
    ncp = jnp.concatenate([jnp.broadcast_to(-c_past[hh:hh + 1, :], (ts, past)) for hh in range(H_FOX)], axis=0)
    ncn = ncn_scr[:H_FOX, :] - carry
    ncn = jnp.concatenate([jnp.broadcast_to(ncn[hh:hh + 1, :], (ts, n)) for hh in range(H_FOX)], axis=0)

    s_p = _mm_nt(q_bd, ck_ref[0].astype(BF16)) + ncp
    s_n = _mm(q_bd, knt_scr[...]) + ncn
    tok = lax.broadcasted_iota(jnp.int32, (nr, n), 1)
    qt = _imod(lax.broadcasted_iota(jnp.int32, (nr, n), 0), ts)
    valid = jnp.logical_and(_idiv(tok, ts) == b, _imod(tok, ts) <= qt)
    s_n = jnp.where(valid, s_n, -jnp.inf)
    m = jnp.maximum(jnp.max(s_p, axis=-1, keepdims=True), jnp.max(s_n, axis=-1, keepdims=True))
    p_p = jnp.exp(s_p - m)
    p_n = jnp.exp(s_n - m)
    l = jnp.sum(p_p, axis=-1, keepdims=True) + jnp.sum(p_n, axis=-1, keepdims=True)
    o_bd = (_mm(p_p.astype(BF16), cv_ref[0].astype(BF16)) + _mm(p_n.astype(BF16), vn_scr[...])) / l
    o_bd = jnp.where(head_sel, o_bd, 0.0)
    fo = o_bd[0:ts, :]
    for hh in range(1, H_FOX):
        fo = fo + o_bd[hh * ts:(hh + 1) * ts, :]
    fo_scr[pl.ds(r0, ts), :] = fo

    tokr = lax.broadcasted_iota(jnp.int32, (ts, n), 1)
    qtr = lax.broadcasted_iota(jnp.int32, (ts, n), 0)
    in_b = _idiv(tokr, ts) == b
    dt = (qtr - _imod(tokr, ts)).astype(F32)
    tokc = lax.broadcasted_iota(jnp.int32, (1, n), 1)
    kpos = _imod(tokc, ts).astype(F32)
    rowi = lax.broadcasted_iota(jnp.int32, (ts, 1), 0).astype(F32)
    r_parts = []
    for hh in range(H_RET):
        lg = lg_ref[hh]
        q = rq_scr[pl.ds(r0, ts), hh * DK_RET:(hh + 1) * DK_RET].astype(BF16)
        kt = rkt_scr[hh * DK_RET:(hh + 1) * DK_RET, :]
        v = rv_scr[:, hh * DV_RET:(hh + 1) * DV_RET]
        state = st0_ref[0, hh]
        dmask = jnp.where(jnp.logical_and(in_b, dt >= 0.0), jnp.exp(lg * jnp.maximum(dt, 0.0)), 0.0)
        inter = _mm(q, state.astype(BF16)) * jnp.exp(lg * (rowi + 1.0))
        scores = _mm(q, kt.astype(BF16)) * dmask
        r_parts.append(inter + _mm(scores.astype(BF16), v))
        deck = jnp.where(_idiv(tokc, ts) == b,jnp.exp(lg * (ts - 1.0 - kpos)), 0.0)
        st_ref[0, hh] = jnp.exp(jnp.full((1, DV_RET), lg * ts, F32)) * state + _mm((kt * deck).astype(BF16), v)
    ro_scr[pl.ds(r0, ts), :] = jnp.concatenate(r_parts, axis=1)

    m_parts = []
    for hh in range(H_MEM):
        cols = slice(hh * DH_MEM, (hh + 1) * DH_MEM)
        q = mq_scr[pl.ds(r0, ts), cols].astype(BF16)
        s = _mm_nt(q, cmk_ref[0, :, cols].astype(BF16)) * (DH_MEM ** -0.5)
        p = jnp.exp(s - jnp.max(s, axis=-1, keepdims=True))
        l = jnp.sum(p, axis=-1, keepdims=True)
        m_parts.append(_mm(p.astype(BF16), cmv_ref[0, :, cols].astype(BF16)) / l)
    mo_scr[pl.ds(r0, ts), :] = jnp.concatenate(m_parts, axis=1)

    @pl.when(b == nb - 1)
    def _output_all():
        h = h_scr[...]
        a_in = (fo_scr[...] * _silu(_mm(h, w_ref[:, OFF_FG:OFF_FG + W_FOX]))).astype(BF16)
        merged = _gate(h, w_ref, bm_ref, 0) * _mm(a_in, wpf_ref[...])
        zrg = _mm(h, w_ref[:, OFF_RG:OFF_RG + W_RET_V])
        ro = ro_scr[...]
        r_parts = []
        for hh in range(H_RET):
            cols = slice(hh * DV_RET, (hh + 1) * DV_RET)
            r_parts.append(_group_norm(ro[:, cols]) * _silu(zrg[:, cols]))
        r_in = jnp.concatenate(r_parts, axis=1).astype(BF16)
        merged = merged + _gate(h, w_ref, bm_ref, 1) * _mm(r_in, wpr_ref[...])
        m_in = (mo_scr[...] * _silu(_mm(h, w_ref[:, OFF_MG:OFF_MG + W_MEM]))).astype(BF16)
        merged = merged + _gate(h, w_ref, bm_ref, 2) * _mm(m_in, wpm_ref[...])
        y_ref[...] = x_ref[...] + _mm(merged.astype(BF16), wo_ref[...])


def _sample_call(x2d, cos, sin, ck, cv, clf_t, st0, cmk, cmv, log_g, g_norm, bf_pad, b_merge, gfq, gfk, gmq, seg,
                 w_main, wpf, wpr, wpm, wo, nb, ts):
    n = nb * ts
    past = ck.shape[1]
    kernel = functools.partial(_sample_kernel, nb=nb, ts=ts, past=past)

    def res(shape):
        nd = len(shape)
        return pl.BlockSpec(shape, lambda b: (0,) * nd, pipeline_mode=pl.Buffered(1))

    in_specs = [
        pl.BlockSpec(memory_space=pltpu.SMEM),
        res((n, D_MODEL)),
        res((n, DK_RET)),
        res((n, DK_RET)),
        pl.BlockSpec((1, past, W_FOX), lambda b: (b, 0, 0)),
        pl.BlockSpec((1, past, W_FOX), lambda b: (b, 0, 0)),
        pl.BlockSpec((1, H_FOX, past), lambda b: (b, 0, 0)),
        pl.BlockSpec((1, H_RET, DK_RET, DV_RET), lambda b: (b, 0, 0, 0)),
        pl.BlockSpec((1, N_MEM, W_MEM), lambda b: (b, 0, 0)),
        pl.BlockSpec((1, N_MEM, W_MEM), lambda b: (b, 0, 0)),
        res((1, D_MODEL)),
        res((1, LANES)),
        res((1, N_BRANCH * D_MODEL)),
        res((1, W_FOX)),
        res((1, W_FOX)),
        res((1, DH_MEM)),
        res((W_FOX, W_FOX)),
        res((D_MODEL, W_MAIN)),
        res((W_FOX, D_MODEL)),
        res((W_RET_V, D_MODEL)),
        res((W_MEM, D_MODEL)),
        res((D_MODEL, D_MODEL)),
    ]
    full = lambda shape: pl.BlockSpec(shape, lambda b: (0,) * len(shape))
    out_specs = [
        full((n, D_MODEL)),
        full((n, W_FOX)),
        full((n, W_FOX)),
        full((n, H_FOX)),
        pl.BlockSpec((1, H_RET, DK_RET, DV_RET), lambda b: (b, 0, 0, 0)),
    ]
    out_shape = [
        jax.ShapeDtypeStruct((n, D_MODEL), F32),
        jax.ShapeDtypeStruct((n, W_FOX), F32),
        jax.ShapeDtypeStruct((n, W_FOX), F32),
        jax.ShapeDtypeStruct((n, H_FOX), F32),
        jax.ShapeDtypeStruct((nb, H_RET, DK_RET, DV_RET), F32),
    ]
    scratch = [
        pltpu.VMEM((n, D_MODEL), BF16),
        pltpu.VMEM((n, W_FOX), F32),
        pltpu.VMEM((W_FOX, n), BF16),
        pltpu.VMEM((n, W_FOX), BF16),
        pltpu.VMEM((LANES, n), F32),
        pltpu.VMEM((n, W_RET_QK), F32),
        pltpu.VMEM((W_RET_QK, n), F32),
        pltpu.VMEM((n, W_RET_V), BF16),
        pltpu.VMEM((n, W_MEM), F32),
        pltpu.VMEM((n, W_FOX), F32),
        pltpu.VMEM((n, W_RET_V), F32),
        pltpu.VMEM((n, W_MEM), F32),
    ]
    return pl.pallas_call(
        kernel,
        grid=(nb,),
        in_specs=in_specs,
        out_specs=out_specs,
        out_shape=out_shape,
        scratch_shapes=scratch,
        compiler_params=pltpu.CompilerParams(
            dimension_semantics=("arbitrary",), vmem_limit_bytes=VMEM_LIMIT_BYTES),
        name="sample",
    )(log_g, x2d, cos, sin, ck, cv, clf_t, st0, cmk, cmv, g_norm, bf_pad, b_merge, gfq, gfk, gmq, seg,
      w_main, wpf, wpr, wpm, wo)


def _rope_tables(pos):
    half = DK_RET // 2
    inv = ROPE_BASE ** (-jnp.arange(half, dtype=F32) / half)
    ang = pos.astype(F32)[:, None] * inv[None, :]
    cos = jnp.cos(ang)
    sin = jnp.sin(ang)
    return jnp.concatenate([cos, cos], axis=1), jnp.concatenate([-sin, sin], axis=1)


def _reorder_w_in(w):
    f0 = 3 * W_FOX
    pad = jnp.zeros((D_MODEL, LANES - H_FOX), w.dtype)
    return jnp.concatenate([w[:, :f0], w[:, f0 + H_FOX:], w[:, f0:f0 + H_FOX], pad], axis=1).astype(BF16)


def kernel(x_prompt, x_sample, mem_prompt, cache_fox_k, cache_fox_v, cache_fox_logf, state_ret, cache_mem_k, cache_mem_v,
           g_norm, g_mem_norm, w_in, b_f, b_merge, g_fox_q, g_fox_k, g_mem_q, g_mem_k, w_mem_kv,
           w_p_fox, w_p_ret, w_p_mem, w_out):
    depth = w_in.shape[0]
    assert depth == 1, "single-layer kernel"
    B, T, _ = x_prompt.shape
    nb, ts, _ = x_sample.shape
    past = cache_fox_k.shape[2]

    log_g = jnp.log1p(-jnp.exp2(-5.0 - jnp.arange(H_RET, dtype=F32)))
    head_of_lane = jnp.arange(W_FOX) // DH_FOX
    seg = (head_of_lane[:, None] == head_of_lane[None, :]).astype(BF16)
    w_main = _reorder_w_in(w_in[0])
    gn = g_norm[0].reshape(1, D_MODEL)
    bf_pad = jnp.concatenate([b_f[0], jnp.zeros((LANES - H_FOX,), F32)]).reshape(1, LANES)
    bm = b_merge[0].reshape(1, N_BRANCH * D_MODEL)
    gfq = jnp.tile(g_fox_q[0], H_FOX).reshape(1, W_FOX)
    gfk = jnp.tile(g_fox_k[0], H_FOX).reshape(1, W_FOX)
    gmq = g_mem_q[0].reshape(1, DH_MEM)
    wpf = w_p_fox[0].astype(BF16)
    wpr = w_p_ret[0].astype(BF16)
    wpm = w_p_mem[0].astype(BF16)
    wo = w_out[0].astype(BF16)
    shared = (gn, bf_pad, bm, gfq, gfk, gmq, seg, w_main, wpf, wpr, wpm, wo)

    mk, mv, mkt, mvb = _memkv_call(mem_prompt, g_mem_norm[0], w_mem_kv[0], g_mem_k[0])

    cos_p, sin_p = _rope_tables(jnp.arange(T, dtype=jnp.int32))
    y_p, fk_p, fv_p, lf_p, st_p = _prompt_call(x_prompt, cos_p, sin_p, mkt, mvb, log_g, *shared)

    pos_s = past + jnp.arange(ts, dtype=jnp.int32)
    cos_s, sin_s = _rope_tables(jnp.tile(pos_s, nb))
    y_s, fk_s, fv_s, lf_s, st_s = _sample_call(
        x_sample.reshape(nb * ts, D_MODEL), cos_s, sin_s,
        cache_fox_k[0].reshape(nb, past, W_FOX), cache_fox_v[0].reshape(nb, past, W_FOX),
        jnp.swapaxes(cache_fox_logf[0], 1, 2), state_ret[0],
        cache_mem_k[0].reshape(nb, N_MEM, W_MEM), cache_mem_v[0].reshape(nb, N_MEM, W_MEM),
        log_g, *shared, nb=nb, ts=ts)

    return (y_p, y_s.reshape(nb, ts, D_MODEL),
            fk_p.reshape(1, B, T, H_FOX, DH_FOX), fv_p.reshape(1, B, T, H_FOX, DH_FOX), lf_p[None],
            st_p[None], mk.reshape(1, B, N_MEM, H_MEM, DH_MEM), mv.reshape(1, B, N_MEM, H_MEM, DH_MEM),
            fk_s.reshape(1, nb, ts, H_FOX, DH_FOX), fv_s.reshape(1, nb, ts, H_FOX, DH_FOX),
            lf_s.reshape(1, nb, ts, H_FOX), st_s[None])
```

```python
import functools

import jax
import jax.numpy as jnp
from jax import lax
from jax.experimental import pallas as pl
from jax.experimental.pallas import tpu as pltpu

D_MODEL = 1024
N_MEM = 256
H_FOX = 8
DH_FOX = 64
H_RET = 4
DK_RET = 128
DV_RET = 256
H_MEM = 4
DH_MEM = 128
W_FOX = H_FOX * DH_FOX
W_RET_QK = H_RET * DK_RET
W_RET_V = H_RET * DV_RET
W_MEM = H_MEM * DH_MEM
N_BRANCH = 3
ROPE_BASE = 10000.0
EPS = 1e-6
LOG2E = 1.4426950408889634

LANES = 128

OFF_FQ = 0
OFF_FK = OFF_FQ + W_FOX
OFF_FV = OFF_FK + W_FOX
OFF_FG = OFF_FV + W_FOX
OFF_RQ = OFF_FG + W_FOX
OFF_RK = OFF_RQ + W_RET_QK
OFF_RV = OFF_RK + W_RET_QK
OFF_RG = OFF_RV + W_RET_V
OFF_MQ = OFF_RG + W_RET_V
OFF_MG = OFF_MQ + W_MEM
OFF_GL = OFF_MG + W_MEM
OFF_FF = OFF_GL + N_BRANCH * D_MODEL
W_MAIN = OFF_FF + LANES

PROMPT_BLOCK = 256
VMEM_LIMIT_BYTES = 60 * 1024 * 1024

F32 = jnp.float32
BF16 = jnp.bfloat16


def _mm(a, b):
    return jnp.dot(a, b, preferred_element_type=F32)


def _mm_nt(a, b):
    return lax.dot_general(a, b, (((1,), (1,)), ((), ())), preferred_element_type=F32)


def _idiv(x, d):
    assert d & (d - 1) == 0
    return lax.shift_right_logical(x, d.bit_length() - 1)


def _imod(x, d):
    assert d & (d - 1) == 0
    return jnp.bitwise_and(x, d - 1)


def _rms_rows(x, g):
    ms = jnp.mean(x * x, axis=-1, keepdims=True)
    return x * lax.rsqrt(ms + EPS) * g


def _head_norm64(z, g, seg_ref):
    ss = _mm((z * z).astype(BF16), seg_ref[...])
    return z * lax.rsqrt(ss * (1.0 / DH_FOX) + EPS) * g


def _log_sigmoid(u):
    return jnp.minimum(u, 0.0) - jnp.log1p(jnp.exp(-jnp.abs(u)))


def _silu(u):
    return u * jax.nn.sigmoid(u)


def _split3(a):
    hi = a.astype(BF16)
    r1 = a - hi.astype(F32)
    mid = r1.astype(BF16)
    lo = (r1 - mid.astype(F32)).astype(BF16)
    return hi, mid, lo


def _tri_cumsum(tri, a):
    hi, mid, lo = _split3(a)
    return _mm(tri, hi) + _mm(tri, mid) + _mm(tri, lo)


def _cumsum_lanes(triu, a):
    hi, mid, lo = _split3(a)
    return _mm(hi, triu) + _mm(mid, triu) + _mm(lo, triu)


def _rope(x, cos, sin_signed):
    return x * cos + pltpu.roll(x, DK_RET // 2, 1) * sin_signed


def _group_norm(o):
    mu = jnp.mean(o, axis=-1, keepdims=True)
    d = o - mu
    var = jnp.mean(d * d, axis=-1, keepdims=True)
    return d * lax.rsqrt(var + EPS)


def _gate(h, w_ref, bm_ref, i):
    z = _mm(h, w_ref[:, OFF_GL + i * D_MODEL:OFF_GL + (i + 1) * D_MODEL])
    return jax.nn.sigmoid(z + bm_ref[:, i * D_MODEL:(i + 1) * D_MODEL])


def _memkv_kernel(mem_ref, gmn_ref, w_ref, gmk_ref, mk_ref, mv_ref, mkt_ref, mvb_ref):
    h = _rms_rows(mem_ref[0], gmn_ref[...]).astype(BF16)
    z = _mm(h, w_ref[...])
    ks = []
    for hh in range(H_MEM):
        ks.append(_rms_rows(z[:, hh * DH_MEM:(hh + 1) * DH_MEM], gmk_ref[...]))
    mk = jnp.concatenate(ks, axis=1)
    mv = z[:, W_MEM:]
    mk_ref[0] = mk
    mv_ref[0] = mv
    mkt_ref[0] = mk.T.astype(BF16)
    mvb_ref[0] = mv.astype(BF16)


def _memkv_call(mem, g_mem_norm, w_mem_kv, g_mem_k):
    B = mem.shape[0]
    const = lambda b: (0, 0)
    per_b = lambda b: (b, 0, 0)
    return pl.pallas_call(
        _memkv_kernel,
        grid=(B,),
        in_specs=[
            pl.BlockSpec((1, N_MEM, D_MODEL), per_b),
            pl.BlockSpec((1, D_MODEL), const),
            pl.BlockSpec((D_MODEL, 2 * W_MEM), const),
            pl.BlockSpec((1, DH_MEM), const),
        ],
        out_specs=[
            pl.BlockSpec((1, N_MEM, W_MEM), per_b),
            pl.BlockSpec((1, N_MEM, W_MEM), per_b),
            pl.BlockSpec((1, W_MEM, N_MEM), per_b),
            pl.BlockSpec((1, N_MEM, W_MEM), per_b),
        ],
        out_shape=[
            jax.ShapeDtypeStruct((B, N_MEM, W_MEM), F32),
            jax.ShapeDtypeStruct((B, N_MEM, W_MEM), F32),
            jax.ShapeDtypeStruct((B, W_MEM, N_MEM), BF16),
            jax.ShapeDtypeStruct((B, N_MEM, W_MEM), BF16),
        ],
        compiler_params=pltpu.CompilerParams(dimension_semantics=("arbitrary",)),
        name="memkv",
    )(mem, g_mem_norm.reshape(1, D_MODEL), w_mem_kv.astype(BF16), g_mem_k.reshape(1, DH_MEM))


def _prompt_kernel(lg_ref, x_ref, cos_ref, sin_ref, mkt_ref, mvb_ref, gn_ref, bf_ref, bm_ref, gfq_ref, gfk_ref,
                   gmq_ref, seg_ref, w_ref, wpf_ref, wpr_ref, wpm_ref, wo_ref, wvt_ref,
                   y_ref, fkt_ref, fvt_ref, lft_ref, st_ref,
                   k_scr, vt_scr, qt_scr, m_scr, l_scr, acc_scr, ccar_scr, dmask_scr, decq_scr, deck_scr, ssc_scr,
                   *, tb):
    b = pl.program_id(0)
    j = pl.program_id(1)
    t0 = pl.multiple_of(j * tb, tb)

    @pl.when(jnp.logical_and(b == 0, j == 0))
    def _init_tables():
        ri = lax.broadcasted_iota(jnp.int32, (tb, tb), 0)
        ci = lax.broadcasted_iota(jnp.int32, (tb, tb), 1)
        diff = (ri - ci).astype(F32)
        rowi = lax.broadcasted_iota(jnp.int32, (tb, 1), 0).astype(F32)
        coli = lax.broadcasted_iota(jnp.int32, (1, tb), 1).astype(F32)
        for hh in range(H_RET):
            lg = lg_ref[hh]
            dmask_scr[hh] = jnp.where(diff >= 0.0, jnp.exp(lg * jnp.maximum(diff, 0.0)), 0.0)
            decq_scr[hh] = jnp.exp(lg * (rowi + 1.0))
            deck_scr[hh] = jnp.exp(lg * (tb - 1.0 - coli))
            ssc_scr[hh] = jnp.exp(jnp.full((1, DV_RET), lg * tb, F32))

    @pl.when(j == 0)
    def _init_batch():
        ccar_scr[...] = jnp.zeros_like(ccar_scr)
        st_ref[...] = jnp.zeros_like(st_ref)

    def proj(off, width):
        return _mm(h, w_ref[:, off:off + width])

    x = x_ref[0]
    h = _rms_rows(x, gn_ref[...]).astype(BF16)

    zfq = proj(OFF_FQ, W_FOX)
    zfk = proj(OFF_FK, W_FOX)
    fv_t = _mm_nt(wvt_ref[...], h)
    zff = proj(OFF_FF, LANES)
    zrq = proj(OFF_RQ, W_RET_QK)
    ssq = _mm((zfq * zfq).astype(BF16), seg_ref[...])
    ssk = _mm((zfk * zfk).astype(BF16), seg_ref[...])
    zrk = proj(OFF_RK, W_RET_QK)
    logf = _log_sigmoid(zff + bf_ref[...])
    ri = lax.broadcasted_iota(jnp.int32, (tb, tb), 0)
    ci = lax.broadcasted_iota(jnp.int32, (tb, tb), 1)
    tril = jnp.where(ci <= ri, 1.0, 0.0).astype(BF16)
    cblk = _tri_cumsum(tril, logf) + ccar_scr[...]
    ccar_scr[...] = cblk[tb - 1:tb, :]
    rv = proj(OFF_RV, W_RET_V).astype(BF16)
    fq = zfq * lax.rsqrt(ssq * (1.0 / DH_FOX) + EPS) * gfq_ref[...] * (DH_FOX ** -0.5 * LOG2E)
    fk = zfk * lax.rsqrt(ssk * (1.0 / DH_FOX) + EPS) * gfk_ref[...]
    fkt_ref[0] = fk.T
    fvt_ref[0] = fv_t
    lft_ref[0] = logf.T[:H_FOX, :]
    hi, mid, lo = _split3(cblk * (-LOG2E))
    lane = lax.broadcasted_iota(jnp.int32, (tb, LANES), 1)
    bias = jnp.where(
        jnp.logical_and(lane >= DH_FOX, lane < DH_FOX + H_FOX), pltpu.roll(hi.astype(F32), DH_FOX, 1),
        jnp.where(jnp.logical_and(lane >= DH_FOX + H_FOX, lane < DH_FOX + 2 * H_FOX),
                  pltpu.roll(mid.astype(F32), DH_FOX + H_FOX, 1),
                  jnp.where(jnp.logical_and(lane >= DH_FOX + 2 * H_FOX, lane < DH_FOX + 3 * H_FOX),
                            pltpu.roll(lo.astype(F32), DH_FOX + 2 * H_FOX, 1), 0.0)))
    fq_t = fq.T
    vt_scr[:, pl.ds(t0, tb)] = fv_t.astype(BF16)
    qrow = lax.broadcasted_iota(jnp.int32, (LANES - DH_FOX, tb), 0)
    k_blk = []
    qt_blk = []
    for hh in range(H_FOX):
        pair = fk[:, (hh // 2) * LANES:(hh // 2 + 1) * LANES]
        if hh % 2:
            pair = pltpu.roll(pair, DH_FOX, 1)
        k_aug = jnp.where(lane < DH_FOX, pair, bias).astype(BF16)
        k_scr[hh, pl.ds(t0, tb), :] = k_aug
        k_blk.append(k_aug)
        ones = jnp.where(
            jnp.logical_or(qrow == hh, jnp.logical_or(qrow == H_FOX + hh, qrow == 2 * H_FOX + hh)), 1.0, 0.0)
        qt_aug = jnp.concatenate([fq_t[hh * DH_FOX:(hh + 1) * DH_FOX, :], ones], axis=0).astype(BF16)
        qt_scr[hh] = qt_aug
        qt_blk.append(qt_aug)

    key_le_query = ri <= ci
    s_all = [_mm(k_blk[hh], qt_blk[hh]) for hh in range(H_FOX)]
    zrg = proj(OFF_RG, W_RET_V)
    p_all = []
    for hh in range(H_FOX):
        s_t = jnp.where(key_le_query, s_all[hh], -jnp.inf)
        m = jnp.max(s_t, axis=0, keepdims=True)
        p_t = jnp.exp2(s_t - m)
        m_scr[hh] = m
        l_scr[hh] = jnp.sum(p_t, axis=0, keepdims=True)
        p_all.append(p_t.astype(BF16))
    for hh in range(H_FOX):
        acc_scr[hh] = _mm(vt_scr[hh * DH_FOX:(hh + 1) * DH_FOX, pl.ds(t0, tb)], p_all[hh])

    def kv_block(kb, carry):
        off = pl.multiple_of(kb * tb, tb)
        s_all = [_mm(k_scr[hh, pl.ds(off, tb), :], qt_scr[hh]) for hh in range(H_FOX)]
        p_all = []
        alphas = []
        for hh in range(H_FOX):
            m_old = m_scr[hh]
            m_new = jnp.maximum(m_old, jnp.max(s_all[hh], axis=0, keepdims=True))
            alpha = jnp.exp2(m_old - m_new)
            p_t = jnp.exp2(s_all[hh] - m_new)
            m_scr[hh] = m_new
            l_scr[hh] = alpha * l_scr[hh] + jnp.sum(p_t, axis=0, keepdims=True)
            p_all.append(p_t.astype(BF16))
            alphas.append(alpha)
        for hh in range(H_FOX):
            acc_scr[hh] = alphas[hh] * acc_scr[hh] + _mm(
                vt_scr[hh * DH_FOX:(hh + 1) * DH_FOX, pl.ds(off, tb)], p_all[hh])
        return carry

    lax.fori_loop(0, j, kv_block, 0)
    zfg = proj(OFF_FG, W_FOX)
    gate0 = _gate(h, w_ref, bm_ref, 0)
    fo = jnp.concatenate([acc_scr[hh] / l_scr[hh] for hh in range(H_FOX)], axis=0).T
    a_in = (fo * _silu(zfg)).astype(BF16)

    cos = cos_ref[...]
    sin = sin_ref[...]
    rq, rkt, rvs, states = [], [], [], []
    for hh in range(H_RET):
        cols = slice(hh * DK_RET, (hh + 1) * DK_RET)
        rq.append(_rope(zrq[:, cols], cos, sin).astype(BF16))
        rkt.append((_rope(zrk[:, cols], cos, sin) * (DK_RET ** -0.5)).T)
        rvs.append(rv[:, hh * DV_RET:(hh + 1) * DV_RET])
        states.append(st_ref[0, hh])
    inter = [_mm(rq[hh], states[hh].astype(BF16)) for hh in range(H_RET)]
    scores = [_mm(rq[hh], rkt[hh].astype(BF16)) for hh in range(H_RET)]
    zmq = proj(OFF_MQ, W_MEM)
    merged = gate0 * _mm(a_in, wpf_ref[...])
    ret_o = [inter[hh] * decq_scr[hh] + _mm((scores[hh] * dmask_scr[hh]).astype(BF16), rvs[hh])
             for hh in range(H_RET)]
    for hh in range(H_RET):
        st_ref[0, hh] = ssc_scr[hh] * states[hh] + _mm((rkt[hh] * deck_scr[hh]).astype(BF16), rvs[hh])
    gate1 = _gate(h, w_ref, bm_ref, 1)
    zmg = proj(OFF_MG, W_MEM)

    mem_s = []
    for hh in range(H_MEM):
        cols = slice(hh * DH_MEM, (hh + 1) * DH_MEM)
        q = _rms_rows(zmq[:, cols], gmq_ref[...]).astype(BF16)
        mem_s.append(_mm(q, mkt_ref[0, cols, :]) * (DH_MEM ** -0.5))
    r_in = jnp.concatenate(
        [_group_norm(ret_o[hh]) * _silu(zrg[:, hh * DV_RET:(hh + 1) * DV_RET]) for hh in range(H_RET)],
        axis=1).astype(BF16)
    merged = merged + gate1 * _mm(r_in, wpr_ref[...])
    m_parts = []
    for hh in range(H_MEM):
        cols = slice(hh * DH_MEM, (hh + 1) * DH_MEM)
        p = jnp.exp(mem_s[hh] - jnp.max(mem_s[hh], axis=-1, keepdims=True))
        l = jnp.sum(p, axis=-1, keepdims=True)
        m_parts.append(_mm(p.astype(BF16), mvb_ref[0, :, cols]) / l)
    gate2 = _gate(h, w_ref, bm_ref, 2)
    m_in = (jnp.concatenate(m_parts, axis=1) * _silu(zmg)).astype(BF16)
    merged = merged + gate2 * _mm(m_in, wpm_ref[...])

    y_ref[0] = x + _mm(merged.astype(BF16), wo_ref[...])


def _resident(shape):
    nd = len(shape)
    return pl.BlockSpec(shape, lambda b, j: (0,) * nd, pipeline_mode=pl.Buffered(1))


def _prompt_call(x, cos, sin, mkt, mvb, wv_t, log_g, g_norm, bf_pad, b_merge, gfq, gfk, gmq, seg, w_main, wpf, wpr, wpm,
                 wo, tb=PROMPT_BLOCK):
    B, T, _ = x.shape
    nt = T // tb
    kernel = functools.partial(_prompt_kernel, tb=tb)
    in_specs = [
        pl.BlockSpec(memory_space=pltpu.SMEM),
        pl.BlockSpec((1, tb, D_MODEL), lambda b, j: (b, j, 0)),
        pl.BlockSpec((tb, DK_RET), lambda b, j: (j, 0)),
        pl.BlockSpec((tb, DK_RET), lambda b, j: (j, 0)),
        pl.BlockSpec((1, W_MEM, N_MEM), lambda b, j: (b, 0, 0)),
        pl.BlockSpec((1, N_MEM, W_MEM), lambda b, j: (b, 0, 0)),
        _resident((1, D_MODEL)),
        _resident((1, LANES)),
        _resident((1, N_BRANCH * D_MODEL)),
        _resident((1, W_FOX)),
        _resident((1, W_FOX)),
        _resident((1, DH_MEM)),
        _resident((W_FOX, W_FOX)),
        _resident((D_MODEL, W_MAIN)),
        _resident((W_FOX, D_MODEL)),
        _resident((W_RET_V, D_MODEL)),
        _resident((W_MEM, D_MODEL)),
        _resident((D_MODEL, D_MODEL)),
        _resident((W_FOX, D_MODEL)),
    ]
    out_specs = [
        pl.BlockSpec((1, tb, D_MODEL), lambda b, j: (b, j, 0)),
        pl.BlockSpec((1, W_FOX, tb), lambda b, j: (b, 0, j)),
        pl.BlockSpec((1, W_FOX, tb), lambda b, j: (b, 0, j)),
        pl.BlockSpec((1, H_FOX, tb), lambda b, j: (b, 0, j)),
        pl.BlockSpec((1, H_RET, DK_RET, DV_RET), lambda b, j: (b, 0, 0, 0)),
    ]
    out_shape = [
        jax.ShapeDtypeStruct((B, T, D_MODEL), F32),
        jax.ShapeDtypeStruct((B, W_FOX, T), F32),
        jax.ShapeDtypeStruct((B, W_FOX, T), F32),
        jax.ShapeDtypeStruct((B, H_FOX, T), F32),
        jax.ShapeDtypeStruct((B, H_RET, DK_RET, DV_RET), F32),
    ]
    scratch = [
        pltpu.VMEM((H_FOX, T, LANES), BF16),
        pltpu.VMEM((W_FOX, T), BF16),
        pltpu.VMEM((H_FOX, LANES, tb), BF16),
        pltpu.VMEM((H_FOX, 1, tb), F32),
        pltpu.VMEM((H_FOX, 1, tb), F32),
        pltpu.VMEM((H_FOX, DH_FOX, tb), F32),
        pltpu.VMEM((1, LANES), F32),
        pltpu.VMEM((H_RET, tb, tb), F32),
        pltpu.VMEM((H_RET, tb, 1), F32),
        pltpu.VMEM((H_RET, 1, tb), F32),
        pltpu.VMEM((H_RET, 1, DV_RET), F32),
    ]
    return pl.pallas_call(
        kernel,
        grid=(B, nt),
        in_specs=in_specs,
        out_specs=out_specs,
        out_shape=out_shape,
        scratch_shapes=scratch,
        compiler_params=pltpu.CompilerParams(
            dimension_semantics=("arbitrary", "arbitrary"), vmem_limit_bytes=VMEM_LIMIT_BYTES),
        name="prompt",
    )(log_g, x, cos, sin, mkt, mvb, g_norm, bf_pad, b_merge, gfq, gfk, gmq, seg, w_main, wpf, wpr, wpm, wo, wv_t)


def _sample_kernel(lg_ref, x_ref, cos_ref, sin_ref, ckt_ref, cvt_ref, clf_ref, st0_ref, cmk_ref, cmv_ref,
                   gn_ref, bf_ref, bm_ref, gfq_ref, gfk_ref, gmq_ref, seg_ref, w_ref, wpf_ref, wpr_ref, wpm_ref,
                   wo_ref,
                   y_ref, fk_ref, fv_ref, lf_ref, st_ref,
                   h_scr, fq_scr, knt_scr, vn_scr, ncn_scr, rq_scr, rkt_scr, rv_scr, mq_scr, fo_scr, ro_scr, mo_scr,
                   *, nb, ts, past):
    b = pl.program_id(0)
    n = nb * ts
    r0 = pl.multiple_of(b * ts, ts)

    @pl.when(b == 0)
    def _project_all():
        h = _rms_rows(x_ref[...], gn_ref[...]).astype(BF16)
        h_scr[...] = h
        fq_scr[...] = _head_norm64(_mm(h, w_ref[:, OFF_FQ:OFF_FQ + W_FOX]), gfq_ref[...], seg_ref) * (DH_FOX ** -0.5)
        fk = _head_norm64(_mm(h, w_ref[:, OFF_FK:OFF_FK + W_FOX]), gfk_ref[...], seg_ref)
        fv = _mm(h, w_ref[:, OFF_FV:OFF_FV + W_FOX])
        logf = _log_sigmoid(_mm(h, w_ref[:, OFF_FF:OFF_FF + LANES]) + bf_ref[...])
        fk_ref[...] = fk
        fv_ref[...] = fv
        lf_ref[...] = logf[:, :H_FOX]
        knt_scr[...] = fk.T.astype(BF16)
        vn_scr[...] = fv.astype(BF16)
        ri = lax.broadcasted_iota(jnp.int32, (n, n), 0)
        ci = lax.broadcasted_iota(jnp.int32, (n, n), 1)
        same = _idiv(ri, ts) == _idiv(ci, ts)
        tril = jnp.where(jnp.logical_and(same, ci <= ri), 1.0, 0.0).astype(BF16)
        ncn_scr[...] = -(_tri_cumsum(tril, logf).T)
        cos = cos_ref[...]
        sin = sin_ref[...]
        zrq = _mm(h, w_ref[:, OFF_RQ:OFF_RQ + W_RET_QK])
        zrk = _mm(h, w_ref[:, OFF_RK:OFF_RK + W_RET_QK])
        for hh in range(H_RET):
            cols = slice(hh * DK_RET, (hh + 1) * DK_RET)
            rq_scr[:, cols] = _rope(zrq[:, cols], cos, sin)
            rkt_scr[cols, :] = (_rope(zrk[:, cols], cos, sin) * (DK_RET ** -0.5)).T
        rv_scr[...] = _mm(h, w_ref[:, OFF_RV:OFF_RV + W_RET_V]).astype(BF16)
        zmq = _mm(h, w_ref[:, OFF_MQ:OFF_MQ + W_MEM])
        for hh in range(H_MEM):
            cols = slice(hh * DH_MEM, (hh + 1) * DH_MEM)
            mq_scr[:, cols] = _rms_rows(zmq[:, cols], gmq_ref[...])

    nr = H_FOX * ts
    fq_b = fq_scr[pl.ds(r0, ts), :]
    row_head = _idiv(lax.broadcasted_iota(jnp.int32, (nr, W_FOX), 0), ts)
    col_head = _idiv(lax.broadcasted_iota(jnp.int32, (nr, W_FOX), 1), DH_FOX)
    head_sel = row_head == col_head
    q_bd = jnp.where(head_sel, jnp.concatenate([fq_b] * H_FOX, axis=0), 0.0).astype(BF16)

    cc = 256
    ri = lax.broadcasted_iota(jnp.int32, (cc, cc), 0)
    ci = lax.broadcasted_iota(jnp.int32, (cc, cc), 1)
    triu = jnp.where(ri <= ci, 1.0, 0.0).astype(BF16)
    carry = jnp.zeros((H_FOX, 1), F32)
    c_chunks = []
    for c in range(past // cc):
        cch = _cumsum_lanes(triu, clf_ref[0, :, c * cc:(c + 1) * cc]) + carry
        carry = cch[:, cc - 1:cc]
        c_chunks.append(cch)
    c_past = jnp.concatenate(c_chunks, axis=1)
    ncp = jnp.concatenate([jnp.broadcast_to(-c_past[hh:hh + 1, :], (ts, past)) for hh in range(H_FOX)], axis=0)
    ncn = ncn_scr[:H_FOX, :] - carry
    ncn = jnp.concatenate([jnp.broadcast_to(ncn[hh:hh + 1, :], (ts, n)) for hh in range(H_FOX)], axis=0)

    s_p = _mm(q_bd, ckt_ref[0].astype(BF16)) + ncp
    s_n = _mm(q_bd, knt_scr[...]) + ncn
    tok = lax.broadcasted_iota(jnp.int32, (nr, n), 1)
    qt = _imod(lax.broadcasted_iota(jnp.int32, (nr, n), 0), ts)
    valid = jnp.logical_and(_idiv(tok, ts) == b, _imod(tok, ts) <= qt)
    s_n = jnp.where(valid, s_n, -jnp.inf)
    m = jnp.maximum(jnp.max(s_p, axis=-1, keepdims=True), jnp.max(s_n, axis=-1, keepdims=True))
    p_p = jnp.exp(s_p - m)
    p_n = jnp.exp(s_n - m)
    l = jnp.sum(p_p, axis=-1, keepdims=True) + jnp.sum(p_n, axis=-1, keepdims=True)
    o_past = _mm(cvt_ref[0].astype(BF16), p_p.T.astype(BF16)).T
    o_bd = (o_past + _mm(p_n.astype(BF16), vn_scr[...])) / l
    o_bd = jnp.where(head_sel, o_bd, 0.0)
    fo = o_bd[0:ts, :]
    for hh in range(1, H_FOX):
        fo = fo + o_bd[hh * ts:(hh + 1) * ts, :]
    fo_scr[pl.ds(r0, ts), :] = fo

    tokr = lax.broadcasted_iota(jnp.int32, (ts, n), 1)
    qtr = lax.broadcasted_iota(jnp.int32, (ts, n), 0)
    in_b = _idiv(tokr, ts) == b
    dt = (qtr - _imod(tokr, ts)).astype(F32)
    tokc = lax.broadcasted_iota(jnp.int32, (1, n), 1)
    kpos = _imod(tokc, ts).astype(F32)
    rowi = lax.broadcasted_iota(jnp.int32, (ts, 1), 0).astype(F32)
    r_parts = []
    for hh in range(H_RET):
        lg = lg_ref[hh]
        q = rq_scr[pl.ds(r0, ts), hh * DK_RET:(hh + 1) * DK_RET].astype(BF16)
        kt = rkt_scr[hh * DK_RET:(hh + 1) * DK_RET, :]
        v = rv_scr[:, hh * DV_RET:(hh + 1) * DV_RET]
        state = st0_ref[0, hh]
        dmask = jnp.where(jnp.logical_and(in_b, dt >= 0.0), jnp.exp(lg * jnp.maximum(dt, 0.0)), 0.0)
        inter = _mm(q, state.astype(BF16)) * jnp.exp(lg * (rowi + 1.0))
        scores = _mm(q, kt.astype(BF16)) * dmask
        r_parts.append(inter + _mm(scores.astype(BF16), v))
        deck = jnp.where(_idiv(tokc, ts) == b,jnp.exp(lg * (ts - 1.0 - kpos)), 0.0)
        st_ref[0, hh] = jnp.exp(jnp.full((1, DV_RET), lg * ts, F32)) * state + _mm((kt * deck).astype(BF16), v)
    ro_scr[pl.ds(r0, ts), :] = jnp.concatenate(r_parts, axis=1)

    m_parts = []
    for hh in range(H_MEM):
        cols = slice(hh * DH_MEM, (hh + 1) * DH_MEM)
        q = mq_scr[pl.ds(r0, ts), cols].astype(BF16)
        s = _mm_nt(q, cmk_ref[0, :, cols].astype(BF16)) * (DH_MEM ** -0.5)
        p = jnp.exp(s - jnp.max(s, axis=-1, keepdims=True))
        l = jnp.sum(p, axis=-1, keepdims=True)
        m_parts.append(_mm(p.astype(BF16), cmv_ref[0, :, cols].astype(BF16)) / l)
    mo_scr[pl.ds(r0, ts), :] = jnp.concatenate(m_parts, axis=1)

    @pl.when(b == nb - 1)
    def _output_all():
        h = h_scr[...]
        a_in = (fo_scr[...] * _silu(_mm(h, w_ref[:, OFF_FG:OFF_FG + W_FOX]))).astype(BF16)
        merged = _gate(h, w_ref, bm_ref, 0) * _mm(a_in, wpf_ref[...])
        zrg = _mm(h, w_ref[:, OFF_RG:OFF_RG + W_RET_V])
        ro = ro_scr[...]
        r_parts = []
        for hh in range(H_RET):
            cols = slice(hh * DV_RET, (hh + 1) * DV_RET)
            r_parts.append(_group_norm(ro[:, cols]) * _silu(zrg[:, cols]))
        r_in = jnp.concatenate(r_parts, axis=1).astype(BF16)
        merged = merged + _gate(h, w_ref, bm_ref, 1) * _mm(r_in, wpr_ref[...])
        m_in = (mo_scr[...] * _silu(_mm(h, w_ref[:, OFF_MG:OFF_MG + W_MEM]))).astype(BF16)
        merged = merged + _gate(h, w_ref, bm_ref, 2) * _mm(m_in, wpm_ref[...])
        y_ref[...] = x_ref[...] + _mm(merged.astype(BF16), wo_ref[...])


def _sample_call(x2d, cos, sin, ckt, cvt, clf_t, st0, cmk, cmv, log_g, g_norm, bf_pad, b_merge, gfq, gfk, gmq, seg,
                 w_main, wpf, wpr, wpm, wo, nb, ts):
    n = nb * ts
    past = ckt.shape[2]
    kernel = functools.partial(_sample_kernel, nb=nb, ts=ts, past=past)

    def res(shape):
        nd = len(shape)
        return pl.BlockSpec(shape, lambda b: (0,) * nd, pipeline_mode=pl.Buffered(1))

    in_specs = [
        pl.BlockSpec(memory_space=pltpu.SMEM),
        res((n, D_MODEL)),
        res((n, DK_RET)),
        res((n, DK_RET)),
        pl.BlockSpec((1, W_FOX, past), lambda b: (b, 0, 0)),
        pl.BlockSpec((1, W_FOX, past), lambda b: (b, 0, 0)),
        pl.BlockSpec((1, H_FOX, past), lambda b: (b, 0, 0)),
        pl.BlockSpec((1, H_RET, DK_RET, DV_RET), lambda b: (b, 0, 0, 0)),
        pl.BlockSpec((1, N_MEM, W_MEM), lambda b: (b, 0, 0)),
        pl.BlockSpec((1, N_MEM, W_MEM), lambda b: (b, 0, 0)),
        res((1, D_MODEL)),
        res((1, LANES)),
        res((1, N_BRANCH * D_MODEL)),
        res((1, W_FOX)),
        res((1, W_FOX)),
        res((1, DH_MEM)),
        res((W_FOX, W_FOX)),
        res((D_MODEL, W_MAIN)),
        res((W_FOX, D_MODEL)),
        res((W_RET_V, D_MODEL)),
        res((W_MEM, D_MODEL)),
        res((D_MODEL, D_MODEL)),
    ]
    full = lambda shape: pl.BlockSpec(shape, lambda b: (0,) * len(shape))
    out_specs = [
        full((n, D_MODEL)),
        full((n, W_FOX)),
        full((n, W_FOX)),
        full((n, H_FOX)),
        pl.BlockSpec((1, H_RET, DK_RET, DV_RET), lambda b: (b, 0, 0, 0)),
    ]
    out_shape = [
        jax.ShapeDtypeStruct((n, D_MODEL), F32),
        jax.ShapeDtypeStruct((n, W_FOX), F32),
        jax.ShapeDtypeStruct((n, W_FOX), F32),
        jax.ShapeDtypeStruct((n, H_FOX), F32),
        jax.ShapeDtypeStruct((nb, H_RET, DK_RET, DV_RET), F32),
    ]
    scratch = [
        pltpu.VMEM((n, D_MODEL), BF16),
        pltpu.VMEM((n, W_FOX), F32),
        pltpu.VMEM((W_FOX, n), BF16),
        pltpu.VMEM((n, W_FOX), BF16),
        pltpu.VMEM((LANES, n), F32),
        pltpu.VMEM((n, W_RET_QK), F32),
        pltpu.VMEM((W_RET_QK, n), F32),
        pltpu.VMEM((n, W_RET_V), BF16),
        pltpu.VMEM((n, W_MEM), F32),
        pltpu.VMEM((n, W_FOX), F32),
        pltpu.VMEM((n, W_RET_V), F32),
        pltpu.VMEM((n, W_MEM), F32),
    ]
    return pl.pallas_call(
        kernel,
        grid=(nb,),
        in_specs=in_specs,
        out_specs=out_specs,
        out_shape=out_shape,
        scratch_shapes=scratch,
        compiler_params=pltpu.CompilerParams(
            dimension_semantics=("arbitrary",), vmem_limit_bytes=VMEM_LIMIT_BYTES),
        name="sample",
    )(log_g, x2d, cos, sin, ckt, cvt, clf_t, st0, cmk, cmv, g_norm, bf_pad, b_merge, gfq, gfk, gmq, seg,
      w_main, wpf, wpr, wpm, wo)


def _rope_tables(pos):
    half = DK_RET // 2
    inv = ROPE_BASE ** (-jnp.arange(half, dtype=F32) / half)
    ang = pos.astype(F32)[:, None] * inv[None, :]
    cos = jnp.cos(ang)
    sin = jnp.sin(ang)
    return jnp.concatenate([cos, cos], axis=1), jnp.concatenate([-sin, sin], axis=1)


def _reorder_w_in(w):
    f0 = 3 * W_FOX
    pad = jnp.zeros((D_MODEL, LANES - H_FOX), w.dtype)
    return jnp.concatenate([w[:, :f0], w[:, f0 + H_FOX:], w[:, f0:f0 + H_FOX], pad], axis=1).astype(BF16)


def kernel(x_prompt, x_sample, mem_prompt, cache_fox_k, cache_fox_v, cache_fox_logf, state_ret, cache_mem_k, cache_mem_v,
           g_norm, g_mem_norm, w_in, b_f, b_merge, g_fox_q, g_fox_k, g_mem_q, g_mem_k, w_mem_kv,
           w_p_fox, w_p_ret, w_p_mem, w_out):
    depth = w_in.shape[0]
    assert depth == 1, "single-layer kernel"
    B, T, _ = x_prompt.shape
    nb, ts, _ = x_sample.shape
    past = cache_fox_k.shape[2]

    log_g = jnp.log1p(-jnp.exp2(-5.0 - jnp.arange(H_RET, dtype=F32)))
    head_of_lane = jnp.arange(W_FOX) // DH_FOX
    seg = (head_of_lane[:, None] == head_of_lane[None, :]).astype(BF16)
    w_main = _reorder_w_in(w_in[0])
    gn = g_norm[0].reshape(1, D_MODEL)
    bf_pad = jnp.concatenate([b_f[0], jnp.zeros((LANES - H_FOX,), F32)]).reshape(1, LANES)
    bm = b_merge[0].reshape(1, N_BRANCH * D_MODEL)
    gfq = jnp.tile(g_fox_q[0], H_FOX).reshape(1, W_FOX)
    gfk = jnp.tile(g_fox_k[0], H_FOX).reshape(1, W_FOX)
    gmq = g_mem_q[0].reshape(1, DH_MEM)
    wpf = w_p_fox[0].astype(BF16)
    wpr = w_p_ret[0].astype(BF16)
    wpm = w_p_mem[0].astype(BF16)
    wo = w_out[0].astype(BF16)
    shared = (gn, bf_pad, bm, gfq, gfk, gmq, seg, w_main, wpf, wpr, wpm, wo)

    mk, mv, mkt, mvb = _memkv_call(mem_prompt, g_mem_norm[0], w_mem_kv[0], g_mem_k[0])

    cos_p, sin_p = _rope_tables(jnp.arange(T, dtype=jnp.int32))
    wv_t = w_in[0][:, 2 * W_FOX:3 * W_FOX].T.astype(BF16)
    y_p, fkt_p, fvt_p, lft_p, st_p = _prompt_call(x_prompt, cos_p, sin_p, mkt, mvb, wv_t, log_g, *shared)

    def time_major(a_t):
        return jnp.transpose(a_t.reshape(a_t.shape[0], H_FOX, DH_FOX, a_t.shape[2]), (0, 3, 1, 2))[None]

    def time_minor(a):
        return jnp.transpose(a, (0, 2, 3, 1)).reshape(a.shape[0], W_FOX, a.shape[1])

    fk_p = time_major(fkt_p)
    fv_p = time_major(fvt_p)
    lf_p = jnp.swapaxes(lft_p, 1, 2)[None]

    pos_s = past + jnp.arange(ts, dtype=jnp.int32)
    cos_s, sin_s = _rope_tables(jnp.tile(pos_s, nb))
    y_s, fk_s, fv_s, lf_s, st_s = _sample_call(
        x_sample.reshape(nb * ts, D_MODEL), cos_s, sin_s,
        time_minor(cache_fox_k[0]), time_minor(cache_fox_v[0]),
        jnp.swapaxes(cache_fox_logf[0], 1, 2), state_ret[0],
        cache_mem_k[0].reshape(nb, N_MEM, W_MEM), cache_mem_v[0].reshape(nb, N_MEM, W_MEM),
        log_g, *shared, nb=nb, ts=ts)

    return (y_p, y_s.reshape(nb, ts, D_MODEL),
            fk_p, fv_p, lf_p,
            st_p[None], mk.reshape(1, B, N_MEM, H_MEM, DH_MEM), mv.reshape(1, B, N_MEM, H_MEM, DH_MEM),
            fk_s.reshape(1, nb, ts, H_FOX, DH_FOX), fv_s.reshape(1, nb, ts, H_FOX, DH_FOX),
            lf_s.reshape(1, nb, ts, H_FOX), st_s[None])
```

```python
import functools

import jax
import jax.numpy as jnp
from jax import lax
from jax.experimental import pallas as pl
from jax.experimental.pallas import tpu as pltpu

D_MODEL = 1024
N_MEM = 256
H_FOX = 8
DH_FOX = 64
H_RET = 4
DK_RET = 128
DV_RET = 256
H_MEM = 4
DH_MEM = 128
W_FOX = H_FOX * DH_FOX
W_RET_QK = H_RET * DK_RET
W_RET_V = H_RET * DV_RET
W_MEM = H_MEM * DH_MEM
N_BRANCH = 3
ROPE_BASE = 10000.0
EPS = 1e-6
LOG2E = 1.4426950408889634

LANES = 128
MXU_TILE = 256
BF16_SUBLANES = 16
V_ROWS = DH_FOX + BF16_SUBLANES
F_ROWS = BF16_SUBLANES

OFF_FQ = 0
OFF_FK = OFF_FQ + W_FOX
OFF_FV = OFF_FK + W_FOX
OFF_FG = OFF_FV + W_FOX
OFF_RQ = OFF_FG + W_FOX
OFF_RK = OFF_RQ + W_RET_QK
OFF_RV = OFF_RK + W_RET_QK
OFF_RG = OFF_RV + W_RET_V
OFF_MQ = OFF_RG + W_RET_V
OFF_MG = OFF_MQ + W_MEM
OFF_GL = OFF_MG + W_MEM
OFF_FF = OFF_GL + N_BRANCH * D_MODEL
W_MAIN = OFF_FF + LANES

PROMPT_BLOCK = 256
VMEM_LIMIT_BYTES = 60 * 1024 * 1024

F32 = jnp.float32
BF16 = jnp.bfloat16


def _mm(a, b):
    return jnp.dot(a, b, preferred_element_type=F32)


def _mm_nt(a, b):
    return lax.dot_general(a, b, (((1,), (1,)), ((), ())), preferred_element_type=F32)


def _idiv(x, d):
    assert d & (d - 1) == 0
    return lax.shift_right_logical(x, d.bit_length() - 1)


def _imod(x, d):
    assert d & (d - 1) == 0
    return jnp.bitwise_and(x, d - 1)


def _rms_rows(x, g):
    ms = jnp.mean(x * x, axis=-1, keepdims=True)
    return x * lax.rsqrt(ms + EPS) * g


def _head_sumsq(z, seg_ref):
    sq = (z * z).astype(BF16)
    return jnp.concatenate(
        [_mm(sq[:, c:c + MXU_TILE], seg_ref[...]) for c in range(0, W_FOX, MXU_TILE)], axis=1)


def _head_norm64(z, g, seg_ref):
    return z * lax.rsqrt(_head_sumsq(z, seg_ref) * (1.0 / DH_FOX) + EPS) * g


def _log_sigmoid(u):
    return jnp.minimum(u, 0.0) - jnp.log1p(jnp.exp(-jnp.abs(u)))


def _silu(u):
    return u * jax.nn.sigmoid(u)


def _split3(a):
    hi = a.astype(BF16)
    r1 = a - hi.astype(F32)
    mid = r1.astype(BF16)
    lo = (r1 - mid.astype(F32)).astype(BF16)
    return hi, mid, lo


def _tri_cumsum(tri, a):
    hi, mid, lo = _split3(a)
    return _mm(tri, hi) + _mm(tri, mid) + _mm(tri, lo)


def _cumsum_lanes(triu, a):
    hi, mid, lo = _split3(a)
    return _mm(hi, triu) + _mm(mid, triu) + _mm(lo, triu)


def _rope(x, cos, sin_signed):
    return x * cos + pltpu.roll(x, DK_RET // 2, 1) * sin_signed


def _group_norm(o):
    mu = jnp.mean(o, axis=-1, keepdims=True)
    d = o - mu
    var = jnp.mean(d * d, axis=-1, keepdims=True)
    return d * lax.rsqrt(var + EPS)


def _gate(h, w_ref, bm_ref, i):
    z = _mm(h, w_ref[:, OFF_GL + i * D_MODEL:OFF_GL + (i + 1) * D_MODEL])
    return jax.nn.sigmoid(z + bm_ref[:, i * D_MODEL:(i + 1) * D_MODEL])


def _memkv_kernel(mem_ref, gmn_ref, w_ref, gmk_ref, mk_ref, mv_ref, mkt_ref, mvb_ref):
    h = _rms_rows(mem_ref[0], gmn_ref[...]).astype(BF16)
    z = _mm(h, w_ref[...])
    ks = []
    for hh in range(H_MEM):
        ks.append(_rms_rows(z[:, hh * DH_MEM:(hh + 1) * DH_MEM], gmk_ref[...]))
    mk = jnp.concatenate(ks, axis=1)
    mv = z[:, W_MEM:]
    mk_ref[0] = mk
    mv_ref[0] = mv
    mkt_ref[0] = mk.T.astype(BF16)
    mvb_ref[0] = mv.astype(BF16)


def _memkv_call(mem, g_mem_norm, w_mem_kv, g_mem_k):
    B = mem.shape[0]
    const = lambda b: (0, 0)
    per_b = lambda b: (b, 0, 0)
    return pl.pallas_call(
        _memkv_kernel,
        grid=(B,),
        in_specs=[
            pl.BlockSpec((1, N_MEM, D_MODEL), per_b),
            pl.BlockSpec((1, D_MODEL), const),
            pl.BlockSpec((D_MODEL, 2 * W_MEM), const),
            pl.BlockSpec((1, DH_MEM), const),
        ],
        out_specs=[
            pl.BlockSpec((1, N_MEM, W_MEM), per_b),
            pl.BlockSpec((1, N_MEM, W_MEM), per_b),
            pl.BlockSpec((1, W_MEM, N_MEM), per_b),
            pl.BlockSpec((1, N_MEM, W_MEM), per_b),
        ],
        out_shape=[
            jax.ShapeDtypeStruct((B, N_MEM, W_MEM), F32),
            jax.ShapeDtypeStruct((B, N_MEM, W_MEM), F32),
            jax.ShapeDtypeStruct((B, W_MEM, N_MEM), BF16),
            jax.ShapeDtypeStruct((B, N_MEM, W_MEM), BF16),
        ],
        compiler_params=pltpu.CompilerParams(dimension_semantics=("arbitrary",)),
        name="memkv",
    )(mem, g_mem_norm.reshape(1, D_MODEL), w_mem_kv.astype(BF16), g_mem_k.reshape(1, DH_MEM))


def _prompt_kernel(lg_ref, x_ref, cos_ref, sin_ref, mkt_ref, mvb_ref, gn_ref, bf_ref, bm_ref, gfq_ref, gfk_ref,
                   gmq_ref, seg_ref, w_ref, wpf_ref, wpr_ref, wpm_ref, wo_ref, wvt_ref,
                   y_ref, fkt_ref, fvt_ref, lft_ref, st_ref,
                   k_scr, vt_scr, qt_scr, m_scr, acc_scr, sa_scr, sb_scr, ccar_scr, dmask_scr, decq_scr, deck_scr, ssc_scr,
                   *, tb):
    b = pl.program_id(0)
    j = pl.program_id(1)
    t0 = pl.multiple_of(j * tb, tb)

    @pl.when(jnp.logical_and(b == 0, j == 0))
    def _init_tables():
        ri = lax.broadcasted_iota(jnp.int32, (tb, tb), 0)
        ci = lax.broadcasted_iota(jnp.int32, (tb, tb), 1)
        diff = (ri - ci).astype(F32)
        rowi = lax.broadcasted_iota(jnp.int32, (tb, 1), 0).astype(F32)
        coli = lax.broadcasted_iota(jnp.int32, (1, tb), 1).astype(F32)
        for hh in range(H_RET):
            lg = lg_ref[hh]
            dmask_scr[hh] = jnp.where(diff >= 0.0, jnp.exp(lg * jnp.maximum(diff, 0.0)), 0.0)
            decq_scr[hh] = jnp.exp(lg * (rowi + 1.0))
            deck_scr[hh] = jnp.exp(lg * (tb - 1.0 - coli))
            ssc_scr[hh] = jnp.exp(jnp.full((1, DV_RET), lg * tb, F32))

    @pl.when(j == 0)
    def _init_batch():
        ccar_scr[...] = jnp.zeros_like(ccar_scr)
        st_ref[...] = jnp.zeros_like(st_ref)

    def proj(off, width):
        return _mm(h, w_ref[:, off:off + width])

    x = x_ref[0]
    h = _rms_rows(x, gn_ref[...]).astype(BF16)

    zfq = proj(OFF_FQ, W_FOX)
    zfk = proj(OFF_FK, W_FOX)
    vz_t = _mm_nt(wvt_ref[...], h)
    fv_t = vz_t[:W_FOX]
    zrq = proj(OFF_RQ, W_RET_QK)
    ssq = _head_sumsq(zfq, seg_ref)
    ssk = _head_sumsq(zfk, seg_ref)
    zrk = proj(OFF_RK, W_RET_QK)
    logf_t = _log_sigmoid(vz_t[W_FOX:] + bf_ref[...])
    ri = lax.broadcasted_iota(jnp.int32, (tb, tb), 0)
    ci = lax.broadcasted_iota(jnp.int32, (tb, tb), 1)
    key_le_query = ri <= ci
    c_t = _cumsum_lanes(jnp.where(key_le_query, 1.0, 0.0).astype(BF16), logf_t) + ccar_scr[...]
    ccar_scr[...] = c_t[:, tb - 1:tb]
    rv = proj(OFF_RV, W_RET_V).astype(BF16)
    fq = zfq * lax.rsqrt(ssq * (1.0 / DH_FOX) + EPS) * gfq_ref[...] * (DH_FOX ** -0.5 * LOG2E)
    fk = zfk * lax.rsqrt(ssk * (1.0 / DH_FOX) + EPS) * gfk_ref[...]
    fkt_ref[0] = fk.T
    fvt_ref[0] = fv_t
    lft_ref[0] = logf_t[:H_FOX]
    hi, mid, lo = _split3(c_t[:H_FOX] * (-LOG2E))
    parts_t = jnp.concatenate(
        [hi.astype(F32), mid.astype(F32), lo.astype(F32), jnp.zeros((LANES - 3 * H_FOX, tb), F32)], axis=0)
    bias = pltpu.roll(parts_t.T, DH_FOX, 1)
    lane = lax.broadcasted_iota(jnp.int32, (tb, LANES), 1)
    fq_t = fq.T
    vrow = lax.broadcasted_iota(jnp.int32, (V_ROWS - DH_FOX, tb), 0)
    ones_row = jnp.where(vrow == 0, 1.0, 0.0).astype(BF16)
    fv_tb = fv_t.astype(BF16)
    for hh in range(H_FOX):
        vt_scr[hh, :, pl.ds(t0, tb)] = jnp.concatenate([fv_tb[hh * DH_FOX:(hh + 1) * DH_FOX, :], ones_row], axis=0)
    qrow = lax.broadcasted_iota(jnp.int32, (LANES - DH_FOX, tb), 0)
    k_blk = []
    qt_blk = []
    for hh in range(H_FOX):
        pair = fk[:, (hh // 2) * LANES:(hh // 2 + 1) * LANES]
        if hh % 2:
            pair = pltpu.roll(pair, DH_FOX, 1)
        k_aug = jnp.where(lane < DH_FOX, pair, bias).astype(BF16)
        k_scr[hh, pl.ds(t0, tb), :] = k_aug
        k_blk.append(k_aug)
        ones = jnp.where(
            jnp.logical_or(qrow == hh, jnp.logical_or(qrow == H_FOX + hh, qrow == 2 * H_FOX + hh)), 1.0, 0.0)
        qt_aug = jnp.concatenate([fq_t[hh * DH_FOX:(hh + 1) * DH_FOX, :], ones], axis=0).astype(BF16)
        qt_scr[hh] = qt_aug
        qt_blk.append(qt_aug)

    s_all = [_mm(k_blk[hh], qt_blk[hh]) for hh in range(H_FOX)]
    zrg = proj(OFF_RG, W_RET_V)
    p_all = []
    for hh in range(H_FOX):
        s_t = jnp.where(key_le_query, s_all[hh], -jnp.inf)
        m = jnp.max(s_t, axis=0, keepdims=True)
        m_scr[hh] = m
        p_all.append(jnp.exp2((s_t - m).astype(BF16)))
    for hh in range(H_FOX):
        acc_scr[hh] = _mm(vt_scr[hh, :, pl.ds(t0, tb)], p_all[hh])

    def scores_into(buf, kb):
        off = pl.multiple_of(kb * tb, tb)
        for hh in range(H_FOX):
            buf[hh] = _mm(k_scr[hh, pl.ds(off, tb), :], qt_scr[hh])

    def absorb(buf, kb):
        off = pl.multiple_of(kb * tb, tb)
        p_all = []
        alphas = []
        for hh in range(H_FOX):
            s_t = buf[hh]
            m_old = m_scr[hh]
            m_new = jnp.maximum(m_old, jnp.max(s_t, axis=0, keepdims=True))
            alpha = jnp.exp2(m_old - m_new)
            m_scr[hh] = m_new
            p_all.append(jnp.exp2((s_t - m_new).astype(BF16)))
            alphas.append(alpha)
        for hh in range(H_FOX):
            acc_scr[hh] = alphas[hh] * acc_scr[hh] + _mm(vt_scr[hh, :, pl.ds(off, tb)], p_all[hh])

    scores_into(sa_scr, 0)

    def kv_pair(i, carry):
        kb = 2 * i
        scores_into(sb_scr, kb + 1)
        absorb(sa_scr, kb)
        scores_into(sa_scr, kb + 2)
        absorb(sb_scr, kb + 1)
        return carry

    lax.fori_loop(0, lax.shift_right_logical(j, 1), kv_pair, 0)

    @pl.when(jnp.bitwise_and(j, 1) == 1)
    def _last_odd_block():
        absorb(sa_scr, j - 1)

    zfg = proj(OFF_FG, W_FOX)
    gate0 = _gate(h, w_ref, bm_ref, 0)
    fo = jnp.concatenate(
        [acc_scr[hh, :DH_FOX, :] / acc_scr[hh, DH_FOX:DH_FOX + 1, :] for hh in range(H_FOX)], axis=0).T
    a_in = (fo * _silu(zfg)).astype(BF16)

    cos = cos_ref[...]
    sin = sin_ref[...]
    rq, rkt, rvs, states = [], [], [], []
    for hh in range(H_RET):
        cols = slice(hh * DK_RET, (hh + 1) * DK_RET)
        rq.append(_rope(zrq[:, cols], cos, sin).astype(BF16))
        rkt.append((_rope(zrk[:, cols], cos, sin) * (DK_RET ** -0.5)).T)
        rvs.append(rv[:, hh * DV_RET:(hh + 1) * DV_RET])
        states.append(st_ref[0, hh])
    inter = [_mm(rq[hh], states[hh].astype(BF16)) for hh in range(H_RET)]
    scores = [_mm(rq[hh], rkt[hh].astype(BF16)) for hh in range(H_RET)]
    zmq = proj(OFF_MQ, W_MEM)
    merged = gate0 * _mm(a_in, wpf_ref[...])
    ret_o = [inter[hh] * decq_scr[hh] + _mm((scores[hh] * dmask_scr[hh]).astype(BF16), rvs[hh])
             for hh in range(H_RET)]
    for hh in range(H_RET):
        st_ref[0, hh] = ssc_scr[hh] * states[hh] + _mm((rkt[hh] * deck_scr[hh]).astype(BF16), rvs[hh])
    gate1 = _gate(h, w_ref, bm_ref, 1)
    zmg = proj(OFF_MG, W_MEM)

    mem_s = []
    for hh in range(H_MEM):
        cols = slice(hh * DH_MEM, (hh + 1) * DH_MEM)
        q = _rms_rows(zmq[:, cols], gmq_ref[...]).astype(BF16)
        mem_s.append(_mm(q, mkt_ref[0, cols, :]) * (DH_MEM ** -0.5))
    r_in = jnp.concatenate(
        [_group_norm(ret_o[hh]) * _silu(zrg[:, hh * DV_RET:(hh + 1) * DV_RET]) for hh in range(H_RET)],
        axis=1).astype(BF16)
    merged = merged + gate1 * _mm(r_in, wpr_ref[...])
    m_parts = []
    for hh in range(H_MEM):
        cols = slice(hh * DH_MEM, (hh + 1) * DH_MEM)
        p = jnp.exp(mem_s[hh] - jnp.max(mem_s[hh], axis=-1, keepdims=True))
        l = jnp.sum(p, axis=-1, keepdims=True)
        m_parts.append(_mm(p.astype(BF16), mvb_ref[0, :, cols]) / l)
    gate2 = _gate(h, w_ref, bm_ref, 2)
    m_in = (jnp.concatenate(m_parts, axis=1) * _silu(zmg)).astype(BF16)
    merged = merged + gate2 * _mm(m_in, wpm_ref[...])

    y_ref[0] = x + _mm(merged.astype(BF16), wo_ref[...])


def _resident(shape):
    nd = len(shape)
    return pl.BlockSpec(shape, lambda b, j: (0,) * nd, pipeline_mode=pl.Buffered(1))


def _prompt_call(x, cos, sin, mkt, mvb, wvf_t, bf_col, log_g, g_norm, b_merge, gfq, gfk, gmq, seg, w_main, wpf, wpr,
                 wpm, wo, tb=PROMPT_BLOCK):
    B, T, _ = x.shape
    nt = T // tb
    kernel = functools.partial(_prompt_kernel, tb=tb)
    in_specs = [
        pl.BlockSpec(memory_space=pltpu.SMEM),
        pl.BlockSpec((1, tb, D_MODEL), lambda b, j: (b, j, 0)),
        pl.BlockSpec((tb, DK_RET), lambda b, j: (j, 0)),
        pl.BlockSpec((tb, DK_RET), lambda b, j: (j, 0)),
        pl.BlockSpec((1, W_MEM, N_MEM), lambda b, j: (b, 0, 0)),
        pl.BlockSpec((1, N_MEM, W_MEM), lambda b, j: (b, 0, 0)),
        _resident((1, D_MODEL)),
        _resident((F_ROWS, 1)),
        _resident((1, N_BRANCH * D_MODEL)),
        _resident((1, W_FOX)),
        _resident((1, W_FOX)),
        _resident((1, DH_MEM)),
        _resident((MXU_TILE, MXU_TILE)),
        _resident((D_MODEL, W_MAIN)),
        _resident((W_FOX, D_MODEL)),
        _resident((W_RET_V, D_MODEL)),
        _resident((W_MEM, D_MODEL)),
        _resident((D_MODEL, D_MODEL)),
        _resident((W_FOX + F_ROWS, D_MODEL)),
    ]
    out_specs = [
        pl.BlockSpec((1, tb, D_MODEL), lambda b, j: (b, j, 0)),
        pl.BlockSpec((1, W_FOX, tb), lambda b, j: (b, 0, j)),
        pl.BlockSpec((1, W_FOX, tb), lambda b, j: (b, 0, j)),
        pl.BlockSpec((1, H_FOX, tb), lambda b, j: (b, 0, j)),
        pl.BlockSpec((1, H_RET, DK_RET, DV_RET), lambda b, j: (b, 0, 0, 0)),
    ]
    out_shape = [
        jax.ShapeDtypeStruct((B, T, D_MODEL), F32),
        jax.ShapeDtypeStruct((B, W_FOX, T), F32),
        jax.ShapeDtypeStruct((B, W_FOX, T), F32),
        jax.ShapeDtypeStruct((B, H_FOX, T), F32),
        jax.ShapeDtypeStruct((B, H_RET, DK_RET, DV_RET), F32),
    ]
    scratch = [
        pltpu.VMEM((H_FOX, T, LANES), BF16),
        pltpu.VMEM((H_FOX, V_ROWS, T), BF16),
        pltpu.VMEM((H_FOX, LANES, tb), BF16),
        pltpu.VMEM((H_FOX, 1, tb), F32),
        pltpu.VMEM((H_FOX, V_ROWS, tb), F32),
        pltpu.VMEM((H_FOX, tb, tb), F32),
        pltpu.VMEM((H_FOX, tb, tb), F32),
        pltpu.VMEM((F_ROWS, 1), F32),
        pltpu.VMEM((H_RET, tb, tb), F32),
        pltpu.VMEM((H_RET, tb, 1), F32),
        pltpu.VMEM((H_RET, 1, tb), F32),
        pltpu.VMEM((H_RET, 1, DV_RET), F32),
    ]
    return pl.pallas_call(
        kernel,
        grid=(B, nt),
        in_specs=in_specs,
        out_specs=out_specs,
        out_shape=out_shape,
        scratch_shapes=scratch,
        compiler_params=pltpu.CompilerParams(
            dimension_semantics=("arbitrary", "arbitrary"), vmem_limit_bytes=VMEM_LIMIT_BYTES),
        name="prompt",
    )(log_g, x, cos, sin, mkt, mvb, g_norm, bf_col, b_merge, gfq, gfk, gmq, seg, w_main, wpf, wpr, wpm, wo, wvf_t)


def _sample_kernel(lg_ref, x_ref, cos_ref, sin_ref, ckt_ref, cvt_ref, clf_ref, st0_ref, cmk_ref, cmv_ref,
                   gn_ref, bf_ref, bm_ref, gfq_ref, gfk_ref, gmq_ref, seg_ref, w_ref, wpf_ref, wpr_ref, wpm_ref,
                   wo_ref,
                   y_ref, fk_ref, fv_ref, lf_ref, st_ref,
                   h_scr, fq_scr, knt_scr, vn_scr, ncn_scr, rq_scr, rkt_scr, rv_scr, mq_scr, fo_scr, ro_scr, mo_scr,
                   *, nb, ts, past):
    b = pl.program_id(0)
    n = nb * ts
    r0 = pl.multiple_of(b * ts, ts)

    @pl.when(b == 0)
    def _project_all():
        h = _rms_rows(x_ref[...], gn_ref[...]).astype(BF16)
        h_scr[...] = h
        fq_scr[...] = _head_norm64(_mm(h, w_ref[:, OFF_FQ:OFF_FQ + W_FOX]), gfq_ref[...], seg_ref) * (DH_FOX ** -0.5)
        fk = _head_norm64(_mm(h, w_ref[:, OFF_FK:OFF_FK + W_FOX]), gfk_ref[...], seg_ref)
        fv = _mm(h, w_ref[:, OFF_FV:OFF_FV + W_FOX])
        logf = _log_sigmoid(_mm(h, w_ref[:, OFF_FF:OFF_FF + LANES]) + bf_ref[...])
        fk_ref[...] = fk
        fv_ref[...] = fv
        lf_ref[...] = logf[:, :H_FOX]
        knt_scr[...] = fk.T.astype(BF16)
        vn_scr[...] = fv.astype(BF16)
        ri = lax.broadcasted_iota(jnp.int32, (n, n), 0)
        ci = lax.broadcasted_iota(jnp.int32, (n, n), 1)
        same = _idiv(ri, ts) == _idiv(ci, ts)
        tril = jnp.where(jnp.logical_and(same, ci <= ri), 1.0, 0.0).astype(BF16)
        ncn_scr[...] = -(_tri_cumsum(tril, logf).T)
        cos = cos_ref[...]
        sin = sin_ref[...]
        zrq = _mm(h, w_ref[:, OFF_RQ:OFF_RQ + W_RET_QK])
        zrk = _mm(h, w_ref[:, OFF_RK:OFF_RK + W_RET_QK])
        for hh in range(H_RET):
            cols = slice(hh * DK_RET, (hh + 1) * DK_RET)
            rq_scr[:, cols] = _rope(zrq[:, cols], cos, sin)
            rkt_scr[cols, :] = (_rope(zrk[:, cols], cos, sin) * (DK_RET ** -0.5)).T
        rv_scr[...] = _mm(h, w_ref[:, OFF_RV:OFF_RV + W_RET_V]).astype(BF16)
        zmq = _mm(h, w_ref[:, OFF_MQ:OFF_MQ + W_MEM])
        for hh in range(H_MEM):
            cols = slice(hh * DH_MEM, (hh + 1) * DH_MEM)
            mq_scr[:, cols] = _rms_rows(zmq[:, cols], gmq_ref[...])

    nr = H_FOX * ts
    fq_b = fq_scr[pl.ds(r0, ts), :]
    row_head = _idiv(lax.broadcasted_iota(jnp.int32, (nr, W_FOX), 0), ts)
    col_head = _idiv(lax.broadcasted_iota(jnp.int32, (nr, W_FOX), 1), DH_FOX)
    head_sel = row_head == col_head
    q_bd = jnp.where(head_sel, jnp.concatenate([fq_b] * H_FOX, axis=0), 0.0).astype(BF16)

    cc = 256
    ri = lax.broadcasted_iota(jnp.int32, (cc, cc), 0)
    ci = lax.broadcasted_iota(jnp.int32, (cc, cc), 1)
    triu = jnp.where(ri <= ci, 1.0, 0.0).astype(BF16)
    carry = jnp.zeros((H_FOX, 1), F32)
    c_chunks = []
    for c in range(past // cc):
        cch = _cumsum_lanes(triu, clf_ref[0, :, c * cc:(c + 1) * cc]) + carry
        carry = cch[:, cc - 1:cc]
        c_chunks.append(cch)
    c_past = jnp.concatenate(c_chunks, axis=1)
    ncp = jnp.concatenate([jnp.broadcast_to(-c_past[hh:hh + 1, :], (ts, past)) for hh in range(H_FOX)], axis=0)
    ncn = ncn_scr[:H_FOX, :] - carry
    ncn = jnp.concatenate([jnp.broadcast_to(ncn[hh:hh + 1, :], (ts, n)) for hh in range(H_FOX)], axis=0)

    s_p = _mm(q_bd, ckt_ref[0].astype(BF16)) + ncp
    s_n = _mm(q_bd, knt_scr[...]) + ncn
    tok = lax.broadcasted_iota(jnp.int32, (nr, n), 1)
    qt = _imod(lax.broadcasted_iota(jnp.int32, (nr, n), 0), ts)
    valid = jnp.logical_and(_idiv(tok, ts) == b, _imod(tok, ts) <= qt)
    s_n = jnp.where(valid, s_n, -jnp.inf)
    m = jnp.maximum(jnp.max(s_p, axis=-1, keepdims=True), jnp.max(s_n, axis=-1, keepdims=True))
    p_p = jnp.exp(s_p - m)
    p_n = jnp.exp(s_n - m)
    l = jnp.sum(p_p, axis=-1, keepdims=True) + jnp.sum(p_n, axis=-1, keepdims=True)
    o_past = _mm(cvt_ref[0].astype(BF16), p_p.T.astype(BF16)).T
    o_bd = (o_past + _mm(p_n.astype(BF16), vn_scr[...])) / l
    o_bd = jnp.where(head_sel, o_bd, 0.0)
    fo = o_bd[0:ts, :]
    for hh in range(1, H_FOX):
        fo = fo + o_bd[hh * ts:(hh + 1) * ts, :]
    fo_scr[pl.ds(r0, ts), :] = fo

    tokr = lax.broadcasted_iota(jnp.int32, (ts, n), 1)
    qtr = lax.broadcasted_iota(jnp.int32, (ts, n), 0)
    in_b = _idiv(tokr, ts) == b
    dt = (qtr - _imod(tokr, ts)).astype(F32)
    tokc = lax.broadcasted_iota(jnp.int32, (1, n), 1)
    kpos = _imod(tokc, ts).astype(F32)
    rowi = lax.broadcasted_iota(jnp.int32, (ts, 1), 0).astype(F32)
    r_parts = []
    for hh in range(H_RET):
        lg = lg_ref[hh]
        q = rq_scr[pl.ds(r0, ts), hh * DK_RET:(hh + 1) * DK_RET].astype(BF16)
        kt = rkt_scr[hh * DK_RET:(hh + 1) * DK_RET, :]
        v = rv_scr[:, hh * DV_RET:(hh + 1) * DV_RET]
        state = st0_ref[0, hh]
        dmask = jnp.where(jnp.logical_and(in_b, dt >= 0.0), jnp.exp(lg * jnp.maximum(dt, 0.0)), 0.0)
        inter = _mm(q, state.astype(BF16)) * jnp.exp(lg * (rowi + 1.0))
        scores = _mm(q, kt.astype(BF16)) * dmask
        r_parts.append(inter + _mm(scores.astype(BF16), v))
        deck = jnp.where(_idiv(tokc, ts) == b,jnp.exp(lg * (ts - 1.0 - kpos)), 0.0)
        st_ref[0, hh] = jnp.exp(jnp.full((1, DV_RET), lg * ts, F32)) * state + _mm((kt * deck).astype(BF16), v)
    ro_scr[pl.ds(r0, ts), :] = jnp.concatenate(r_parts, axis=1)

    m_parts = []
    for hh in range(H_MEM):
        cols = slice(hh * DH_MEM, (hh + 1) * DH_MEM)
        q = mq_scr[pl.ds(r0, ts), cols].astype(BF16)
        s = _mm_nt(q, cmk_ref[0, :, cols].astype(BF16)) * (DH_MEM ** -0.5)
        p = jnp.exp(s - jnp.max(s, axis=-1, keepdims=True))
        l = jnp.sum(p, axis=-1, keepdims=True)
        m_parts.append(_mm(p.astype(BF16), cmv_ref[0, :, cols].astype(BF16)) / l)
    mo_scr[pl.ds(r0, ts), :] = jnp.concatenate(m_parts, axis=1)

    @pl.when(b == nb - 1)
    def _output_all():
        h = h_scr[...]
        a_in = (fo_scr[...] * _silu(_mm(h, w_ref[:, OFF_FG:OFF_FG + W_FOX]))).astype(BF16)
        merged = _gate(h, w_ref, bm_ref, 0) * _mm(a_in, wpf_ref[...])
        zrg = _mm(h, w_ref[:, OFF_RG:OFF_RG + W_RET_V])
        ro = ro_scr[...]
        r_parts = []
        for hh in range(H_RET):
            cols = slice(hh * DV_RET, (hh + 1) * DV_RET)
            r_parts.append(_group_norm(ro[:, cols]) * _silu(zrg[:, cols]))
        r_in = jnp.concatenate(r_parts, axis=1).astype(BF16)
        merged = merged + _gate(h, w_ref, bm_ref, 1) * _mm(r_in, wpr_ref[...])
        m_in = (mo_scr[...] * _silu(_mm(h, w_ref[:, OFF_MG:OFF_MG + W_MEM]))).astype(BF16)
        merged = merged + _gate(h, w_ref, bm_ref, 2) * _mm(m_in, wpm_ref[...])
        y_ref[...] = x_ref[...] + _mm(merged.astype(BF16), wo_ref[...])


def _sample_call(x2d, cos, sin, ckt, cvt, clf_t, st0, cmk, cmv, log_g, g_norm, bf_pad, b_merge, gfq, gfk, gmq, seg,
                 w_main, wpf, wpr, wpm, wo, nb, ts):
    n = nb * ts
    past = ckt.shape[2]
    kernel = functools.partial(_sample_kernel, nb=nb, ts=ts, past=past)

    def res(shape):
        nd = len(shape)
        return pl.BlockSpec(shape, lambda b: (0,) * nd, pipeline_mode=pl.Buffered(1))

    in_specs = [
        pl.BlockSpec(memory_space=pltpu.SMEM),
        res((n, D_MODEL)),
        res((n, DK_RET)),
        res((n, DK_RET)),
        pl.BlockSpec((1, W_FOX, past), lambda b: (b, 0, 0)),
        pl.BlockSpec((1, W_FOX, past), lambda b: (b, 0, 0)),
        pl.BlockSpec((1, H_FOX, past), lambda b: (b, 0, 0)),
        pl.BlockSpec((1, H_RET, DK_RET, DV_RET), lambda b: (b, 0, 0, 0)),
        pl.BlockSpec((1, N_MEM, W_MEM), lambda b: (b, 0, 0)),
        pl.BlockSpec((1, N_MEM, W_MEM), lambda b: (b, 0, 0)),
        res((1, D_MODEL)),
        res((1, LANES)),
        res((1, N_BRANCH * D_MODEL)),
        res((1, W_FOX)),
        res((1, W_FOX)),
        res((1, DH_MEM)),
        res((MXU_TILE, MXU_TILE)),
        res((D_MODEL, W_MAIN)),
        res((W_FOX, D_MODEL)),
        res((W_RET_V, D_MODEL)),
        res((W_MEM, D_MODEL)),
        res((D_MODEL, D_MODEL)),
    ]
    full = lambda shape: pl.BlockSpec(shape, lambda b: (0,) * len(shape))
    out_specs = [
        full((n, D_MODEL)),
        full((n, W_FOX)),
        full((n, W_FOX)),
        full((n, H_FOX)),
        pl.BlockSpec((1, H_RET, DK_RET, DV_RET), lambda b: (b, 0, 0, 0)),
    ]
    out_shape = [
        jax.ShapeDtypeStruct((n, D_MODEL), F32),
        jax.ShapeDtypeStruct((n, W_FOX), F32),
        jax.ShapeDtypeStruct((n, W_FOX), F32),
        jax.ShapeDtypeStruct((n, H_FOX), F32),
        jax.ShapeDtypeStruct((nb, H_RET, DK_RET, DV_RET), F32),
    ]
    scratch = [
        pltpu.VMEM((n, D_MODEL), BF16),
        pltpu.VMEM((n, W_FOX), F32),
        pltpu.VMEM((W_FOX, n), BF16),
        pltpu.VMEM((n, W_FOX), BF16),
        pltpu.VMEM((LANES, n), F32),
        pltpu.VMEM((n, W_RET_QK), F32),
        pltpu.VMEM((W_RET_QK, n), F32),
        pltpu.VMEM((n, W_RET_V), BF16),
        pltpu.VMEM((n, W_MEM), F32),
        pltpu.VMEM((n, W_FOX), F32),
        pltpu.VMEM((n, W_RET_V), F32),
        pltpu.VMEM((n, W_MEM), F32),
    ]
    return pl.pallas_call(
        kernel,
        grid=(nb,),
        in_specs=in_specs,
        out_specs=out_specs,
        out_shape=out_shape,
        scratch_shapes=scratch,
        compiler_params=pltpu.CompilerParams(
            dimension_semantics=("arbitrary",), vmem_limit_bytes=VMEM_LIMIT_BYTES),
        name="sample",
    )(log_g, x2d, cos, sin, ckt, cvt, clf_t, st0, cmk, cmv, g_norm, bf_pad, b_merge, gfq, gfk, gmq, seg,
      w_main, wpf, wpr, wpm, wo)


def _rope_tables(pos):
    half = DK_RET // 2
    inv = ROPE_BASE ** (-jnp.arange(half, dtype=F32) / half)
    ang = pos.astype(F32)[:, None] * inv[None, :]
    cos = jnp.cos(ang)
    sin = jnp.sin(ang)
    return jnp.concatenate([cos, cos], axis=1), jnp.concatenate([-sin, sin], axis=1)


def _reorder_w_in(w):
    rows = 128
    return pl.pallas_call(
        _reorder_kernel,
        grid=(D_MODEL // rows,),
        in_specs=[pl.BlockSpec((rows, w.shape[1]), lambda i: (i, 0))],
        out_specs=pl.BlockSpec((rows, W_MAIN), lambda i: (i, 0)),
        out_shape=jax.ShapeDtypeStruct((D_MODEL, W_MAIN), BF16),
        compiler_params=pltpu.CompilerParams(dimension_semantics=("arbitrary",)),
        name="reorder_w_in",
    )(w)


def _reorder_kernel(w_ref, o_ref):
    f0 = 3 * W_FOX
    w = w_ref[...]
    o_ref[:, :f0] = w[:, :f0].astype(BF16)
    o_ref[:, f0:OFF_FF] = w[:, f0 + H_FOX:].astype(BF16)
    ff = jnp.concatenate([w[:, f0:f0 + H_FOX], jnp.zeros((w.shape[0], LANES - H_FOX), w.dtype)], axis=1)
    o_ref[:, OFF_FF:] = ff.astype(BF16)


def kernel(x_prompt, x_sample, mem_prompt, cache_fox_k, cache_fox_v, cache_fox_logf, state_ret, cache_mem_k, cache_mem_v,
           g_norm, g_mem_norm, w_in, b_f, b_merge, g_fox_q, g_fox_k, g_mem_q, g_mem_k, w_mem_kv,
           w_p_fox, w_p_ret, w_p_mem, w_out):
    depth = w_in.shape[0]
    assert depth == 1, "single-layer kernel"
    B, T, _ = x_prompt.shape
    nb, ts, _ = x_sample.shape
    past = cache_fox_k.shape[2]

    log_g = jnp.log1p(-jnp.exp2(-5.0 - jnp.arange(H_RET, dtype=F32)))
    head_of_lane = jnp.arange(MXU_TILE) // DH_FOX
    seg = (head_of_lane[:, None] == head_of_lane[None, :]).astype(BF16)
    w_main = _reorder_w_in(w_in[0])
    gn = g_norm[0].reshape(1, D_MODEL)
    bf_pad = jnp.concatenate([b_f[0], jnp.zeros((LANES - H_FOX,), F32)]).reshape(1, LANES)
    bm = b_merge[0].reshape(1, N_BRANCH * D_MODEL)
    gfq = jnp.tile(g_fox_q[0], H_FOX).reshape(1, W_FOX)
    gfk = jnp.tile(g_fox_k[0], H_FOX).reshape(1, W_FOX)
    gmq = g_mem_q[0].reshape(1, DH_MEM)
    wpf = w_p_fox[0].astype(BF16)
    wpr = w_p_ret[0].astype(BF16)
    wpm = w_p_mem[0].astype(BF16)
    wo = w_out[0].astype(BF16)
    shared = (gn, bf_pad, bm, gfq, gfk, gmq, seg, w_main, wpf, wpr, wpm, wo)

    mk, mv, mkt, mvb = _memkv_call(mem_prompt, g_mem_norm[0], w_mem_kv[0], g_mem_k[0])

    cos_p, sin_p = _rope_tables(jnp.arange(T, dtype=jnp.int32))
    wvf_t = jnp.concatenate(
        [w_in[0][:, 2 * W_FOX:3 * W_FOX + H_FOX], jnp.zeros((D_MODEL, F_ROWS - H_FOX), F32)], axis=1).T.astype(BF16)
    bf_col = jnp.concatenate([b_f[0], jnp.zeros((F_ROWS - H_FOX,), F32)]).reshape(F_ROWS, 1)
    y_p, fkt_p, fvt_p, lft_p, st_p = _prompt_call(
        x_prompt, cos_p, sin_p, mkt, mvb, wvf_t, bf_col, log_g, gn, bm, gfq, gfk, gmq, seg, w_main, wpf, wpr, wpm, wo)

    def time_major(a_t):
        return jnp.transpose(a_t.reshape(a_t.shape[0], H_FOX, DH_FOX, a_t.shape[2]), (0, 3, 1, 2))[None]

    def time_minor(a):
        return jnp.transpose(a, (0, 2, 3, 1)).reshape(a.shape[0], W_FOX, a.shape[1])

    fk_p = time_major(fkt_p)
    fv_p = time_major(fvt_p)
    lf_p = jnp.swapaxes(lft_p, 1, 2)[None]

    pos_s = past + jnp.arange(ts, dtype=jnp.int32)
    cos_s, sin_s = _rope_tables(jnp.tile(pos_s, nb))
    y_s, fk_s, fv_s, lf_s, st_s = _sample_call(
        x_sample.reshape(nb * ts, D_MODEL), cos_s, sin_s,
        time_minor(cache_fox_k[0]), time_minor(cache_fox_v[0]),
        jnp.swapaxes(cache_fox_logf[0], 1, 2), state_ret[0],
        cache_mem_k[0].reshape(nb, N_MEM, W_MEM), cache_mem_v[0].reshape(nb, N_MEM, W_MEM),
        log_g, *shared, nb=nb, ts=ts)

    return (y_p, y_s.reshape(nb, ts, D_MODEL),
            fk_p, fv_p, lf_p,
            st_p[None], mk.reshape(1, B, N_MEM, H_MEM, DH_MEM), mv.reshape(1, B, N_MEM, H_MEM, DH_MEM),
            fk_s.reshape(1, nb, ts, H_FOX, DH_FOX), fv_s.reshape(1, nb, ts, H_FOX, DH_FOX),
            lf_s.reshape(1, nb, ts, H_FOX), st_s[None])
```

```python
import functools

import jax
import jax.numpy as jnp
from jax import lax
from jax.experimental import pallas as pl
from jax.experimental.pallas import tpu as pltpu

D_MODEL = 1024
N_MEM = 256
H_FOX = 8
DH_FOX = 64
H_RET = 4
DK_RET = 128
DV_RET = 256
H_MEM = 4
DH_MEM = 128
W_FOX = H_FOX * DH_FOX
W_RET_QK = H_RET * DK_RET
W_RET_V = H_RET * DV_RET
W_MEM = H_MEM * DH_MEM
N_BRANCH = 3
ROPE_BASE = 10000.0
EPS = 1e-6
LOG2E = 1.4426950408889634

LANES = 128
MXU_TILE = 256
BF16_SUBLANES = 16
V_ROWS = DH_FOX + BF16_SUBLANES
F_ROWS = BF16_SUBLANES

OFF_FQ = 0
OFF_FK = OFF_FQ + W_FOX
OFF_FV = OFF_FK + W_FOX
OFF_FG = OFF_FV + W_FOX
OFF_RQ = OFF_FG + W_FOX
OFF_RK = OFF_RQ + W_RET_QK
OFF_RV = OFF_RK + W_RET_QK
OFF_RG = OFF_RV + W_RET_V
OFF_MQ = OFF_RG + W_RET_V
OFF_MG = OFF_MQ + W_MEM
OFF_GL = OFF_MG + W_MEM
W_MAIN = OFF_GL + N_BRANCH * D_MODEL
REORDER_ROWS = 512

PROMPT_BLOCK = 256
VMEM_LIMIT_BYTES = 60 * 1024 * 1024

F32 = jnp.float32
BF16 = jnp.bfloat16


def _mm(a, b):
    return jnp.dot(a, b, preferred_element_type=F32)


def _mm_nt(a, b):
    return lax.dot_general(a, b, (((1,), (1,)), ((), ())), preferred_element_type=F32)


def _idiv(x, d):
    assert d & (d - 1) == 0
    return lax.shift_right_logical(x, d.bit_length() - 1)


def _imod(x, d):
    assert d & (d - 1) == 0
    return jnp.bitwise_and(x, d - 1)


def _rms_rows(x, g):
    ms = jnp.mean(x * x, axis=-1, keepdims=True)
    return x * lax.rsqrt(ms + EPS) * g


def _head_sumsq(z, seg_ref):
    sq = (z * z).astype(BF16)
    return jnp.concatenate(
        [_mm(sq[:, c:c + MXU_TILE], seg_ref[...]) for c in range(0, W_FOX, MXU_TILE)], axis=1)


def _head_norm64(z, g, seg_ref):
    return z * lax.rsqrt(_head_sumsq(z, seg_ref) * (1.0 / DH_FOX) + EPS) * g


def _log_sigmoid(u):
    return jnp.minimum(u, 0.0) - jnp.log1p(jnp.exp(-jnp.abs(u)))


def _silu(u):
    return u * jax.nn.sigmoid(u)


def _split3(a):
    hi = a.astype(BF16)
    r1 = a - hi.astype(F32)
    mid = r1.astype(BF16)
    lo = (r1 - mid.astype(F32)).astype(BF16)
    return hi, mid, lo


def _tri_cumsum(tri, a):
    hi, mid, lo = _split3(a)
    return _mm(tri, hi) + _mm(tri, mid) + _mm(tri, lo)


def _cumsum_lanes(triu, a):
    hi, mid, lo = _split3(a)
    return _mm(hi, triu) + _mm(mid, triu) + _mm(lo, triu)


def _rope(x, cos, sin_signed):
    return x * cos + pltpu.roll(x, DK_RET // 2, 1) * sin_signed


def _group_norm(o):
    mu = jnp.mean(o, axis=-1, keepdims=True)
    d = o - mu
    var = jnp.mean(d * d, axis=-1, keepdims=True)
    return d * lax.rsqrt(var + EPS)


def _gate(h, w_ref, bm_ref, i):
    z = _mm(h, w_ref[:, OFF_GL + i * D_MODEL:OFF_GL + (i + 1) * D_MODEL])
    return jax.nn.sigmoid(z + bm_ref[:, i * D_MODEL:(i + 1) * D_MODEL])


def _memkv_kernel(mem_ref, gmn_ref, w_ref, gmk_ref, mk_ref, mv_ref, mkt_ref, mvb_ref):
    h = _rms_rows(mem_ref[0], gmn_ref[...]).astype(BF16)
    z = _mm(h, w_ref[...])
    ks = []
    for hh in range(H_MEM):
        ks.append(_rms_rows(z[:, hh * DH_MEM:(hh + 1) * DH_MEM], gmk_ref[...]))
    mk = jnp.concatenate(ks, axis=1)
    mv = z[:, W_MEM:]
    mk_ref[0] = mk
    mv_ref[0] = mv
    mkt_ref[0] = mk.T.astype(BF16)
    mvb_ref[0] = mv.astype(BF16)


def _memkv_call(mem, g_mem_norm, w_mem_kv, g_mem_k):
    B = mem.shape[0]
    const = lambda b: (0, 0)
    per_b = lambda b: (b, 0, 0)
    return pl.pallas_call(
        _memkv_kernel,
        grid=(B,),
        in_specs=[
            pl.BlockSpec((1, N_MEM, D_MODEL), per_b),
            pl.BlockSpec((1, D_MODEL), const),
            pl.BlockSpec((D_MODEL, 2 * W_MEM), const),
            pl.BlockSpec((1, DH_MEM), const),
        ],
        out_specs=[
            pl.BlockSpec((1, N_MEM, W_MEM), per_b),
            pl.BlockSpec((1, N_MEM, W_MEM), per_b),
            pl.BlockSpec((1, W_MEM, N_MEM), per_b),
            pl.BlockSpec((1, N_MEM, W_MEM), per_b),
        ],
        out_shape=[
            jax.ShapeDtypeStruct((B, N_MEM, W_MEM), F32),
            jax.ShapeDtypeStruct((B, N_MEM, W_MEM), F32),
            jax.ShapeDtypeStruct((B, W_MEM, N_MEM), BF16),
            jax.ShapeDtypeStruct((B, N_MEM, W_MEM), BF16),
        ],
        compiler_params=pltpu.CompilerParams(dimension_semantics=("arbitrary",)),
        name="memkv",
    )(mem, g_mem_norm.reshape(1, D_MODEL), w_mem_kv.astype(BF16), g_mem_k.reshape(1, DH_MEM))


def _prompt_kernel(lg_ref, x_ref, cos_ref, sin_ref, mkt_ref, mvb_ref, gn_ref, bf_ref, bm_ref, gfq_ref, gfk_ref,
                   gmq_ref, seg_ref, w_ref, wpf_ref, wpr_ref, wpm_ref, wo_ref, wvt_ref,
                   y_ref, fkt_ref, fvt_ref, lft_ref, st_ref,
                   k_scr, vt_scr, qt_scr, m_scr, acc_scr, sa_scr, sb_scr, ccar_scr, dmask_scr, decq_scr, deck_scr, ssc_scr,
                   *, tb):
    b = pl.program_id(0)
    j = pl.program_id(1)
    t0 = pl.multiple_of(j * tb, tb)

    @pl.when(jnp.logical_and(b == 0, j == 0))
    def _init_tables():
        ri = lax.broadcasted_iota(jnp.int32, (tb, tb), 0)
        ci = lax.broadcasted_iota(jnp.int32, (tb, tb), 1)
        diff = (ri - ci).astype(F32)
        rowi = lax.broadcasted_iota(jnp.int32, (tb, 1), 0).astype(F32)
        coli = lax.broadcasted_iota(jnp.int32, (1, tb), 1).astype(F32)
        for hh in range(H_RET):
            lg = lg_ref[hh]
            dmask_scr[hh] = jnp.where(diff >= 0.0, jnp.exp(lg * jnp.maximum(diff, 0.0)), 0.0)
            decq_scr[hh] = jnp.exp(lg * (rowi + 1.0))
            deck_scr[hh] = jnp.exp(lg * (tb - 1.0 - coli))
            ssc_scr[hh] = jnp.exp(jnp.full((1, DV_RET), lg * tb, F32))

    @pl.when(j == 0)
    def _init_batch():
        ccar_scr[...] = jnp.zeros_like(ccar_scr)
        st_ref[...] = jnp.zeros_like(st_ref)

    x = x_ref[0]
    h = _rms_rows(x, gn_ref[...]).astype(BF16)

    def proj(off, width):
        return _mm(h, w_ref[:, off:off + width])

    def gate(i):
        return jax.nn.sigmoid(proj(OFF_GL + i * D_MODEL, D_MODEL) + bm_ref[:, i * D_MODEL:(i + 1) * D_MODEL])


    zfq = proj(OFF_FQ, W_FOX)
    zfk = proj(OFF_FK, W_FOX)
    vz_t = _mm_nt(wvt_ref[...], h)
    fv_t = vz_t[:W_FOX]
    zrq = proj(OFF_RQ, W_RET_QK)
    ssq = _head_sumsq(zfq, seg_ref)
    ssk = _head_sumsq(zfk, seg_ref)
    zrk = proj(OFF_RK, W_RET_QK)
    logf_t = _log_sigmoid(vz_t[W_FOX:] + bf_ref[...])
    ri = lax.broadcasted_iota(jnp.int32, (tb, tb), 0)
    ci = lax.broadcasted_iota(jnp.int32, (tb, tb), 1)
    key_le_query = ri <= ci
    c_t = _cumsum_lanes(jnp.where(key_le_query, 1.0, 0.0).astype(BF16), logf_t) + ccar_scr[...]
    ccar_scr[...] = c_t[:, tb - 1:tb]
    rv = proj(OFF_RV, W_RET_V).astype(BF16)
    fq = zfq * lax.rsqrt(ssq * (1.0 / DH_FOX) + EPS) * gfq_ref[...] * (DH_FOX ** -0.5 * LOG2E)
    fk = zfk * lax.rsqrt(ssk * (1.0 / DH_FOX) + EPS) * gfk_ref[...]
    fkt_ref[0] = fk.T
    fvt_ref[0] = fv_t
    lft_ref[0] = logf_t[:H_FOX]
    hi, mid, lo = _split3(c_t[:H_FOX] * (-LOG2E))
    parts_t = jnp.concatenate(
        [hi.astype(F32), mid.astype(F32), lo.astype(F32), jnp.zeros((LANES - 3 * H_FOX, tb), F32)], axis=0)
    bias = pltpu.roll(parts_t.T, DH_FOX, 1)
    lane = lax.broadcasted_iota(jnp.int32, (tb, LANES), 1)
    fq_t = fq.T
    vrow = lax.broadcasted_iota(jnp.int32, (V_ROWS - DH_FOX, tb), 0)
    ones_row = jnp.where(vrow == 0, 1.0, 0.0).astype(BF16)
    fv_tb = fv_t.astype(BF16)
    for hh in range(H_FOX):
        vt_scr[hh, :, pl.ds(t0, tb)] = jnp.concatenate([fv_tb[hh * DH_FOX:(hh + 1) * DH_FOX, :], ones_row], axis=0)
    qrow = lax.broadcasted_iota(jnp.int32, (LANES - DH_FOX, tb), 0)
    k_blk = []
    qt_blk = []
    for hh in range(H_FOX):
        pair = fk[:, (hh // 2) * LANES:(hh // 2 + 1) * LANES]
        if hh % 2:
            pair = pltpu.roll(pair, DH_FOX, 1)
        k_aug = jnp.where(lane < DH_FOX, pair, bias).astype(BF16)
        k_scr[hh, pl.ds(t0, tb), :] = k_aug
        k_blk.append(k_aug)
        ones = jnp.where(
            jnp.logical_or(qrow == hh, jnp.logical_or(qrow == H_FOX + hh, qrow == 2 * H_FOX + hh)), 1.0, 0.0)
        qt_aug = jnp.concatenate([fq_t[hh * DH_FOX:(hh + 1) * DH_FOX, :], ones], axis=0).astype(BF16)
        qt_scr[hh] = qt_aug
        qt_blk.append(qt_aug)

    s_all = [_mm(k_blk[hh], qt_blk[hh]) for hh in range(H_FOX)]
    zrg = proj(OFF_RG, W_RET_V)
    p_all = []
    for hh in range(H_FOX):
        s_t = jnp.where(key_le_query, s_all[hh], -jnp.inf)
        m = jnp.max(s_t, axis=0, keepdims=True)
        m_scr[hh] = m
        p_all.append(jnp.exp2((s_t - m).astype(BF16)))
    for hh in range(H_FOX):
        acc_scr[hh] = _mm(vt_scr[hh, :, pl.ds(t0, tb)], p_all[hh])

    def scores(buf, kb, hh):
        off = pl.multiple_of(kb * tb, tb)
        buf[hh] = _mm(k_scr[hh, pl.ds(off, tb), :], qt_scr[hh])

    def softmax(buf, hh):
        s_t = buf[hh]
        m_old = m_scr[hh]
        m_new = jnp.maximum(m_old, jnp.max(s_t, axis=0, keepdims=True))
        m_scr[hh] = m_new
        return hh, jnp.exp2((s_t - m_new).astype(BF16)), jnp.exp2(m_old - m_new)

    def accumulate(kb, pending):
        hh, p_t, alpha = pending
        off = pl.multiple_of(kb * tb, tb)
        acc_scr[hh] = alpha * acc_scr[hh] + _mm(vt_scr[hh, :, pl.ds(off, tb)], p_t)

    def absorb(buf, kb, also=None):
        pending = None
        for hh in range(H_FOX):
            cur = softmax(buf, hh)
            if also is not None:
                also(hh)
            if pending is not None:
                accumulate(kb, pending)
            pending = cur
        accumulate(kb, pending)

    def kv_pair(i, carry):
        kb = 2 * i
        for hh in range(H_FOX):
            scores(sa_scr, kb, hh)
        absorb(sa_scr, kb, also=lambda hh: scores(sb_scr, kb + 1, hh))
        absorb(sb_scr, kb + 1)
        return carry

    lax.fori_loop(0, lax.shift_right_logical(j, 1), kv_pair, 0)

    @pl.when(jnp.bitwise_and(j, 1) == 1)
    def _last_odd_block():
        for hh in range(H_FOX):
            scores(sa_scr, j - 1, hh)
        absorb(sa_scr, j - 1)

    zfg = proj(OFF_FG, W_FOX)
    gate0 = gate(0)
    fo = jnp.concatenate(
        [acc_scr[hh, :DH_FOX, :] / acc_scr[hh, DH_FOX:DH_FOX + 1, :] for hh in range(H_FOX)], axis=0).T
    a_in = (fo * _silu(zfg)).astype(BF16)

    cos = cos_ref[...]
    sin = sin_ref[...]
    rq, rkt, rvs, states = [], [], [], []
    for hh in range(H_RET):
        cols = slice(hh * DK_RET, (hh + 1) * DK_RET)
        rq.append(_rope(zrq[:, cols], cos, sin).astype(BF16))
        rkt.append((_rope(zrk[:, cols], cos, sin) * (DK_RET ** -0.5)).T)
        rvs.append(rv[:, hh * DV_RET:(hh + 1) * DV_RET])
        states.append(st_ref[0, hh])
    inter = [_mm(rq[hh], states[hh].astype(BF16)) for hh in range(H_RET)]
    scores = [_mm(rq[hh], rkt[hh].astype(BF16)) for hh in range(H_RET)]
    zmq = proj(OFF_MQ, W_MEM)
    merged = gate0 * _mm(a_in, wpf_ref[...])
    ret_o = [inter[hh] * decq_scr[hh] + _mm((scores[hh] * dmask_scr[hh]).astype(BF16), rvs[hh])
             for hh in range(H_RET)]
    for hh in range(H_RET):
        st_ref[0, hh] = ssc_scr[hh] * states[hh] + _mm((rkt[hh] * deck_scr[hh]).astype(BF16), rvs[hh])
    gate1 = gate(1)
    zmg = proj(OFF_MG, W_MEM)

    mem_s = []
    for hh in range(H_MEM):
        cols = slice(hh * DH_MEM, (hh + 1) * DH_MEM)
        q = _rms_rows(zmq[:, cols], gmq_ref[...]).astype(BF16)
        mem_s.append(_mm(q, mkt_ref[0, cols, :]) * (DH_MEM ** -0.5))
    r_in = jnp.concatenate(
        [_group_norm(ret_o[hh]) * _silu(zrg[:, hh * DV_RET:(hh + 1) * DV_RET]) for hh in range(H_RET)],
        axis=1).astype(BF16)
    merged = merged + gate1 * _mm(r_in, wpr_ref[...])
    m_parts = []
    for hh in range(H_MEM):
        cols = slice(hh * DH_MEM, (hh + 1) * DH_MEM)
        p = jnp.exp(mem_s[hh] - jnp.max(mem_s[hh], axis=-1, keepdims=True))
        l = jnp.sum(p, axis=-1, keepdims=True)
        m_parts.append(_mm(p.astype(BF16), mvb_ref[0, :, cols]) / l)
    gate2 = gate(2)
    m_in = (jnp.concatenate(m_parts, axis=1) * _silu(zmg)).astype(BF16)
    merged = merged + gate2 * _mm(m_in, wpm_ref[...])

    y_ref[0] = x + _mm(merged.astype(BF16), wo_ref[...])


def _resident(shape):
    nd = len(shape)
    return pl.BlockSpec(shape, lambda b, j: (0,) * nd, pipeline_mode=pl.Buffered(1))


def _prompt_call(x, cos, sin, mkt, mvb, wvf_t, bf_col, log_g, g_norm, b_merge, gfq, gfk, gmq, seg, w_main, wpf, wpr,
                 wpm, wo, tb=PROMPT_BLOCK):
    B, T, _ = x.shape
    nt = T // tb
    kernel = functools.partial(_prompt_kernel, tb=tb)
    in_specs = [
        pl.BlockSpec(memory_space=pltpu.SMEM),
        pl.BlockSpec((1, tb, D_MODEL), lambda b, j: (b, j, 0)),
        pl.BlockSpec((tb, DK_RET), lambda b, j: (j, 0)),
        pl.BlockSpec((tb, DK_RET), lambda b, j: (j, 0)),
        pl.BlockSpec((1, W_MEM, N_MEM), lambda b, j: (b, 0, 0)),
        pl.BlockSpec((1, N_MEM, W_MEM), lambda b, j: (b, 0, 0)),
        _resident((1, D_MODEL)),
        _resident((F_ROWS, 1)),
        _resident((1, N_BRANCH * D_MODEL)),
        _resident((1, W_FOX)),
        _resident((1, W_FOX)),
        _resident((1, DH_MEM)),
        _resident((MXU_TILE, MXU_TILE)),
        _resident((D_MODEL, W_MAIN)),
        _resident((W_FOX, D_MODEL)),
        _resident((W_RET_V, D_MODEL)),
        _resident((W_MEM, D_MODEL)),
        _resident((D_MODEL, D_MODEL)),
        _resident((W_FOX + F_ROWS, D_MODEL)),
    ]
    out_specs = [
        pl.BlockSpec((1, tb, D_MODEL), lambda b, j: (b, j, 0)),
        pl.BlockSpec((1, W_FOX, tb), lambda b, j: (b, 0, j)),
        pl.BlockSpec((1, W_FOX, tb), lambda b, j: (b, 0, j)),
        pl.BlockSpec((1, H_FOX, tb), lambda b, j: (b, 0, j)),
        pl.BlockSpec((1, H_RET, DK_RET, DV_RET), lambda b, j: (b, 0, 0, 0)),
    ]
    out_shape = [
        jax.ShapeDtypeStruct((B, T, D_MODEL), F32),
        jax.ShapeDtypeStruct((B, W_FOX, T), F32),
        jax.ShapeDtypeStruct((B, W_FOX, T), F32),
        jax.ShapeDtypeStruct((B, H_FOX, T), F32),
        jax.ShapeDtypeStruct((B, H_RET, DK_RET, DV_RET), F32),
    ]
    scratch = [
        pltpu.VMEM((H_FOX, T, LANES), BF16),
        pltpu.VMEM((H_FOX, V_ROWS, T), BF16),
        pltpu.VMEM((H_FOX, LANES, tb), BF16),
        pltpu.VMEM((H_FOX, 1, tb), F32),
        pltpu.VMEM((H_FOX, V_ROWS, tb), F32),
        pltpu.VMEM((H_FOX, tb, tb), F32),
        pltpu.VMEM((H_FOX, tb, tb), F32),
        pltpu.VMEM((F_ROWS, 1), F32),
        pltpu.VMEM((H_RET, tb, tb), F32),
        pltpu.VMEM((H_RET, tb, 1), F32),
        pltpu.VMEM((H_RET, 1, tb), F32),
        pltpu.VMEM((H_RET, 1, DV_RET), F32),
    ]
    return pl.pallas_call(
        kernel,
        grid=(B, nt),
        in_specs=in_specs,
        out_specs=out_specs,
        out_shape=out_shape,
        scratch_shapes=scratch,
        compiler_params=pltpu.CompilerParams(
            dimension_semantics=("arbitrary", "arbitrary"), vmem_limit_bytes=VMEM_LIMIT_BYTES),
        name="prompt",
    )(log_g, x, cos, sin, mkt, mvb, g_norm, bf_col, b_merge, gfq, gfk, gmq, seg, w_main, wpf, wpr, wpm, wo, wvf_t)


def _sample_kernel(lg_ref, x_ref, cos_ref, sin_ref, ckt_ref, cvt_ref, clf_ref, st0_ref, cmk_ref, cmv_ref,
                   gn_ref, bf_ref, bm_ref, gfq_ref, gfk_ref, gmq_ref, seg_ref, w_ref, wpf_ref, wpr_ref, wpm_ref,
                   wo_ref, wft_ref,
                   y_ref, fk_ref, fv_ref, lft_ref, st_ref,
                   h_scr, fq_scr, knt_scr, vn_scr, ncn_scr, rq_scr, rkt_scr, rv_scr, mq_scr, fo_scr, ro_scr, mo_scr,
                   *, nb, ts, past):
    b = pl.program_id(0)
    n = nb * ts
    r0 = pl.multiple_of(b * ts, ts)

    @pl.when(b == 0)
    def _project_all():
        h = _rms_rows(x_ref[...], gn_ref[...]).astype(BF16)
        h_scr[...] = h
        fq_scr[...] = _head_norm64(_mm(h, w_ref[:, OFF_FQ:OFF_FQ + W_FOX]), gfq_ref[...], seg_ref) * (DH_FOX ** -0.5)
        fk = _head_norm64(_mm(h, w_ref[:, OFF_FK:OFF_FK + W_FOX]), gfk_ref[...], seg_ref)
        fv = _mm(h, w_ref[:, OFF_FV:OFF_FV + W_FOX])
        logf_t = _log_sigmoid(_mm_nt(wft_ref[...], h) + bf_ref[...])
        fk_ref[...] = fk
        fv_ref[...] = fv
        lft_ref[...] = logf_t[:H_FOX]
        knt_scr[...] = fk.T.astype(BF16)
        vn_scr[...] = fv.astype(BF16)
        ri = lax.broadcasted_iota(jnp.int32, (n, n), 0)
        ci = lax.broadcasted_iota(jnp.int32, (n, n), 1)
        same = _idiv(ri, ts) == _idiv(ci, ts)
        triu = jnp.where(jnp.logical_and(same, ri <= ci), 1.0, 0.0).astype(BF16)
        ncn_scr[...] = -_cumsum_lanes(triu, logf_t)
        cos = cos_ref[...]
        sin = sin_ref[...]
        zrq = _mm(h, w_ref[:, OFF_RQ:OFF_RQ + W_RET_QK])
        zrk = _mm(h, w_ref[:, OFF_RK:OFF_RK + W_RET_QK])
        for hh in range(H_RET):
            cols = slice(hh * DK_RET, (hh + 1) * DK_RET)
            rq_scr[:, cols] = _rope(zrq[:, cols], cos, sin)
            rkt_scr[cols, :] = (_rope(zrk[:, cols], cos, sin) * (DK_RET ** -0.5)).T
        rv_scr[...] = _mm(h, w_ref[:, OFF_RV:OFF_RV + W_RET_V]).astype(BF16)
        zmq = _mm(h, w_ref[:, OFF_MQ:OFF_MQ + W_MEM])
        for hh in range(H_MEM):
            cols = slice(hh * DH_MEM, (hh + 1) * DH_MEM)
            mq_scr[:, cols] = _rms_rows(zmq[:, cols], gmq_ref[...])

    nr = H_FOX * ts
    fq_b = fq_scr[pl.ds(r0, ts), :]
    row_head = _idiv(lax.broadcasted_iota(jnp.int32, (nr, W_FOX), 0), ts)
    col_head = _idiv(lax.broadcasted_iota(jnp.int32, (nr, W_FOX), 1), DH_FOX)
    head_sel = row_head == col_head
    q_bd = jnp.where(head_sel, jnp.concatenate([fq_b] * H_FOX, axis=0), 0.0).astype(BF16)

    cc = 256
    ri = lax.broadcasted_iota(jnp.int32, (cc, cc), 0)
    ci = lax.broadcasted_iota(jnp.int32, (cc, cc), 1)
    triu = jnp.where(ri <= ci, 1.0, 0.0).astype(BF16)
    carry = jnp.zeros((H_FOX, 1), F32)
    c_chunks = []
    for c in range(past // cc):
        cch = _cumsum_lanes(triu, clf_ref[0, :, c * cc:(c + 1) * cc]) + carry
        carry = cch[:, cc - 1:cc]
        c_chunks.append(cch)
    c_past = jnp.concatenate(c_chunks, axis=1)
    ncp = jnp.concatenate([jnp.broadcast_to(-c_past[hh:hh + 1, :], (ts, past)) for hh in range(H_FOX)], axis=0)
    ncn = ncn_scr[:H_FOX, :] - carry
    ncn = jnp.concatenate([jnp.broadcast_to(ncn[hh:hh + 1, :], (ts, n)) for hh in range(H_FOX)], axis=0)

    s_p = _mm(q_bd, ckt_ref[0].astype(BF16)) + ncp
    s_n = _mm(q_bd, knt_scr[...]) + ncn
    tok = lax.broadcasted_iota(jnp.int32, (nr, n), 1)
    qt = _imod(lax.broadcasted_iota(jnp.int32, (nr, n), 0), ts)
    valid = jnp.logical_and(_idiv(tok, ts) == b, _imod(tok, ts) <= qt)
    s_n = jnp.where(valid, s_n, -jnp.inf)
    m = jnp.maximum(jnp.max(s_p, axis=-1, keepdims=True), jnp.max(s_n, axis=-1, keepdims=True))
    p_p = jnp.exp(s_p - m)
    p_n = jnp.exp(s_n - m)
    l = jnp.sum(p_p, axis=-1, keepdims=True) + jnp.sum(p_n, axis=-1, keepdims=True)
    o_past = _mm(cvt_ref[0].astype(BF16), p_p.T.astype(BF16)).T
    o_bd = (o_past + _mm(p_n.astype(BF16), vn_scr[...])) / l
    o_bd = jnp.where(head_sel, o_bd, 0.0)
    fo = o_bd[0:ts, :]
    for hh in range(1, H_FOX):
        fo = fo + o_bd[hh * ts:(hh + 1) * ts, :]
    fo_scr[pl.ds(r0, ts), :] = fo

    tokr = lax.broadcasted_iota(jnp.int32, (ts, n), 1)
    qtr = lax.broadcasted_iota(jnp.int32, (ts, n), 0)
    in_b = _idiv(tokr, ts) == b
    dt = (qtr - _imod(tokr, ts)).astype(F32)
    tokc = lax.broadcasted_iota(jnp.int32, (1, n), 1)
    kpos = _imod(tokc, ts).astype(F32)
    rowi = lax.broadcasted_iota(jnp.int32, (ts, 1), 0).astype(F32)
    r_parts = []
    for hh in range(H_RET):
        lg = lg_ref[hh]
        q = rq_scr[pl.ds(r0, ts), hh * DK_RET:(hh + 1) * DK_RET].astype(BF16)
        kt = rkt_scr[hh * DK_RET:(hh + 1) * DK_RET, :]
        v = rv_scr[:, hh * DV_RET:(hh + 1) * DV_RET]
        state = st0_ref[0, hh]
        dmask = jnp.where(jnp.logical_and(in_b, dt >= 0.0), jnp.exp(lg * jnp.maximum(dt, 0.0)), 0.0)
        inter = _mm(q, state.astype(BF16)) * jnp.exp(lg * (rowi + 1.0))
        scores = _mm(q, kt.astype(BF16)) * dmask
        r_parts.append(inter + _mm(scores.astype(BF16), v))
        deck = jnp.where(_idiv(tokc, ts) == b,jnp.exp(lg * (ts - 1.0 - kpos)), 0.0)
        st_ref[0, hh] = jnp.exp(jnp.full((1, DV_RET), lg * ts, F32)) * state + _mm((kt * deck).astype(BF16), v)
    ro_scr[pl.ds(r0, ts), :] = jnp.concatenate(r_parts, axis=1)

    m_parts = []
    for hh in range(H_MEM):
        cols = slice(hh * DH_MEM, (hh + 1) * DH_MEM)
        q = mq_scr[pl.ds(r0, ts), cols].astype(BF16)
        s = _mm_nt(q, cmk_ref[0, :, cols].astype(BF16)) * (DH_MEM ** -0.5)
        p = jnp.exp(s - jnp.max(s, axis=-1, keepdims=True))
        l = jnp.sum(p, axis=-1, keepdims=True)
        m_parts.append(_mm(p.astype(BF16), cmv_ref[0, :, cols].astype(BF16)) / l)
    mo_scr[pl.ds(r0, ts), :] = jnp.concatenate(m_parts, axis=1)

    @pl.when(b == nb - 1)
    def _output_all():
        h = h_scr[...]
        a_in = (fo_scr[...] * _silu(_mm(h, w_ref[:, OFF_FG:OFF_FG + W_FOX]))).astype(BF16)
        merged = _gate(h, w_ref, bm_ref, 0) * _mm(a_in, wpf_ref[...])
        zrg = _mm(h, w_ref[:, OFF_RG:OFF_RG + W_RET_V])
        ro = ro_scr[...]
        r_parts = []
        for hh in range(H_RET):
            cols = slice(hh * DV_RET, (hh + 1) * DV_RET)
            r_parts.append(_group_norm(ro[:, cols]) * _silu(zrg[:, cols]))
        r_in = jnp.concatenate(r_parts, axis=1).astype(BF16)
        merged = merged + _gate(h, w_ref, bm_ref, 1) * _mm(r_in, wpr_ref[...])
        m_in = (mo_scr[...] * _silu(_mm(h, w_ref[:, OFF_MG:OFF_MG + W_MEM]))).astype(BF16)
        merged = merged + _gate(h, w_ref, bm_ref, 2) * _mm(m_in, wpm_ref[...])
        y_ref[...] = x_ref[...] + _mm(merged.astype(BF16), wo_ref[...])


def _sample_call(x2d, cos, sin, ckt, cvt, clf_t, st0, cmk, cmv, wf_t, bf_col, log_g, g_norm, b_merge, gfq, gfk, gmq, seg,
                 w_main, wpf, wpr, wpm, wo, nb, ts):
    n = nb * ts
    past = ckt.shape[2]
    kernel = functools.partial(_sample_kernel, nb=nb, ts=ts, past=past)

    def res(shape):
        nd = len(shape)
        return pl.BlockSpec(shape, lambda b: (0,) * nd, pipeline_mode=pl.Buffered(1))

    in_specs = [
        pl.BlockSpec(memory_space=pltpu.SMEM),
        res((n, D_MODEL)),
        res((n, DK_RET)),
        res((n, DK_RET)),
        pl.BlockSpec((1, W_FOX, past), lambda b: (b, 0, 0)),
        pl.BlockSpec((1, W_FOX, past), lambda b: (b, 0, 0)),
        pl.BlockSpec((1, H_FOX, past), lambda b: (b, 0, 0)),
        pl.BlockSpec((1, H_RET, DK_RET, DV_RET), lambda b: (b, 0, 0, 0)),
        pl.BlockSpec((1, N_MEM, W_MEM), lambda b: (b, 0, 0)),
        pl.BlockSpec((1, N_MEM, W_MEM), lambda b: (b, 0, 0)),
        res((1, D_MODEL)),
        res((F_ROWS, 1)),
        res((1, N_BRANCH * D_MODEL)),
        res((1, W_FOX)),
        res((1, W_FOX)),
        res((1, DH_MEM)),
        res((MXU_TILE, MXU_TILE)),
        res((D_MODEL, W_MAIN)),
        res((W_FOX, D_MODEL)),
        res((W_RET_V, D_MODEL)),
        res((W_MEM, D_MODEL)),
        res((D_MODEL, D_MODEL)),
        res((F_ROWS, D_MODEL)),
    ]
    full = lambda shape: pl.BlockSpec(shape, lambda b: (0,) * len(shape))
    out_specs = [
        full((n, D_MODEL)),
        full((n, W_FOX)),
        full((n, W_FOX)),
        full((H_FOX, n)),
        pl.BlockSpec((1, H_RET, DK_RET, DV_RET), lambda b: (b, 0, 0, 0)),
    ]
    out_shape = [
        jax.ShapeDtypeStruct((n, D_MODEL), F32),
        jax.ShapeDtypeStruct((n, W_FOX), F32),
        jax.ShapeDtypeStruct((n, W_FOX), F32),
        jax.ShapeDtypeStruct((H_FOX, n), F32),
        jax.ShapeDtypeStruct((nb, H_RET, DK_RET, DV_RET), F32),
    ]
    scratch = [
        pltpu.VMEM((n, D_MODEL), BF16),
        pltpu.VMEM((n, W_FOX), F32),
        pltpu.VMEM((W_FOX, n), BF16),
        pltpu.VMEM((n, W_FOX), BF16),
        pltpu.VMEM((F_ROWS, n), F32),
        pltpu.VMEM((n, W_RET_QK), F32),
        pltpu.VMEM((W_RET_QK, n), F32),
        pltpu.VMEM((n, W_RET_V), BF16),
        pltpu.VMEM((n, W_MEM), F32),
        pltpu.VMEM((n, W_FOX), F32),
        pltpu.VMEM((n, W_RET_V), F32),
        pltpu.VMEM((n, W_MEM), F32),
    ]
    return pl.pallas_call(
        kernel,
        grid=(nb,),
        in_specs=in_specs,
        out_specs=out_specs,
        out_shape=out_shape,
        scratch_shapes=scratch,
        compiler_params=pltpu.CompilerParams(
            dimension_semantics=("arbitrary",), vmem_limit_bytes=VMEM_LIMIT_BYTES),
        name="sample",
    )(log_g, x2d, cos, sin, ckt, cvt, clf_t, st0, cmk, cmv, g_norm, bf_col, b_merge, gfq, gfk, gmq, seg,
      w_main, wpf, wpr, wpm, wo, wf_t)


def _rope_tables(pos):
    half = DK_RET // 2
    inv = ROPE_BASE ** (-jnp.arange(half, dtype=F32) / half)
    ang = pos.astype(F32)[:, None] * inv[None, :]
    cos = jnp.cos(ang)
    sin = jnp.sin(ang)
    return jnp.concatenate([cos, cos], axis=1), jnp.concatenate([-sin, sin], axis=1)


def _reorder_w_in(w_t):
    assert (3 * W_FOX) % REORDER_ROWS == 0 and W_MAIN % REORDER_ROWS == 0 and REORDER_ROWS % H_FOX == 0
    assert w_t.shape == (W_MAIN + H_FOX, D_MODEL)
    return pl.pallas_call(
        _reorder_kernel,
        grid=(W_MAIN // REORDER_ROWS,),
        in_specs=[
            pl.BlockSpec((REORDER_ROWS, D_MODEL), lambda c: (c, 0)),
            pl.BlockSpec((H_FOX, D_MODEL), lambda c: ((c + 1) * (REORDER_ROWS // H_FOX), 0)),
        ],
        out_specs=pl.BlockSpec((D_MODEL, REORDER_ROWS), lambda c: (0, c)),
        out_shape=jax.ShapeDtypeStruct((D_MODEL, W_MAIN), BF16),
        compiler_params=pltpu.CompilerParams(dimension_semantics=("arbitrary",)),
        name="reorder_w_in",
    )(w_t, w_t)


def _reorder_kernel(a_ref, b_ref, o_ref):
    c = pl.program_id(0)
    a = a_ref[...]
    shifted = jnp.concatenate([a[H_FOX:], b_ref[...]], axis=0)
    src = jnp.where(c < (3 * W_FOX) // REORDER_ROWS, a, shifted)
    o_ref[...] = src.T.astype(BF16)


def kernel(x_prompt, x_sample, mem_prompt, cache_fox_k, cache_fox_v, cache_fox_logf, state_ret, cache_mem_k, cache_mem_v,
           g_norm, g_mem_norm, w_in, b_f, b_merge, g_fox_q, g_fox_k, g_mem_q, g_mem_k, w_mem_kv,
           w_p_fox, w_p_ret, w_p_mem, w_out):
    depth = w_in.shape[0]
    assert depth == 1, "single-layer kernel"
    B, T, _ = x_prompt.shape
    nb, ts, _ = x_sample.shape
    past = cache_fox_k.shape[2]

    log_g = jnp.log1p(-jnp.exp2(-5.0 - jnp.arange(H_RET, dtype=F32)))
    head_of_lane = jnp.arange(MXU_TILE) // DH_FOX
    seg = (head_of_lane[:, None] == head_of_lane[None, :]).astype(BF16)
    w_t = jnp.swapaxes(w_in[0], 0, 1)
    w_main = _reorder_w_in(w_t)
    gn = g_norm[0].reshape(1, D_MODEL)
    bm = b_merge[0].reshape(1, N_BRANCH * D_MODEL)
    gfq = jnp.tile(g_fox_q[0], H_FOX).reshape(1, W_FOX)
    gfk = jnp.tile(g_fox_k[0], H_FOX).reshape(1, W_FOX)
    gmq = g_mem_q[0].reshape(1, DH_MEM)
    wpf = w_p_fox[0].astype(BF16)
    wpr = w_p_ret[0].astype(BF16)
    wpm = w_p_mem[0].astype(BF16)
    wo = w_out[0].astype(BF16)
    shared = (gn, bm, gfq, gfk, gmq, seg, w_main, wpf, wpr, wpm, wo)

    mk, mv, mkt, mvb = _memkv_call(mem_prompt, g_mem_norm[0], w_mem_kv[0], g_mem_k[0])

    cos_p, sin_p = _rope_tables(jnp.arange(T, dtype=jnp.int32))
    wvf_t = jnp.concatenate(
        [w_t[2 * W_FOX:3 * W_FOX + H_FOX], jnp.zeros((F_ROWS - H_FOX, D_MODEL), F32)], axis=0).astype(BF16)
    bf_col = jnp.concatenate([b_f[0], jnp.zeros((F_ROWS - H_FOX,), F32)]).reshape(F_ROWS, 1)
    y_p, fkt_p, fvt_p, lft_p, st_p = _prompt_call(
        x_prompt, cos_p, sin_p, mkt, mvb, wvf_t, bf_col, log_g, *shared)

    def time_major(a_t):
        return jnp.transpose(a_t.reshape(a_t.shape[0], H_FOX, DH_FOX, a_t.shape[2]), (0, 3, 1, 2))[None]

    def time_minor(a):
        return jnp.transpose(a, (0, 2, 3, 1)).reshape(a.shape[0], W_FOX, a.shape[1])

    fk_p = time_major(fkt_p)
    fv_p = time_major(fvt_p)
    lf_p = jnp.swapaxes(lft_p, 1, 2)[None]

    pos_s = past + jnp.arange(ts, dtype=jnp.int32)
    cos_s, sin_s = _rope_tables(jnp.tile(pos_s, nb))
    y_s, fk_s, fv_s, lft_s, st_s = _sample_call(
        x_sample.reshape(nb * ts, D_MODEL), cos_s, sin_s,
        time_minor(cache_fox_k[0]), time_minor(cache_fox_v[0]),
        jnp.swapaxes(cache_fox_logf[0], 1, 2), state_ret[0],
        cache_mem_k[0].reshape(nb, N_MEM, W_MEM), cache_mem_v[0].reshape(nb, N_MEM, W_MEM),
        wvf_t[W_FOX:], bf_col, log_g, *shared, nb=nb, ts=ts)
    lf_s = jnp.transpose(lft_s.reshape(H_FOX, nb, ts), (1, 2, 0))[None]

    return (y_p, y_s.reshape(nb, ts, D_MODEL),
            fk_p, fv_p, lf_p,
            st_p[None], mk.reshape(1, B, N_MEM, H_MEM, DH_MEM), mv.reshape(1, B, N_MEM, H_MEM, DH_MEM),
            fk_s.reshape(1, nb, ts, H_FOX, DH_FOX), fv_s.reshape(1, nb, ts, H_FOX, DH_FOX),
            lf_s, st_s[None])
```

```python
import functools

import jax
import jax.numpy as jnp
from jax import lax
from jax.experimental import pallas as pl
from jax.experimental.pallas import tpu as pltpu

D_MODEL = 1024
N_MEM = 256
H_FOX = 8
DH_FOX = 64
H_RET = 4
DK_RET = 128
DV_RET = 256
H_MEM = 4
DH_MEM = 128
W_FOX = H_FOX * DH_FOX
W_RET_QK = H_RET * DK_RET
W_RET_V = H_RET * DV_RET
W_MEM = H_MEM * DH_MEM
N_BRANCH = 3
ROPE_BASE = 10000.0
EPS = 1e-6
LOG2E = 1.4426950408889634

LANES = 128
MXU_TILE = 256
BF16_SUBLANES = 16
V_ROWS = DH_FOX + BF16_SUBLANES
F_ROWS = BF16_SUBLANES

OFF_FQ = 0
OFF_FK = OFF_FQ + W_FOX
OFF_FV = OFF_FK + W_FOX
OFF_FG = OFF_FV + W_FOX
OFF_RQ = OFF_FG + W_FOX
OFF_RK = OFF_RQ + W_RET_QK
OFF_RV = OFF_RK + W_RET_QK
OFF_RG = OFF_RV + W_RET_V
OFF_MQ = OFF_RG + W_RET_V
OFF_MG = OFF_MQ + W_MEM
OFF_GL = OFF_MG + W_MEM
W_MAIN = OFF_GL + N_BRANCH * D_MODEL
REORDER_ROWS = 512

PROMPT_BLOCK = 256
VMEM_LIMIT_BYTES = 60 * 1024 * 1024

F32 = jnp.float32
BF16 = jnp.bfloat16


def _mm(a, b):
    return jnp.dot(a, b, preferred_element_type=F32)


def _mm_nt(a, b):
    return lax.dot_general(a, b, (((1,), (1,)), ((), ())), preferred_element_type=F32)


def _idiv(x, d):
    assert d & (d - 1) == 0
    return lax.shift_right_logical(x, d.bit_length() - 1)


def _imod(x, d):
    assert d & (d - 1) == 0
    return jnp.bitwise_and(x, d - 1)


def _rms_rows(x, g):
    ms = jnp.mean(x * x, axis=-1, keepdims=True)
    return x * lax.rsqrt(ms + EPS) * g


def _head_sumsq(z, seg_ref):
    sq = (z * z).astype(BF16)
    return jnp.concatenate(
        [_mm(sq[:, c:c + MXU_TILE], seg_ref[...]) for c in range(0, W_FOX, MXU_TILE)], axis=1)


def _head_norm64(z, g, seg_ref):
    return z * lax.rsqrt(_head_sumsq(z, seg_ref) * (1.0 / DH_FOX) + EPS) * g


def _log_sigmoid(u):
    return jnp.minimum(u, 0.0) - jnp.log1p(jnp.exp(-jnp.abs(u)))


def _silu(u):
    return u * jax.nn.sigmoid(u)


def _split3(a):
    hi = a.astype(BF16)
    r1 = a - hi.astype(F32)
    mid = r1.astype(BF16)
    lo = (r1 - mid.astype(F32)).astype(BF16)
    return hi, mid, lo


def _tri_cumsum(tri, a):
    hi, mid, lo = _split3(a)
    return _mm(tri, hi) + _mm(tri, mid) + _mm(tri, lo)


def _cumsum_lanes(triu, a):
    hi, mid, lo = _split3(a)
    return _mm(hi, triu) + _mm(mid, triu) + _mm(lo, triu)


def _rope(x, cos, sin_signed):
    return x * cos + pltpu.roll(x, DK_RET // 2, 1) * sin_signed


def _group_norm(o):
    mu = jnp.mean(o, axis=-1, keepdims=True)
    d = o - mu
    var = jnp.mean(d * d, axis=-1, keepdims=True)
    return d * lax.rsqrt(var + EPS)


def _gate(h, w_ref, bm_ref, i):
    z = _mm(h, w_ref[:, OFF_GL + i * D_MODEL:OFF_GL + (i + 1) * D_MODEL])
    return jax.nn.sigmoid(z + bm_ref[:, i * D_MODEL:(i + 1) * D_MODEL])


def _memkv_kernel(mem_ref, gmn_ref, w_ref, gmk_ref, mk_ref, mv_ref, mkt_ref, mvb_ref):
    h = _rms_rows(mem_ref[0], gmn_ref[...]).astype(BF16)
    z = _mm(h, w_ref[...])
    ks = []
    for hh in range(H_MEM):
        ks.append(_rms_rows(z[:, hh * DH_MEM:(hh + 1) * DH_MEM], gmk_ref[...]))
    mk = jnp.concatenate(ks, axis=1)
    mv = z[:, W_MEM:]
    for hh in range(H_MEM):
        rows = pl.ds(hh, N_MEM, stride=H_MEM)
        mk_ref[0, rows, :] = ks[hh]
        mv_ref[0, rows, :] = mv[:, hh * DH_MEM:(hh + 1) * DH_MEM]
    mkt_ref[0] = mk.T.astype(BF16)
    mvb_ref[0] = mv.astype(BF16)


def _memkv_call(mem, g_mem_norm, w_mem_kv, g_mem_k):
    B = mem.shape[0]
    const = lambda b: (0, 0)
    per_b = lambda b: (b, 0, 0)
    return pl.pallas_call(
        _memkv_kernel,
        grid=(B,),
        in_specs=[
            pl.BlockSpec((1, N_MEM, D_MODEL), per_b),
            pl.BlockSpec((1, D_MODEL), const),
            pl.BlockSpec((D_MODEL, 2 * W_MEM), const),
            pl.BlockSpec((1, DH_MEM), const),
        ],
        out_specs=[
            pl.BlockSpec((1, N_MEM * H_MEM, DH_MEM), per_b),
            pl.BlockSpec((1, N_MEM * H_MEM, DH_MEM), per_b),
            pl.BlockSpec((1, W_MEM, N_MEM), per_b),
            pl.BlockSpec((1, N_MEM, W_MEM), per_b),
        ],
        out_shape=[
            jax.ShapeDtypeStruct((B, N_MEM * H_MEM, DH_MEM), F32),
            jax.ShapeDtypeStruct((B, N_MEM * H_MEM, DH_MEM), F32),
            jax.ShapeDtypeStruct((B, W_MEM, N_MEM), BF16),
            jax.ShapeDtypeStruct((B, N_MEM, W_MEM), BF16),
        ],
        compiler_params=pltpu.CompilerParams(dimension_semantics=("arbitrary",)),
        name="memkv",
    )(mem, g_mem_norm.reshape(1, D_MODEL), w_mem_kv.astype(BF16), g_mem_k.reshape(1, DH_MEM))


def _prompt_kernel(lg_ref, x_ref, cos_ref, sin_ref, mkt_ref, mvb_ref, gn_ref, bf_ref, bm_ref, gfq_ref, gfk_ref,
                   gmq_ref, seg_ref, w_ref, wpf_ref, wpr_ref, wpm_ref, wo_ref, wvt_ref,
                   y_ref, fkt_ref, fvt_ref, lft_ref, st_ref,
                   k_scr, vt_scr, qt_scr, m_scr, acc_scr, sa_scr, sb_scr, ccar_scr, dmask_scr, decq_scr, deck_scr,
                   ssc_scr, *, tb):
    b = pl.program_id(0)
    j = pl.program_id(1)
    t0 = pl.multiple_of(j * tb, tb)

    @pl.when(jnp.logical_and(b == 0, j == 0))
    def _init_tables():
        ri = lax.broadcasted_iota(jnp.int32, (tb, tb), 0)
        ci = lax.broadcasted_iota(jnp.int32, (tb, tb), 1)
        diff = (ri - ci).astype(F32)
        rowi = lax.broadcasted_iota(jnp.int32, (tb, 1), 0).astype(F32)
        coli = lax.broadcasted_iota(jnp.int32, (1, tb), 1).astype(F32)
        for hh in range(H_RET):
            lg = lg_ref[hh]
            dmask_scr[hh] = jnp.where(diff >= 0.0, jnp.exp(lg * jnp.maximum(diff, 0.0)), 0.0)
            decq_scr[hh] = jnp.exp(lg * (rowi + 1.0))
            deck_scr[hh] = jnp.exp(lg * (tb - 1.0 - coli))
            ssc_scr[hh] = jnp.exp(jnp.full((1, DV_RET), lg * tb, F32))

    @pl.when(j == 0)
    def _init_batch():
        ccar_scr[...] = jnp.zeros_like(ccar_scr)
        st_ref[...] = jnp.zeros_like(st_ref)

    h = _rms_rows(x_ref[0], gn_ref[...]).astype(BF16)

    def proj(off, width):
        return _mm(h, w_ref[:, off:off + width])

    def gate(i):
        return jax.nn.sigmoid(proj(OFF_GL + i * D_MODEL, D_MODEL) + bm_ref[:, i * D_MODEL:(i + 1) * D_MODEL])


    zfq = proj(OFF_FQ, W_FOX)
    zfk = proj(OFF_FK, W_FOX)
    vz_t = _mm_nt(wvt_ref[...], h)
    fv_t = vz_t[:W_FOX]
    zrq = proj(OFF_RQ, W_RET_QK)
    ssq = _head_sumsq(zfq, seg_ref)
    ssk = _head_sumsq(zfk, seg_ref)
    zrk = proj(OFF_RK, W_RET_QK)
    logf_t = _log_sigmoid(vz_t[W_FOX:] + bf_ref[...])
    ri = lax.broadcasted_iota(jnp.int32, (tb, tb), 0)
    ci = lax.broadcasted_iota(jnp.int32, (tb, tb), 1)
    key_le_query = ri <= ci
    c_t = _cumsum_lanes(jnp.where(key_le_query, 1.0, 0.0).astype(BF16), logf_t) + ccar_scr[...]
    ccar_scr[...] = c_t[:, tb - 1:tb]
    rv = proj(OFF_RV, W_RET_V).astype(BF16)
    fq = zfq * lax.rsqrt(ssq * (1.0 / DH_FOX) + EPS) * gfq_ref[...] * (DH_FOX ** -0.5 * LOG2E)
    fk = zfk * lax.rsqrt(ssk * (1.0 / DH_FOX) + EPS) * gfk_ref[...]
    fkt_ref[0] = fk.T
    fvt_ref[0] = fv_t
    lft_ref[0] = logf_t[:H_FOX]
    hi, mid, lo = _split3(c_t[:H_FOX] * (-LOG2E))
    parts_t = jnp.concatenate(
        [hi.astype(F32), mid.astype(F32), lo.astype(F32), jnp.zeros((LANES - 3 * H_FOX, tb), F32)], axis=0)
    bias = pltpu.roll(parts_t.T, DH_FOX, 1)
    lane = lax.broadcasted_iota(jnp.int32, (tb, LANES), 1)
    fq_t = fq.T
    vrow = lax.broadcasted_iota(jnp.int32, (V_ROWS - DH_FOX, tb), 0)
    ones_row = jnp.where(vrow == 0, 1.0, 0.0).astype(BF16)
    fv_tb = fv_t.astype(BF16)
    for hh in range(H_FOX):
        vt_scr[hh, :, pl.ds(t0, tb)] = jnp.concatenate([fv_tb[hh * DH_FOX:(hh + 1) * DH_FOX, :], ones_row], axis=0)
    qrow = lax.broadcasted_iota(jnp.int32, (LANES - DH_FOX, tb), 0)
    k_blk = []
    qt_blk = []
    for hh in range(H_FOX):
        pair = fk[:, (hh // 2) * LANES:(hh // 2 + 1) * LANES]
        if hh % 2:
            pair = pltpu.roll(pair, DH_FOX, 1)
        k_aug = jnp.where(lane < DH_FOX, pair, bias).astype(BF16)
        k_scr[hh, pl.ds(t0, tb), :] = k_aug
        k_blk.append(k_aug)
        ones = jnp.where(
            jnp.logical_or(qrow == hh, jnp.logical_or(qrow == H_FOX + hh, qrow == 2 * H_FOX + hh)), 1.0, 0.0)
        qt_aug = jnp.concatenate([fq_t[hh * DH_FOX:(hh + 1) * DH_FOX, :], ones], axis=0).astype(BF16)
        qt_scr[hh] = qt_aug
        qt_blk.append(qt_aug)

    s_all = [_mm(k_blk[hh], qt_blk[hh]) for hh in range(H_FOX)]
    zrg = proj(OFF_RG, W_RET_V)
    p_all = []
    for hh in range(H_FOX):
        s_t = jnp.where(key_le_query, s_all[hh], -jnp.inf)
        m = jnp.max(s_t, axis=0, keepdims=True)
        m_scr[hh] = m
        p_all.append(jnp.exp2((s_t - m).astype(BF16)))
    for hh in range(H_FOX):
        acc_scr[hh] = _mm(vt_scr[hh, :, pl.ds(t0, tb)], p_all[hh])

    def scores(buf, kb, hh):
        off = pl.multiple_of(kb * tb, tb)
        buf[hh] = _mm(k_scr[hh, pl.ds(off, tb), :], qt_scr[hh])

    def softmax(buf, hh):
        s_t = buf[hh]
        m_old = m_scr[hh]
        m_new = jnp.maximum(m_old, jnp.max(s_t, axis=0, keepdims=True))
        m_scr[hh] = m_new
        return hh, jnp.exp2((s_t - m_new).astype(BF16)), jnp.exp2(m_old - m_new)

    def accumulate(kb, pending):
        hh, p_t, alpha = pending
        off = pl.multiple_of(kb * tb, tb)
        acc_scr[hh] = alpha * acc_scr[hh] + _mm(vt_scr[hh, :, pl.ds(off, tb)], p_t)

    def absorb(buf, kb, also=None):
        pending = None
        for hh in range(H_FOX):
            cur = softmax(buf, hh)
            if also is not None:
                also(hh)
            if pending is not None:
                accumulate(kb, pending)
            pending = cur
        accumulate(kb, pending)

    def run_blocks(kb, count):
        bufs = (sa_scr, sb_scr)
        for hh in range(H_FOX):
            scores(bufs[0], kb, hh)
        for c in range(count):
            nxt = None
            if c + 1 < count:
                nxt = functools.partial(scores, bufs[(c + 1) % 2], kb + c + 1)
            absorb(bufs[c % 2], kb + c, also=nxt)

    def kv_quad(i, carry):
        run_blocks(4 * i, 4)
        return carry

    n_quads = lax.shift_right_logical(j, 2)
    lax.fori_loop(0, n_quads, kv_quad, 0)

    @pl.when(jnp.bitwise_and(j, 2) == 2)
    def _pair():
        run_blocks(4 * n_quads, 2)

    @pl.when(jnp.bitwise_and(j, 1) == 1)
    def _single():
        run_blocks(j - 1, 1)

    zfg = proj(OFF_FG, W_FOX)
    gate0 = gate(0)
    fo = jnp.concatenate(
        [acc_scr[hh, :DH_FOX, :] / acc_scr[hh, DH_FOX:DH_FOX + 1, :] for hh in range(H_FOX)], axis=0).T
    a_in = (fo * _silu(zfg)).astype(BF16)

    cos = cos_ref[...]
    sin = sin_ref[...]
    rq, rkt, rvs, states = [], [], [], []
    for hh in range(H_RET):
        cols = slice(hh * DK_RET, (hh + 1) * DK_RET)
        rq.append(_rope(zrq[:, cols], cos, sin).astype(BF16))
        rkt.append((_rope(zrk[:, cols], cos, sin) * (DK_RET ** -0.5)).T)
        rvs.append(rv[:, hh * DV_RET:(hh + 1) * DV_RET])
        states.append(st_ref[0, hh])
    inter = [_mm(rq[hh], states[hh].astype(BF16)) for hh in range(H_RET)]
    scores = [_mm(rq[hh], rkt[hh].astype(BF16)) for hh in range(H_RET)]
    zmq = proj(OFF_MQ, W_MEM)
    merged = gate0 * _mm(a_in, wpf_ref[...])
    ret_o = [inter[hh] * decq_scr[hh] + _mm((scores[hh] * dmask_scr[hh]).astype(BF16), rvs[hh])
             for hh in range(H_RET)]
    for hh in range(H_RET):
        st_ref[0, hh] = ssc_scr[hh] * states[hh] + _mm((rkt[hh] * deck_scr[hh]).astype(BF16), rvs[hh])
    gate1 = gate(1)
    zmg = proj(OFF_MG, W_MEM)

    mem_s = []
    for hh in range(H_MEM):
        cols = slice(hh * DH_MEM, (hh + 1) * DH_MEM)
        q = _rms_rows(zmq[:, cols], gmq_ref[...]).astype(BF16)
        mem_s.append(_mm(q, mkt_ref[0, cols, :]) * (DH_MEM ** -0.5))
    r_in = jnp.concatenate(
        [_group_norm(ret_o[hh]) * _silu(zrg[:, hh * DV_RET:(hh + 1) * DV_RET]) for hh in range(H_RET)],
        axis=1).astype(BF16)
    merged = merged + gate1 * _mm(r_in, wpr_ref[...])
    m_parts = []
    for hh in range(H_MEM):
        cols = slice(hh * DH_MEM, (hh + 1) * DH_MEM)
        p = jnp.exp(mem_s[hh] - jnp.max(mem_s[hh], axis=-1, keepdims=True))
        l = jnp.sum(p, axis=-1, keepdims=True)
        m_parts.append(_mm(p.astype(BF16), mvb_ref[0, :, cols]) / l)
    gate2 = gate(2)
    m_in = (jnp.concatenate(m_parts, axis=1) * _silu(zmg)).astype(BF16)
    merged = merged + gate2 * _mm(m_in, wpm_ref[...])

    y_ref[0] = x_ref[0] + _mm(merged.astype(BF16), wo_ref[...])


def _resident(shape):
    nd = len(shape)
    return pl.BlockSpec(shape, lambda b, j: (0,) * nd, pipeline_mode=pl.Buffered(1))


def _prompt_call(x, cos, sin, mkt, mvb, wvf_t, bf_col, log_g, g_norm, b_merge, gfq, gfk, gmq, seg, w_main, wpf, wpr,
                 wpm, wo, tb=PROMPT_BLOCK):
    B, T, _ = x.shape
    nt = T // tb
    kernel = functools.partial(_prompt_kernel, tb=tb)
    in_specs = [
        pl.BlockSpec(memory_space=pltpu.SMEM),
        pl.BlockSpec((1, tb, D_MODEL), lambda b, j: (b, j, 0)),
        pl.BlockSpec((tb, DK_RET), lambda b, j: (j, 0)),
        pl.BlockSpec((tb, DK_RET), lambda b, j: (j, 0)),
        pl.BlockSpec((1, W_MEM, N_MEM), lambda b, j: (b, 0, 0)),
        pl.BlockSpec((1, N_MEM, W_MEM), lambda b, j: (b, 0, 0)),
        _resident((1, D_MODEL)),
        _resident((F_ROWS, 1)),
        _resident((1, N_BRANCH * D_MODEL)),
        _resident((1, W_FOX)),
        _resident((1, W_FOX)),
        _resident((1, DH_MEM)),
        _resident((MXU_TILE, MXU_TILE)),
        _resident((D_MODEL, W_MAIN)),
        _resident((W_FOX, D_MODEL)),
        _resident((W_RET_V, D_MODEL)),
        _resident((W_MEM, D_MODEL)),
        _resident((D_MODEL, D_MODEL)),
        _resident((W_FOX + F_ROWS, D_MODEL)),
    ]
    out_specs = [
        pl.BlockSpec((1, tb, D_MODEL), lambda b, j: (b, j, 0)),
        pl.BlockSpec((1, W_FOX, tb), lambda b, j: (b, 0, j)),
        pl.BlockSpec((1, W_FOX, tb), lambda b, j: (b, 0, j)),
        pl.BlockSpec((1, H_FOX, tb), lambda b, j: (b, 0, j)),
        pl.BlockSpec((1, H_RET, DK_RET, DV_RET), lambda b, j: (b, 0, 0, 0)),
    ]
    out_shape = [
        jax.ShapeDtypeStruct((B, T, D_MODEL), F32),
        jax.ShapeDtypeStruct((B, W_FOX, T), F32),
        jax.ShapeDtypeStruct((B, W_FOX, T), F32),
        jax.ShapeDtypeStruct((B, H_FOX, T), F32),
        jax.ShapeDtypeStruct((B, H_RET, DK_RET, DV_RET), F32),
    ]
    scratch = [
        pltpu.VMEM((H_FOX, T, LANES), BF16),
        pltpu.VMEM((H_FOX, V_ROWS, T), BF16),
        pltpu.VMEM((H_FOX, LANES, tb), BF16),
        pltpu.VMEM((H_FOX, 1, tb), F32),
        pltpu.VMEM((H_FOX, V_ROWS, tb), F32),
        pltpu.VMEM((H_FOX, tb, tb), F32),
        pltpu.VMEM((H_FOX, tb, tb), F32),
        pltpu.VMEM((F_ROWS, 1), F32),
        pltpu.VMEM((H_RET, tb, tb), F32),
        pltpu.VMEM((H_RET, tb, 1), F32),
        pltpu.VMEM((H_RET, 1, tb), F32),
        pltpu.VMEM((H_RET, 1, DV_RET), F32),
    ]
    return pl.pallas_call(
        kernel,
        grid=(B, nt),
        in_specs=in_specs,
        out_specs=out_specs,
        out_shape=out_shape,
        scratch_shapes=scratch,
        compiler_params=pltpu.CompilerParams(
            dimension_semantics=("arbitrary", "arbitrary"), vmem_limit_bytes=VMEM_LIMIT_BYTES),
        name="prompt",
    )(log_g, x, cos, sin, mkt, mvb, g_norm, bf_col, b_merge, gfq, gfk, gmq, seg, w_main, wpf, wpr, wpm, wo, wvf_t)


def _sample_kernel(lg_ref, x_ref, cos_ref, sin_ref, ckt_ref, cvt_ref, clf_ref, st0_ref, cmk_ref, cmv_ref,
                   gn_ref, bf_ref, bm_ref, gfq_ref, gfk_ref, gmq_ref, seg_ref, w_ref, wpf_ref, wpr_ref, wpm_ref,
                   wo_ref, wft_ref,
                   y_ref, fk_ref, fv_ref, lft_ref, st_ref,
                   h_scr, fq_scr, knt_scr, vn_scr, ncn_scr, rq_scr, rkt_scr, rv_scr, mq_scr, fo_scr, ro_scr, mo_scr,
                   *, nb, ts, past):
    b = pl.program_id(0)
    n = nb * ts
    r0 = pl.multiple_of(b * ts, ts)

    @pl.when(b == 0)
    def _project_all():
        h = _rms_rows(x_ref[...], gn_ref[...]).astype(BF16)
        h_scr[...] = h
        fq_scr[...] = _head_norm64(_mm(h, w_ref[:, OFF_FQ:OFF_FQ + W_FOX]), gfq_ref[...], seg_ref) * (DH_FOX ** -0.5)
        fk = _head_norm64(_mm(h, w_ref[:, OFF_FK:OFF_FK + W_FOX]), gfk_ref[...], seg_ref)
        fv = _mm(h, w_ref[:, OFF_FV:OFF_FV + W_FOX])
        logf_t = _log_sigmoid(_mm_nt(wft_ref[...], h) + bf_ref[...])
        fk_ref[...] = fk
        fv_ref[...] = fv
        lft_ref[...] = logf_t[:H_FOX]
        knt_scr[...] = fk.T.astype(BF16)
        vn_scr[...] = fv.astype(BF16)
        ri = lax.broadcasted_iota(jnp.int32, (n, n), 0)
        ci = lax.broadcasted_iota(jnp.int32, (n, n), 1)
        same = _idiv(ri, ts) == _idiv(ci, ts)
        triu = jnp.where(jnp.logical_and(same, ri <= ci), 1.0, 0.0).astype(BF16)
        ncn_scr[...] = -_cumsum_lanes(triu, logf_t)
        cos = cos_ref[...]
        sin = sin_ref[...]
        zrq = _mm(h, w_ref[:, OFF_RQ:OFF_RQ + W_RET_QK])
        zrk = _mm(h, w_ref[:, OFF_RK:OFF_RK + W_RET_QK])
        for hh in range(H_RET):
            cols = slice(hh * DK_RET, (hh + 1) * DK_RET)
            rq_scr[:, cols] = _rope(zrq[:, cols], cos, sin)
            rkt_scr[cols, :] = (_rope(zrk[:, cols], cos, sin) * (DK_RET ** -0.5)).T
        rv_scr[...] = _mm(h, w_ref[:, OFF_RV:OFF_RV + W_RET_V]).astype(BF16)
        zmq = _mm(h, w_ref[:, OFF_MQ:OFF_MQ + W_MEM])
        for hh in range(H_MEM):
            cols = slice(hh * DH_MEM, (hh + 1) * DH_MEM)
            mq_scr[:, cols] = _rms_rows(zmq[:, cols], gmq_ref[...])

    nr = H_FOX * ts
    fq_b = fq_scr[pl.ds(r0, ts), :]
    row_head = _idiv(lax.broadcasted_iota(jnp.int32, (nr, W_FOX), 0), ts)
    col_head = _idiv(lax.broadcasted_iota(jnp.int32, (nr, W_FOX), 1), DH_FOX)
    head_sel = row_head == col_head
    q_bd = jnp.where(head_sel, jnp.concatenate([fq_b] * H_FOX, axis=0), 0.0).astype(BF16)

    cc = 256
    ri = lax.broadcasted_iota(jnp.int32, (cc, cc), 0)
    ci = lax.broadcasted_iota(jnp.int32, (cc, cc), 1)
    triu = jnp.where(ri <= ci, 1.0, 0.0).astype(BF16)
    carry = jnp.zeros((H_FOX, 1), F32)
    c_chunks = []
    for c in range(past // cc):
        cch = _cumsum_lanes(triu, clf_ref[0, :, c * cc:(c + 1) * cc]) + carry
        carry = cch[:, cc - 1:cc]
        c_chunks.append(cch)
    c_past = jnp.concatenate(c_chunks, axis=1)
    ncp = jnp.concatenate([jnp.broadcast_to(-c_past[hh:hh + 1, :], (ts, past)) for hh in range(H_FOX)], axis=0)
    ncn = ncn_scr[:H_FOX, :] - carry
    ncn = jnp.concatenate([jnp.broadcast_to(ncn[hh:hh + 1, :], (ts, n)) for hh in range(H_FOX)], axis=0)

    s_p = _mm(q_bd, ckt_ref[0].astype(BF16)) + ncp
    s_n = _mm(q_bd, knt_scr[...]) + ncn
    tok = lax.broadcasted_iota(jnp.int32, (nr, n), 1)
    qt = _imod(lax.broadcasted_iota(jnp.int32, (nr, n), 0), ts)
    valid = jnp.logical_and(_idiv(tok, ts) == b, _imod(tok, ts) <= qt)
    s_n = jnp.where(valid, s_n, -jnp.inf)
    m = jnp.maximum(jnp.max(s_p, axis=-1, keepdims=True), jnp.max(s_n, axis=-1, keepdims=True))
    p_p = jnp.exp(s_p - m)
    p_n = jnp.exp(s_n - m)
    l = jnp.sum(p_p, axis=-1, keepdims=True) + jnp.sum(p_n, axis=-1, keepdims=True)
    o_past = _mm(cvt_ref[0].astype(BF16), p_p.T.astype(BF16)).T
    o_bd = (o_past + _mm(p_n.astype(BF16), vn_scr[...])) / l
    o_bd = jnp.where(head_sel, o_bd, 0.0)
    fo = o_bd[0:ts, :]
    for hh in range(1, H_FOX):
        fo = fo + o_bd[hh * ts:(hh + 1) * ts, :]
    fo_scr[pl.ds(r0, ts), :] = fo

    tokr = lax.broadcasted_iota(jnp.int32, (ts, n), 1)
    qtr = lax.broadcasted_iota(jnp.int32, (ts, n), 0)
    in_b = _idiv(tokr, ts) == b
    dt = (qtr - _imod(tokr, ts)).astype(F32)
    tokc = lax.broadcasted_iota(jnp.int32, (1, n), 1)
    kpos = _imod(tokc, ts).astype(F32)
    rowi = lax.broadcasted_iota(jnp.int32, (ts, 1), 0).astype(F32)
    r_parts = []
    for hh in range(H_RET):
        lg = lg_ref[hh]
        q = rq_scr[pl.ds(r0, ts), hh * DK_RET:(hh + 1) * DK_RET].astype(BF16)
        kt = rkt_scr[hh * DK_RET:(hh + 1) * DK_RET, :]
        v = rv_scr[:, hh * DV_RET:(hh + 1) * DV_RET]
        state = st0_ref[0, hh]
        dmask = jnp.where(jnp.logical_and(in_b, dt >= 0.0), jnp.exp(lg * jnp.maximum(dt, 0.0)), 0.0)
        inter = _mm(q, state.astype(BF16)) * jnp.exp(lg * (rowi + 1.0))
        scores = _mm(q, kt.astype(BF16)) * dmask
        r_parts.append(inter + _mm(scores.astype(BF16), v))
        deck = jnp.where(_idiv(tokc, ts) == b,jnp.exp(lg * (ts - 1.0 - kpos)), 0.0)
        st_ref[0, hh] = jnp.exp(jnp.full((1, DV_RET), lg * ts, F32)) * state + _mm((kt * deck).astype(BF16), v)
    ro_scr[pl.ds(r0, ts), :] = jnp.concatenate(r_parts, axis=1)

    m_parts = []
    for hh in range(H_MEM):
        cols = slice(hh * DH_MEM, (hh + 1) * DH_MEM)
        q = mq_scr[pl.ds(r0, ts), cols].astype(BF16)
        rows = pl.ds(hh, N_MEM, stride=H_MEM)
        s = _mm_nt(q, cmk_ref[0, rows, :].astype(BF16)) * (DH_MEM ** -0.5)
        p = jnp.exp(s - jnp.max(s, axis=-1, keepdims=True))
        l = jnp.sum(p, axis=-1, keepdims=True)
        m_parts.append(_mm(p.astype(BF16), cmv_ref[0, rows, :].astype(BF16)) / l)
    mo_scr[pl.ds(r0, ts), :] = jnp.concatenate(m_parts, axis=1)

    @pl.when(b == nb - 1)
    def _output_all():
        h = h_scr[...]
        a_in = (fo_scr[...] * _silu(_mm(h, w_ref[:, OFF_FG:OFF_FG + W_FOX]))).astype(BF16)
        merged = _gate(h, w_ref, bm_ref, 0) * _mm(a_in, wpf_ref[...])
        zrg = _mm(h, w_ref[:, OFF_RG:OFF_RG + W_RET_V])
        ro = ro_scr[...]
        r_parts = []
        for hh in range(H_RET):
            cols = slice(hh * DV_RET, (hh + 1) * DV_RET)
            r_parts.append(_group_norm(ro[:, cols]) * _silu(zrg[:, cols]))
        r_in = jnp.concatenate(r_parts, axis=1).astype(BF16)
        merged = merged + _gate(h, w_ref, bm_ref, 1) * _mm(r_in, wpr_ref[...])
        m_in = (mo_scr[...] * _silu(_mm(h, w_ref[:, OFF_MG:OFF_MG + W_MEM]))).astype(BF16)
        merged = merged + _gate(h, w_ref, bm_ref, 2) * _mm(m_in, wpm_ref[...])
        y_ref[...] = x_ref[...] + _mm(merged.astype(BF16), wo_ref[...])


def _sample_call(x2d, cos, sin, ckt, cvt, clf_t, st0, cmk, cmv, wf_t, bf_col, log_g, g_norm, b_merge, gfq, gfk, gmq, seg,
                 w_main, wpf, wpr, wpm, wo, nb, ts):
    n = nb * ts
    past = ckt.shape[2]
    kernel = functools.partial(_sample_kernel, nb=nb, ts=ts, past=past)

    def res(shape):
        nd = len(shape)
        return pl.BlockSpec(shape, lambda b: (0,) * nd, pipeline_mode=pl.Buffered(1))

    in_specs = [
        pl.BlockSpec(memory_space=pltpu.SMEM),
        res((n, D_MODEL)),
        res((n, DK_RET)),
        res((n, DK_RET)),
        pl.BlockSpec((1, W_FOX, past), lambda b: (b, 0, 0)),
        pl.BlockSpec((1, W_FOX, past), lambda b: (b, 0, 0)),
        pl.BlockSpec((1, H_FOX, past), lambda b: (b, 0, 0)),
        pl.BlockSpec((1, H_RET, DK_RET, DV_RET), lambda b: (b, 0, 0, 0)),
        pl.BlockSpec((1, N_MEM * H_MEM, DH_MEM), lambda b: (b, 0, 0)),
        pl.BlockSpec((1, N_MEM * H_MEM, DH_MEM), lambda b: (b, 0, 0)),
        res((1, D_MODEL)),
        res((F_ROWS, 1)),
        res((1, N_BRANCH * D_MODEL)),
        res((1, W_FOX)),
        res((1, W_FOX)),
        res((1, DH_MEM)),
        res((MXU_TILE, MXU_TILE)),
        res((D_MODEL, W_MAIN)),
        res((W_FOX, D_MODEL)),
        res((W_RET_V, D_MODEL)),
        res((W_MEM, D_MODEL)),
        res((D_MODEL, D_MODEL)),
        res((F_ROWS, D_MODEL)),
    ]
    full = lambda shape: pl.BlockSpec(shape, lambda b: (0,) * len(shape))
    out_specs = [
        full((n, D_MODEL)),
        full((n, W_FOX)),
        full((n, W_FOX)),
        full((H_FOX, n)),
        pl.BlockSpec((1, H_RET, DK_RET, DV_RET), lambda b: (b, 0, 0, 0)),
    ]
    out_shape = [
        jax.ShapeDtypeStruct((n, D_MODEL), F32),
        jax.ShapeDtypeStruct((n, W_FOX), F32),
        jax.ShapeDtypeStruct((n, W_FOX), F32),
        jax.ShapeDtypeStruct((H_FOX, n), F32),
        jax.ShapeDtypeStruct((nb, H_RET, DK_RET, DV_RET), F32),
    ]
    scratch = [
        pltpu.VMEM((n, D_MODEL), BF16),
        pltpu.VMEM((n, W_FOX), F32),
        pltpu.VMEM((W_FOX, n), BF16),
        pltpu.VMEM((n, W_FOX), BF16),
        pltpu.VMEM((F_ROWS, n), F32),
        pltpu.VMEM((n, W_RET_QK), F32),
        pltpu.VMEM((W_RET_QK, n), F32),
        pltpu.VMEM((n, W_RET_V), BF16),
        pltpu.VMEM((n, W_MEM), F32),
        pltpu.VMEM((n, W_FOX), F32),
        pltpu.VMEM((n, W_RET_V), F32),
        pltpu.VMEM((n, W_MEM), F32),
    ]
    return pl.pallas_call(
        kernel,
        grid=(nb,),
        in_specs=in_specs,
        out_specs=out_specs,
        out_shape=out_shape,
        scratch_shapes=scratch,
        compiler_params=pltpu.CompilerParams(
            dimension_semantics=("arbitrary",), vmem_limit_bytes=VMEM_LIMIT_BYTES),
        name="sample",
    )(log_g, x2d, cos, sin, ckt, cvt, clf_t, st0, cmk, cmv, g_norm, bf_col, b_merge, gfq, gfk, gmq, seg,
      w_main, wpf, wpr, wpm, wo, wf_t)


def _rope_tables(pos):
    half = DK_RET // 2
    inv = ROPE_BASE ** (-jnp.arange(half, dtype=F32) / half)
    ang = pos.astype(F32)[:, None] * inv[None, :]
    cos = jnp.cos(ang)
    sin = jnp.sin(ang)
    return jnp.concatenate([cos, cos], axis=1), jnp.concatenate([-sin, sin], axis=1)


def _reorder_w_in(w_t):
    assert (3 * W_FOX) % REORDER_ROWS == 0 and W_MAIN % REORDER_ROWS == 0 and REORDER_ROWS % H_FOX == 0
    assert w_t.shape == (W_MAIN + H_FOX, D_MODEL)
    return pl.pallas_call(
        _reorder_kernel,
        grid=(W_MAIN // REORDER_ROWS,),
        in_specs=[
            pl.BlockSpec((REORDER_ROWS, D_MODEL), lambda c: (c, 0)),
            pl.BlockSpec((H_FOX, D_MODEL), lambda c: ((c + 1) * (REORDER_ROWS // H_FOX), 0)),
        ],
        out_specs=pl.BlockSpec((D_MODEL, REORDER_ROWS), lambda c: (0, c)),
        out_shape=jax.ShapeDtypeStruct((D_MODEL, W_MAIN), BF16),
        compiler_params=pltpu.CompilerParams(dimension_semantics=("arbitrary",)),
        name="reorder_w_in",
    )(w_t, w_t)


def _reorder_kernel(a_ref, b_ref, o_ref):
    c = pl.program_id(0)
    a = a_ref[...]
    shifted = jnp.concatenate([a[H_FOX:], b_ref[...]], axis=0)
    src = jnp.where(c < (3 * W_FOX) // REORDER_ROWS, a, shifted)
    o_ref[...] = src.T.astype(BF16)


def kernel(x_prompt, x_sample, mem_prompt, cache_fox_k, cache_fox_v, cache_fox_logf, state_ret, cache_mem_k, cache_mem_v,
           g_norm, g_mem_norm, w_in, b_f, b_merge, g_fox_q, g_fox_k, g_mem_q, g_mem_k, w_mem_kv,
           w_p_fox, w_p_ret, w_p_mem, w_out):
    depth = w_in.shape[0]
    assert depth == 1, "single-layer kernel"
    B, T, _ = x_prompt.shape
    nb, ts, _ = x_sample.shape
    past = cache_fox_k.shape[2]

    log_g = jnp.log1p(-jnp.exp2(-5.0 - jnp.arange(H_RET, dtype=F32)))
    head_of_lane = jnp.arange(MXU_TILE) // DH_FOX
    seg = (head_of_lane[:, None] == head_of_lane[None, :]).astype(BF16)
    w_t = jnp.swapaxes(w_in[0], 0, 1)
    w_main = _reorder_w_in(w_t)
    gn = g_norm[0].reshape(1, D_MODEL)
    bm = b_merge[0].reshape(1, N_BRANCH * D_MODEL)
    gfq = jnp.tile(g_fox_q[0], H_FOX).reshape(1, W_FOX)
    gfk = jnp.tile(g_fox_k[0], H_FOX).reshape(1, W_FOX)
    gmq = g_mem_q[0].reshape(1, DH_MEM)
    wpf = w_p_fox[0].astype(BF16)
    wpr = w_p_ret[0].astype(BF16)
    wpm = w_p_mem[0].astype(BF16)
    wo = w_out[0].astype(BF16)
    shared = (gn, bm, gfq, gfk, gmq, seg, w_main, wpf, wpr, wpm, wo)

    mk, mv, mkt, mvb = _memkv_call(mem_prompt, g_mem_norm[0], w_mem_kv[0], g_mem_k[0])

    cos_p, sin_p = _rope_tables(jnp.arange(T, dtype=jnp.int32))
    wvf_t = jnp.concatenate(
        [w_t[2 * W_FOX:3 * W_FOX + H_FOX], jnp.zeros((F_ROWS - H_FOX, D_MODEL), F32)], axis=0).astype(BF16)
    bf_col = jnp.concatenate([b_f[0], jnp.zeros((F_ROWS - H_FOX,), F32)]).reshape(F_ROWS, 1)
    y_p, fkt_p, fvt_p, lft_p, st_p = _prompt_call(
        x_prompt, cos_p, sin_p, mkt, mvb, wvf_t, bf_col, log_g, *shared)

    def time_major(a_t):
        return jnp.transpose(a_t.reshape(a_t.shape[0], H_FOX, DH_FOX, a_t.shape[2]), (0, 3, 1, 2))[None]

    def time_minor(a):
        return jnp.transpose(a, (0, 2, 3, 1)).reshape(a.shape[0], W_FOX, a.shape[1])

    fk_p = time_major(fkt_p)
    fv_p = time_major(fvt_p)
    lf_p = jnp.swapaxes(lft_p, 1, 2)[None]

    pos_s = past + jnp.arange(ts, dtype=jnp.int32)
    cos_s, sin_s = _rope_tables(jnp.tile(pos_s, nb))
    y_s, fk_s, fv_s, lft_s, st_s = _sample_call(
        x_sample.reshape(nb * ts, D_MODEL), cos_s, sin_s,
        time_minor(cache_fox_k[0]), time_minor(cache_fox_v[0]),
        jnp.swapaxes(cache_fox_logf[0], 1, 2), state_ret[0],
        cache_mem_k[0].reshape(nb, N_MEM * H_MEM, DH_MEM), cache_mem_v[0].reshape(nb, N_MEM * H_MEM, DH_MEM),
        wvf_t[W_FOX:], bf_col, log_g, *shared, nb=nb, ts=ts)
    lf_s = jnp.transpose(lft_s.reshape(H_FOX, nb, ts), (1, 2, 0))[None]

    return (y_p, y_s.reshape(nb, ts, D_MODEL),
            fk_p, fv_p, lf_p,
            st_p[None], mk.reshape(1, B, N_MEM, H_MEM, DH_MEM), mv.reshape(1, B, N_MEM, H_MEM, DH_MEM),
            fk_s.reshape(1, nb, ts, H_FOX, DH_FOX), fv_s.reshape(1, nb, ts, H_FOX, DH_FOX),
            lf_s, st_s[None])
```

```python
import functools

import jax
import jax.numpy as jnp
from jax import lax
from jax.experimental import pallas as pl
from jax.experimental.pallas import tpu as pltpu

D_MODEL = 1024
N_MEM = 256
H_FOX = 8
DH_FOX = 64
H_RET = 4
DK_RET = 128
DV_RET = 256
H_MEM = 4
DH_MEM = 128
W_FOX = H_FOX * DH_FOX
W_RET_QK = H_RET * DK_RET
W_RET_V = H_RET * DV_RET
W_MEM = H_MEM * DH_MEM
N_BRANCH = 3
ROPE_BASE = 10000.0
EPS = 1e-6
LOG2E = 1.4426950408889634

LANES = 128
MXU_TILE = 256
BF16_SUBLANES = 16
V_ROWS = DH_FOX + BF16_SUBLANES
F_ROWS = BF16_SUBLANES

OFF_FQ = 0
OFF_FK = OFF_FQ + W_FOX
OFF_FV = OFF_FK + W_FOX
OFF_FG = OFF_FV + W_FOX
OFF_RQ = OFF_FG + W_FOX
OFF_RK = OFF_RQ + W_RET_QK
OFF_RV = OFF_RK + W_RET_QK
OFF_RG = OFF_RV + W_RET_V
OFF_MQ = OFF_RG + W_RET_V
OFF_MG = OFF_MQ + W_MEM
OFF_GL = OFF_MG + W_MEM
W_MAIN = OFF_GL + N_BRANCH * D_MODEL
REORDER_ROWS = 512

PROMPT_BLOCK = 256
VMEM_LIMIT_BYTES = 60 * 1024 * 1024

F32 = jnp.float32
BF16 = jnp.bfloat16


def _mm(a, b):
    return jnp.dot(a, b, preferred_element_type=F32)


def _mm_nt(a, b):
    return lax.dot_general(a, b, (((1,), (1,)), ((), ())), preferred_element_type=F32)


def _idiv(x, d):
    assert d & (d - 1) == 0
    return lax.shift_right_logical(x, d.bit_length() - 1)


def _imod(x, d):
    assert d & (d - 1) == 0
    return jnp.bitwise_and(x, d - 1)


def _rms_rows(x, g):
    ms = jnp.mean(x * x, axis=-1, keepdims=True)
    return x * lax.rsqrt(ms + EPS) * g


def _head_sumsq(z, seg_ref):
    sq = (z * z).astype(BF16)
    return jnp.concatenate(
        [_mm(sq[:, c:c + MXU_TILE], seg_ref[...]) for c in range(0, W_FOX, MXU_TILE)], axis=1)


def _head_norm64(z, g, seg_ref):
    return z * lax.rsqrt(_head_sumsq(z, seg_ref) * (1.0 / DH_FOX) + EPS) * g


def _log_sigmoid(u):
    return jnp.minimum(u, 0.0) - jnp.log1p(jnp.exp(-jnp.abs(u)))


def _silu(u):
    return u * jax.nn.sigmoid(u)


def _split3(a):
    hi = a.astype(BF16)
    r1 = a - hi.astype(F32)
    mid = r1.astype(BF16)
    lo = (r1 - mid.astype(F32)).astype(BF16)
    return hi, mid, lo


def _tri_cumsum(tri, a):
    hi, mid, lo = _split3(a)
    return _mm(tri, hi) + _mm(tri, mid) + _mm(tri, lo)


def _cumsum_lanes(triu, a):
    hi, mid, lo = _split3(a)
    return _mm(hi, triu) + _mm(mid, triu) + _mm(lo, triu)


def _rope(x, cos, sin_signed):
    return x * cos + pltpu.roll(x, DK_RET // 2, 1) * sin_signed


def _group_norm(o):
    mu = jnp.mean(o, axis=-1, keepdims=True)
    d = o - mu
    var = jnp.mean(d * d, axis=-1, keepdims=True)
    return d * lax.rsqrt(var + EPS)


def _gate(h, w_ref, bm_ref, i):
    z = _mm(h, w_ref[:, OFF_GL + i * D_MODEL:OFF_GL + (i + 1) * D_MODEL])
    return jax.nn.sigmoid(z + bm_ref[:, i * D_MODEL:(i + 1) * D_MODEL])


def _memkv_kernel(mem_ref, gmn_ref, w_ref, gmk_ref, mk_ref, mv_ref, mkt_ref, mvb_ref):
    h = _rms_rows(mem_ref[0], gmn_ref[...]).astype(BF16)
    z = _mm(h, w_ref[...])
    ks = []
    for hh in range(H_MEM):
        ks.append(_rms_rows(z[:, hh * DH_MEM:(hh + 1) * DH_MEM], gmk_ref[...]))
    mk = jnp.concatenate(ks, axis=1)
    mv = z[:, W_MEM:]
    for hh in range(H_MEM):
        rows = pl.ds(hh, N_MEM, stride=H_MEM)
        mk_ref[0, rows, :] = ks[hh]
        mv_ref[0, rows, :] = mv[:, hh * DH_MEM:(hh + 1) * DH_MEM]
    mkt_ref[0] = mk.T.astype(BF16)
    mvb_ref[0] = mv.astype(BF16)


def _memkv_call(mem, g_mem_norm, w_mem_kv, g_mem_k):
    B = mem.shape[0]
    const = lambda b: (0, 0)
    per_b = lambda b: (b, 0, 0)
    return pl.pallas_call(
        _memkv_kernel,
        grid=(B,),
        in_specs=[
            pl.BlockSpec((1, N_MEM, D_MODEL), per_b),
            pl.BlockSpec((1, D_MODEL), const),
            pl.BlockSpec((D_MODEL, 2 * W_MEM), const),
            pl.BlockSpec((1, DH_MEM), const),
        ],
        out_specs=[
            pl.BlockSpec((1, N_MEM * H_MEM, DH_MEM), per_b),
            pl.BlockSpec((1, N_MEM * H_MEM, DH_MEM), per_b),
            pl.BlockSpec((1, W_MEM, N_MEM), per_b),
            pl.BlockSpec((1, N_MEM, W_MEM), per_b),
        ],
        out_shape=[
            jax.ShapeDtypeStruct((B, N_MEM * H_MEM, DH_MEM), F32),
            jax.ShapeDtypeStruct((B, N_MEM * H_MEM, DH_MEM), F32),
            jax.ShapeDtypeStruct((B, W_MEM, N_MEM), BF16),
            jax.ShapeDtypeStruct((B, N_MEM, W_MEM), BF16),
        ],
        compiler_params=pltpu.CompilerParams(dimension_semantics=("arbitrary",)),
        name="memkv",
    )(mem, g_mem_norm.reshape(1, D_MODEL), w_mem_kv.astype(BF16), g_mem_k.reshape(1, DH_MEM))


def _prompt_kernel(lg_ref, x_ref, cos_ref, sin_ref, mkt_ref, mvb_ref, gn_ref, bf_ref, bm_ref, gfq_ref, gfk_ref,
                   gmq_ref, w_ref, wpf_ref, wpr_ref, wpm_ref, wo_ref, wqkvf_ref,
                   y_ref, fkt_ref, fvt_ref, lft_ref, st_ref,
                   k_scr, vt_scr, qt_scr, m_scr, acc_scr, sa_scr, sb_scr, ccar_scr, dmask_scr, decq_scr, deck_scr,
                   ssc_scr, *, tb):
    b = pl.program_id(0)
    j = pl.program_id(1)
    t0 = pl.multiple_of(j * tb, tb)

    @pl.when(jnp.logical_and(b == 0, j == 0))
    def _init_tables():
        ri = lax.broadcasted_iota(jnp.int32, (tb, tb), 0)
        ci = lax.broadcasted_iota(jnp.int32, (tb, tb), 1)
        diff = (ri - ci).astype(F32)
        rowi = lax.broadcasted_iota(jnp.int32, (tb, 1), 0).astype(F32)
        coli = lax.broadcasted_iota(jnp.int32, (1, tb), 1).astype(F32)
        for hh in range(H_RET):
            lg = lg_ref[hh]
            dmask_scr[hh] = jnp.where(diff >= 0.0, jnp.exp(lg * jnp.maximum(diff, 0.0)), 0.0)
            decq_scr[hh] = jnp.exp(lg * (rowi + 1.0))
            deck_scr[hh] = jnp.exp(lg * (tb - 1.0 - coli))
            ssc_scr[hh] = jnp.exp(jnp.full((1, DV_RET), lg * tb, F32))

    @pl.when(j == 0)
    def _init_batch():
        ccar_scr[...] = jnp.zeros_like(ccar_scr)
        st_ref[...] = jnp.zeros_like(st_ref)

    h = _rms_rows(x_ref[0], gn_ref[...]).astype(BF16)

    def proj(off, width):
        return _mm(h, w_ref[:, off:off + width])

    def gate(i):
        return jax.nn.sigmoid(proj(OFF_GL + i * D_MODEL, D_MODEL) + bm_ref[:, i * D_MODEL:(i + 1) * D_MODEL])


    half = (3 * W_FOX + F_ROWS) // (2 * BF16_SUBLANES) * BF16_SUBLANES
    z_t = jnp.concatenate([_mm_nt(wqkvf_ref[:half, :], h), _mm_nt(wqkvf_ref[half:, :], h)], axis=0)
    zrq = proj(OFF_RQ, W_RET_QK)
    zrk = proj(OFF_RK, W_RET_QK)

    def head_norm_t(zh_t, g_col):
        parts = []
        for hh in range(H_FOX):
            zz = zh_t[hh * DH_FOX:(hh + 1) * DH_FOX]
            ms = jnp.sum(zz * zz, axis=0, keepdims=True) * (1.0 / DH_FOX)
            parts.append(zz * lax.rsqrt(ms + EPS))
        return jnp.concatenate(parts, axis=0) * g_col

    fq_t = head_norm_t(z_t[:W_FOX], gfq_ref[...]) * (DH_FOX ** -0.5 * LOG2E)
    fk_t = head_norm_t(z_t[W_FOX:2 * W_FOX], gfk_ref[...])
    fv_t = z_t[2 * W_FOX:3 * W_FOX]
    logf_t = _log_sigmoid(z_t[3 * W_FOX:] + bf_ref[...])
    ri = lax.broadcasted_iota(jnp.int32, (tb, tb), 0)
    ci = lax.broadcasted_iota(jnp.int32, (tb, tb), 1)
    key_le_query = ri <= ci
    c_t = _cumsum_lanes(jnp.where(key_le_query, 1.0, 0.0).astype(BF16), logf_t) + ccar_scr[...]
    ccar_scr[...] = c_t[:, tb - 1:tb]
    rv = proj(OFF_RV, W_RET_V).astype(BF16)
    fk = fk_t.T
    fkt_ref[0] = fk_t
    fvt_ref[0] = fv_t
    lft_ref[0] = logf_t[:H_FOX]
    hi, mid, lo = _split3(c_t[:H_FOX] * (-LOG2E))
    parts_t = jnp.concatenate(
        [hi.astype(F32), mid.astype(F32), lo.astype(F32), jnp.zeros((LANES - 3 * H_FOX, tb), F32)], axis=0)
    bias = pltpu.roll(parts_t.T, DH_FOX, 1)
    lane = lax.broadcasted_iota(jnp.int32, (tb, LANES), 1)
    vrow = lax.broadcasted_iota(jnp.int32, (V_ROWS - DH_FOX, tb), 0)
    ones_row = jnp.where(vrow == 0, 1.0, 0.0).astype(BF16)
    fv_tb = fv_t.astype(BF16)
    for hh in range(H_FOX):
        vt_scr[hh, :, pl.ds(t0, tb)] = jnp.concatenate([fv_tb[hh * DH_FOX:(hh + 1) * DH_FOX, :], ones_row], axis=0)
    qrow = lax.broadcasted_iota(jnp.int32, (LANES - DH_FOX, tb), 0)
    k_blk = []
    qt_blk = []
    for hh in range(H_FOX):
        pair = fk[:, (hh // 2) * LANES:(hh // 2 + 1) * LANES]
        if hh % 2:
            pair = pltpu.roll(pair, DH_FOX, 1)
        k_aug = jnp.where(lane < DH_FOX, pair, bias).astype(BF16)
        k_scr[hh, pl.ds(t0, tb), :] = k_aug
        k_blk.append(k_aug)
        ones = jnp.where(
            jnp.logical_or(qrow == hh, jnp.logical_or(qrow == H_FOX + hh, qrow == 2 * H_FOX + hh)), 1.0, 0.0)
        qt_aug = jnp.concatenate([fq_t[hh * DH_FOX:(hh + 1) * DH_FOX, :], ones], axis=0).astype(BF16)
        qt_scr[hh] = qt_aug
        qt_blk.append(qt_aug)

    s_all = [_mm(k_blk[hh], qt_blk[hh]) for hh in range(H_FOX)]
    zrg = proj(OFF_RG, W_RET_V)
    p_all = []
    for hh in range(H_FOX):
        s_t = jnp.where(key_le_query, s_all[hh], -jnp.inf)
        m = jnp.max(s_t, axis=0, keepdims=True)
        m_scr[hh] = m
        p_all.append(jnp.exp2((s_t - m).astype(BF16)))
    for hh in range(H_FOX):
        acc_scr[hh] = _mm(vt_scr[hh, :, pl.ds(t0, tb)], p_all[hh])

    def scores(buf, kb, hh):
        off = pl.multiple_of(kb * tb, tb)
        buf[hh] = _mm(k_scr[hh, pl.ds(off, tb), :], qt_scr[hh])

    def softmax(buf, hh):
        s_t = buf[hh]
        m_old = m_scr[hh]
        m_new = jnp.maximum(m_old, jnp.max(s_t, axis=0, keepdims=True))
        m_scr[hh] = m_new
        return hh, jnp.exp2((s_t - m_new).astype(BF16)), jnp.exp2(m_old - m_new)

    def accumulate(kb, pending):
        hh, p_t, alpha = pending
        off = pl.multiple_of(kb * tb, tb)
        acc_scr[hh] = alpha * acc_scr[hh] + _mm(vt_scr[hh, :, pl.ds(off, tb)], p_t)

    def absorb(buf, kb, also=None):
        pending = None
        for hh in range(H_FOX):
            cur = softmax(buf, hh)
            if also is not None:
                also(hh)
            if pending is not None:
                accumulate(kb, pending)
            pending = cur
        accumulate(kb, pending)

    def run_blocks(kb, count):
        bufs = (sa_scr, sb_scr)
        for hh in range(H_FOX):
            scores(bufs[0], kb, hh)
        for c in range(count):
            nxt = None
            if c + 1 < count:
                nxt = functools.partial(scores, bufs[(c + 1) % 2], kb + c + 1)
            absorb(bufs[c % 2], kb + c, also=nxt)

    def kv_quad(i, carry):
        run_blocks(4 * i, 4)
        return carry

    n_quads = lax.shift_right_logical(j, 2)
    lax.fori_loop(0, n_quads, kv_quad, 0)

    @pl.when(jnp.bitwise_and(j, 2) == 2)
    def _pair():
        run_blocks(4 * n_quads, 2)

    @pl.when(jnp.bitwise_and(j, 1) == 1)
    def _single():
        run_blocks(j - 1, 1)

    zfg = proj(OFF_FG, W_FOX)
    gate0 = gate(0)
    fo = jnp.concatenate(
        [acc_scr[hh, :DH_FOX, :] / acc_scr[hh, DH_FOX:DH_FOX + 1, :] for hh in range(H_FOX)], axis=0).T
    a_in = (fo * _silu(zfg)).astype(BF16)

    cos = cos_ref[...]
    sin = sin_ref[...]
    rq, rkt, rvs, states = [], [], [], []
    for hh in range(H_RET):
        cols = slice(hh * DK_RET, (hh + 1) * DK_RET)
        rq.append(_rope(zrq[:, cols], cos, sin).astype(BF16))
        rkt.append((_rope(zrk[:, cols], cos, sin) * (DK_RET ** -0.5)).T)
        rvs.append(rv[:, hh * DV_RET:(hh + 1) * DV_RET])
        states.append(st_ref[0, hh])
    inter = [_mm(rq[hh], states[hh].astype(BF16)) for hh in range(H_RET)]
    scores = [_mm(rq[hh], rkt[hh].astype(BF16)) for hh in range(H_RET)]
    zmq = proj(OFF_MQ, W_MEM)
    merged = gate0 * _mm(a_in, wpf_ref[...])
    ret_o = [inter[hh] * decq_scr[hh] + _mm((scores[hh] * dmask_scr[hh]).astype(BF16), rvs[hh])
             for hh in range(H_RET)]
    for hh in range(H_RET):
        st_ref[0, hh] = ssc_scr[hh] * states[hh] + _mm((rkt[hh] * deck_scr[hh]).astype(BF16), rvs[hh])
    gate1 = gate(1)
    zmg = proj(OFF_MG, W_MEM)

    mem_s = []
    for hh in range(H_MEM):
        cols = slice(hh * DH_MEM, (hh + 1) * DH_MEM)
        q = _rms_rows(zmq[:, cols], gmq_ref[...]).astype(BF16)
        mem_s.append(_mm(q, mkt_ref[0, cols, :]) * (DH_MEM ** -0.5))
    r_in = jnp.concatenate(
        [_group_norm(ret_o[hh]) * _silu(zrg[:, hh * DV_RET:(hh + 1) * DV_RET]) for hh in range(H_RET)],
        axis=1).astype(BF16)
    merged = merged + gate1 * _mm(r_in, wpr_ref[...])
    m_parts = []
    for hh in range(H_MEM):
        cols = slice(hh * DH_MEM, (hh + 1) * DH_MEM)
        p = jnp.exp(mem_s[hh] - jnp.max(mem_s[hh], axis=-1, keepdims=True))
        l = jnp.sum(p, axis=-1, keepdims=True)
        m_parts.append(_mm(p.astype(BF16), mvb_ref[0, :, cols]) / l)
    gate2 = gate(2)
    m_in = (jnp.concatenate(m_parts, axis=1) * _silu(zmg)).astype(BF16)
    merged = merged + gate2 * _mm(m_in, wpm_ref[...])

    y_ref[0] = x_ref[0] + _mm(merged.astype(BF16), wo_ref[...])


def _resident(shape):
    nd = len(shape)
    return pl.BlockSpec(shape, lambda b, j: (0,) * nd, pipeline_mode=pl.Buffered(1))


def _prompt_call(x, cos, sin, mkt, mvb, wqkvf_t, bf_col, gfq_col, gfk_col, log_g, g_norm, b_merge, gmq, w_main, wpf,
                 wpr, wpm, wo, tb=PROMPT_BLOCK):
    B, T, _ = x.shape
    nt = T // tb
    kernel = functools.partial(_prompt_kernel, tb=tb)
    in_specs = [
        pl.BlockSpec(memory_space=pltpu.SMEM),
        pl.BlockSpec((1, tb, D_MODEL), lambda b, j: (b, j, 0)),
        pl.BlockSpec((tb, DK_RET), lambda b, j: (j, 0)),
        pl.BlockSpec((tb, DK_RET), lambda b, j: (j, 0)),
        pl.BlockSpec((1, W_MEM, N_MEM), lambda b, j: (b, 0, 0)),
        pl.BlockSpec((1, N_MEM, W_MEM), lambda b, j: (b, 0, 0)),
        _resident((1, D_MODEL)),
        _resident((F_ROWS, 1)),
        _resident((1, N_BRANCH * D_MODEL)),
        _resident((W_FOX, 1)),
        _resident((W_FOX, 1)),
        _resident((1, DH_MEM)),
        _resident((D_MODEL, W_MAIN)),
        _resident((W_FOX, D_MODEL)),
        _resident((W_RET_V, D_MODEL)),
        _resident((W_MEM, D_MODEL)),
        _resident((D_MODEL, D_MODEL)),
        _resident((3 * W_FOX + F_ROWS, D_MODEL)),
    ]
    out_specs = [
        pl.BlockSpec((1, tb, D_MODEL), lambda b, j: (b, j, 0)),
        pl.BlockSpec((1, W_FOX, tb), lambda b, j: (b, 0, j)),
        pl.BlockSpec((1, W_FOX, tb), lambda b, j: (b, 0, j)),
        pl.BlockSpec((1, H_FOX, tb), lambda b, j: (b, 0, j)),
        pl.BlockSpec((1, H_RET, DK_RET, DV_RET), lambda b, j: (b, 0, 0, 0)),
    ]
    out_shape = [
        jax.ShapeDtypeStruct((B, T, D_MODEL), F32),
        jax.ShapeDtypeStruct((B, W_FOX, T), F32),
        jax.ShapeDtypeStruct((B, W_FOX, T), F32),
        jax.ShapeDtypeStruct((B, H_FOX, T), F32),
        jax.ShapeDtypeStruct((B, H_RET, DK_RET, DV_RET), F32),
    ]
    scratch = [
        pltpu.VMEM((H_FOX, T, LANES), BF16),
        pltpu.VMEM((H_FOX, V_ROWS, T), BF16),
        pltpu.VMEM((H_FOX, LANES, tb), BF16),
        pltpu.VMEM((H_FOX, 1, tb), F32),
        pltpu.VMEM((H_FOX, V_ROWS, tb), F32),
        pltpu.VMEM((H_FOX, tb, tb), F32),
        pltpu.VMEM((H_FOX, tb, tb), F32),
        pltpu.VMEM((F_ROWS, 1), F32),
        pltpu.VMEM((H_RET, tb, tb), F32),
        pltpu.VMEM((H_RET, tb, 1), F32),
        pltpu.VMEM((H_RET, 1, tb), F32),
        pltpu.VMEM((H_RET, 1, DV_RET), F32),
    ]
    return pl.pallas_call(
        kernel,
        grid=(B, nt),
        in_specs=in_specs,
        out_specs=out_specs,
        out_shape=out_shape,
        scratch_shapes=scratch,
        compiler_params=pltpu.CompilerParams(
            dimension_semantics=("arbitrary", "arbitrary"), vmem_limit_bytes=VMEM_LIMIT_BYTES),
        name="prompt",
    )(log_g, x, cos, sin, mkt, mvb, g_norm, bf_col, b_merge, gfq_col, gfk_col, gmq, w_main, wpf, wpr, wpm, wo, wqkvf_t)


def _sample_kernel(lg_ref, x_ref, cos_ref, sin_ref, ckt_ref, cvt_ref, clf_ref, st0_ref, cmk_ref, cmv_ref,
                   gn_ref, bf_ref, bm_ref, gfq_ref, gfk_ref, gmq_ref, seg_ref, w_ref, wpf_ref, wpr_ref, wpm_ref,
                   wo_ref, wft_ref,
                   y_ref, fk_ref, fv_ref, lft_ref, st_ref,
                   h_scr, fq_scr, knt_scr, vn_scr, ncn_scr, rq_scr, rkt_scr, rv_scr, mq_scr, fo_scr, ro_scr, mo_scr,
                   *, nb, ts, past):
    b = pl.program_id(0)
    n = nb * ts
    r0 = pl.multiple_of(b * ts, ts)

    @pl.when(b == 0)
    def _project_all():
        h = _rms_rows(x_ref[...], gn_ref[...]).astype(BF16)
        h_scr[...] = h
        fq_scr[...] = _head_norm64(_mm(h, w_ref[:, OFF_FQ:OFF_FQ + W_FOX]), gfq_ref[...], seg_ref) * (DH_FOX ** -0.5)
        fk = _head_norm64(_mm(h, w_ref[:, OFF_FK:OFF_FK + W_FOX]), gfk_ref[...], seg_ref)
        fv = _mm(h, w_ref[:, OFF_FV:OFF_FV + W_FOX])
        logf_t = _log_sigmoid(_mm_nt(wft_ref[...], h) + bf_ref[...])
        fk_ref[...] = fk
        fv_ref[...] = fv
        lft_ref[...] = logf_t[:H_FOX]
        knt_scr[...] = fk.T.astype(BF16)
        vn_scr[...] = fv.astype(BF16)
        ri = lax.broadcasted_iota(jnp.int32, (n, n), 0)
        ci = lax.broadcasted_iota(jnp.int32, (n, n), 1)
        same = _idiv(ri, ts) == _idiv(ci, ts)
        triu = jnp.where(jnp.logical_and(same, ri <= ci), 1.0, 0.0).astype(BF16)
        ncn_scr[...] = -_cumsum_lanes(triu, logf_t)
        cos = cos_ref[...]
        sin = sin_ref[...]
        zrq = _mm(h, w_ref[:, OFF_RQ:OFF_RQ + W_RET_QK])
        zrk = _mm(h, w_ref[:, OFF_RK:OFF_RK + W_RET_QK])
        for hh in range(H_RET):
            cols = slice(hh * DK_RET, (hh + 1) * DK_RET)
            rq_scr[:, cols] = _rope(zrq[:, cols], cos, sin)
            rkt_scr[cols, :] = (_rope(zrk[:, cols], cos, sin) * (DK_RET ** -0.5)).T
        rv_scr[...] = _mm(h, w_ref[:, OFF_RV:OFF_RV + W_RET_V]).astype(BF16)
        zmq = _mm(h, w_ref[:, OFF_MQ:OFF_MQ + W_MEM])
        for hh in range(H_MEM):
            cols = slice(hh * DH_MEM, (hh + 1) * DH_MEM)
            mq_scr[:, cols] = _rms_rows(zmq[:, cols], gmq_ref[...])

    nr = H_FOX * ts
    fq_b = fq_scr[pl.ds(r0, ts), :]
    row_head = _idiv(lax.broadcasted_iota(jnp.int32, (nr, W_FOX), 0), ts)
    col_head = _idiv(lax.broadcasted_iota(jnp.int32, (nr, W_FOX), 1), DH_FOX)
    head_sel = row_head == col_head
    q_bd = jnp.where(head_sel, jnp.concatenate([fq_b] * H_FOX, axis=0), 0.0).astype(BF16)

    cc = 256
    ri = lax.broadcasted_iota(jnp.int32, (cc, cc), 0)
    ci = lax.broadcasted_iota(jnp.int32, (cc, cc), 1)
    triu = jnp.where(ri <= ci, 1.0, 0.0).astype(BF16)
    carry = jnp.zeros((H_FOX, 1), F32)
    c_chunks = []
    for c in range(past // cc):
        cch = _cumsum_lanes(triu, clf_ref[0, :, c * cc:(c + 1) * cc]) + carry
        carry = cch[:, cc - 1:cc]
        c_chunks.append(cch)
    c_past = jnp.concatenate(c_chunks, axis=1)
    ncp = jnp.concatenate([jnp.broadcast_to(-c_past[hh:hh + 1, :], (ts, past)) for hh in range(H_FOX)], axis=0)
    ncn = ncn_scr[:H_FOX, :] - carry
    ncn = jnp.concatenate([jnp.broadcast_to(ncn[hh:hh + 1, :], (ts, n)) for hh in range(H_FOX)], axis=0)

    s_p = _mm(q_bd, ckt_ref[0].astype(BF16)) + ncp
    s_n = _mm(q_bd, knt_scr[...]) + ncn
    tok = lax.broadcasted_iota(jnp.int32, (nr, n), 1)
    qt = _imod(lax.broadcasted_iota(jnp.int32, (nr, n), 0), ts)
    valid = jnp.logical_and(_idiv(tok, ts) == b, _imod(tok, ts) <= qt)
    s_n = jnp.where(valid, s_n, -jnp.inf)
    m = jnp.maximum(jnp.max(s_p, axis=-1, keepdims=True), jnp.max(s_n, axis=-1, keepdims=True))
    p_p = jnp.exp(s_p - m)
    p_n = jnp.exp(s_n - m)
    l = jnp.sum(p_p, axis=-1, keepdims=True) + jnp.sum(p_n, axis=-1, keepdims=True)
    o_past = _mm(cvt_ref[0].astype(BF16), p_p.T.astype(BF16)).T
    o_bd = (o_past + _mm(p_n.astype(BF16), vn_scr[...])) / l
    o_bd = jnp.where(head_sel, o_bd, 0.0)
    fo = o_bd[0:ts, :]
    for hh in range(1, H_FOX):
        fo = fo + o_bd[hh * ts:(hh + 1) * ts, :]
    fo_scr[pl.ds(r0, ts), :] = fo

    tokr = lax.broadcasted_iota(jnp.int32, (ts, n), 1)
    qtr = lax.broadcasted_iota(jnp.int32, (ts, n), 0)
    in_b = _idiv(tokr, ts) == b
    dt = (qtr - _imod(tokr, ts)).astype(F32)
    tokc = lax.broadcasted_iota(jnp.int32, (1, n), 1)
    kpos = _imod(tokc, ts).astype(F32)
    rowi = lax.broadcasted_iota(jnp.int32, (ts, 1), 0).astype(F32)
    r_parts = []
    for hh in range(H_RET):
        lg = lg_ref[hh]
        q = rq_scr[pl.ds(r0, ts), hh * DK_RET:(hh + 1) * DK_RET].astype(BF16)
        kt = rkt_scr[hh * DK_RET:(hh + 1) * DK_RET, :]
        v = rv_scr[:, hh * DV_RET:(hh + 1) * DV_RET]
        state = st0_ref[0, hh]
        dmask = jnp.where(jnp.logical_and(in_b, dt >= 0.0), jnp.exp(lg * jnp.maximum(dt, 0.0)), 0.0)
        inter = _mm(q, state.astype(BF16)) * jnp.exp(lg * (rowi + 1.0))
        scores = _mm(q, kt.astype(BF16)) * dmask
        r_parts.append(inter + _mm(scores.astype(BF16), v))
        deck = jnp.where(_idiv(tokc, ts) == b,jnp.exp(lg * (ts - 1.0 - kpos)), 0.0)
        st_ref[0, hh] = jnp.exp(jnp.full((1, DV_RET), lg * ts, F32)) * state + _mm((kt * deck).astype(BF16), v)
    ro_scr[pl.ds(r0, ts), :] = jnp.concatenate(r_parts, axis=1)

    m_parts = []
    for hh in range(H_MEM):
        cols = slice(hh * DH_MEM, (hh + 1) * DH_MEM)
        q = mq_scr[pl.ds(r0, ts), cols].astype(BF16)
        rows = pl.ds(hh, N_MEM, stride=H_MEM)
        s = _mm_nt(q, cmk_ref[0, rows, :].astype(BF16)) * (DH_MEM ** -0.5)
        p = jnp.exp(s - jnp.max(s, axis=-1, keepdims=True))
        l = jnp.sum(p, axis=-1, keepdims=True)
        m_parts.append(_mm(p.astype(BF16), cmv_ref[0, rows, :].astype(BF16)) / l)
    mo_scr[pl.ds(r0, ts), :] = jnp.concatenate(m_parts, axis=1)

    @pl.when(b == nb - 1)
    def _output_all():
        h = h_scr[...]
        a_in = (fo_scr[...] * _silu(_mm(h, w_ref[:, OFF_FG:OFF_FG + W_FOX]))).astype(BF16)
        merged = _gate(h, w_ref, bm_ref, 0) * _mm(a_in, wpf_ref[...])
        zrg = _mm(h, w_ref[:, OFF_RG:OFF_RG + W_RET_V])
        ro = ro_scr[...]
        r_parts = []
        for hh in range(H_RET):
            cols = slice(hh * DV_RET, (hh + 1) * DV_RET)
            r_parts.append(_group_norm(ro[:, cols]) * _silu(zrg[:, cols]))
        r_in = jnp.concatenate(r_parts, axis=1).astype(BF16)
        merged = merged + _gate(h, w_ref, bm_ref, 1) * _mm(r_in, wpr_ref[...])
        m_in = (mo_scr[...] * _silu(_mm(h, w_ref[:, OFF_MG:OFF_MG + W_MEM]))).astype(BF16)
        merged = merged + _gate(h, w_ref, bm_ref, 2) * _mm(m_in, wpm_ref[...])
        y_ref[...] = x_ref[...] + _mm(merged.astype(BF16), wo_ref[...])


def _sample_call(x2d, cos, sin, ckt, cvt, clf_t, st0, cmk, cmv, wf_t, bf_col, log_g, g_norm, b_merge, gfq, gfk, gmq, seg,
                 w_main, wpf, wpr, wpm, wo, nb, ts):
    n = nb * ts
    past = ckt.shape[2]
    kernel = functools.partial(_sample_kernel, nb=nb, ts=ts, past=past)

    def res(shape):
        nd = len(shape)
        return pl.BlockSpec(shape, lambda b: (0,) * nd, pipeline_mode=pl.Buffered(1))

    in_specs = [
        pl.BlockSpec(memory_space=pltpu.SMEM),
        res((n, D_MODEL)),
        res((n, DK_RET)),
        res((n, DK_RET)),
        pl.BlockSpec((1, W_FOX, past), lambda b: (b, 0, 0)),
        pl.BlockSpec((1, W_FOX, past), lambda b: (b, 0, 0)),
        pl.BlockSpec((1, H_FOX, past), lambda b: (b, 0, 0)),
        pl.BlockSpec((1, H_RET, DK_RET, DV_RET), lambda b: (b, 0, 0, 0)),
        pl.BlockSpec((1, N_MEM * H_MEM, DH_MEM), lambda b: (b, 0, 0)),
        pl.BlockSpec((1, N_MEM * H_MEM, DH_MEM), lambda b: (b, 0, 0)),
        res((1, D_MODEL)),
        res((F_ROWS, 1)),
        res((1, N_BRANCH * D_MODEL)),
        res((1, W_FOX)),
        res((1, W_FOX)),
        res((1, DH_MEM)),
        res((MXU_TILE, MXU_TILE)),
        res((D_MODEL, W_MAIN)),
        res((W_FOX, D_MODEL)),
        res((W_RET_V, D_MODEL)),
        res((W_MEM, D_MODEL)),
        res((D_MODEL, D_MODEL)),
        res((F_ROWS, D_MODEL)),
    ]
    full = lambda shape: pl.BlockSpec(shape, lambda b: (0,) * len(shape))
    out_specs = [
        full((n, D_MODEL)),
        full((n, W_FOX)),
        full((n, W_FOX)),
        full((H_FOX, n)),
        pl.BlockSpec((1, H_RET, DK_RET, DV_RET), lambda b: (b, 0, 0, 0)),
    ]
    out_shape = [
        jax.ShapeDtypeStruct((n, D_MODEL), F32),
        jax.ShapeDtypeStruct((n, W_FOX), F32),
        jax.ShapeDtypeStruct((n, W_FOX), F32),
        jax.ShapeDtypeStruct((H_FOX, n), F32),
        jax.ShapeDtypeStruct((nb, H_RET, DK_RET, DV_RET), F32),
    ]
    scratch = [
        pltpu.VMEM((n, D_MODEL), BF16),
        pltpu.VMEM((n, W_FOX), F32),
        pltpu.VMEM((W_FOX, n), BF16),
        pltpu.VMEM((n, W_FOX), BF16),
        pltpu.VMEM((F_ROWS, n), F32),
        pltpu.VMEM((n, W_RET_QK), F32),
        pltpu.VMEM((W_RET_QK, n), F32),
        pltpu.VMEM((n, W_RET_V), BF16),
        pltpu.VMEM((n, W_MEM), F32),
        pltpu.VMEM((n, W_FOX), F32),
        pltpu.VMEM((n, W_RET_V), F32),
        pltpu.VMEM((n, W_MEM), F32),
    ]
    return pl.pallas_call(
        kernel,
        grid=(nb,),
        in_specs=in_specs,
        out_specs=out_specs,
        out_shape=out_shape,
        scratch_shapes=scratch,
        compiler_params=pltpu.CompilerParams(
            dimension_semantics=("arbitrary",), vmem_limit_bytes=VMEM_LIMIT_BYTES),
        name="sample",
    )(log_g, x2d, cos, sin, ckt, cvt, clf_t, st0, cmk, cmv, g_norm, bf_col, b_merge, gfq, gfk, gmq, seg,
      w_main, wpf, wpr, wpm, wo, wf_t)


def _rope_tables(pos):
    half = DK_RET // 2
    inv = ROPE_BASE ** (-jnp.arange(half, dtype=F32) / half)
    ang = pos.astype(F32)[:, None] * inv[None, :]
    cos = jnp.cos(ang)
    sin = jnp.sin(ang)
    return jnp.concatenate([cos, cos], axis=1), jnp.concatenate([-sin, sin], axis=1)


def _reorder_w_in(w_t):
    assert (3 * W_FOX) % REORDER_ROWS == 0 and W_MAIN % REORDER_ROWS == 0 and REORDER_ROWS % H_FOX == 0
    assert w_t.shape == (W_MAIN + H_FOX, D_MODEL)
    return pl.pallas_call(
        _reorder_kernel,
        grid=(W_MAIN // REORDER_ROWS,),
        in_specs=[
            pl.BlockSpec((REORDER_ROWS, D_MODEL), lambda c: (c, 0)),
            pl.BlockSpec((H_FOX, D_MODEL), lambda c: ((c + 1) * (REORDER_ROWS // H_FOX), 0)),
        ],
        out_specs=pl.BlockSpec((D_MODEL, REORDER_ROWS), lambda c: (0, c)),
        out_shape=jax.ShapeDtypeStruct((D_MODEL, W_MAIN), BF16),
        compiler_params=pltpu.CompilerParams(dimension_semantics=("arbitrary",)),
        name="reorder_w_in",
    )(w_t, w_t)


def _reorder_kernel(a_ref, b_ref, o_ref):
    c = pl.program_id(0)
    a = a_ref[...]
    shifted = jnp.concatenate([a[H_FOX:], b_ref[...]], axis=0)
    src = jnp.where(c < (3 * W_FOX) // REORDER_ROWS, a, shifted)
    o_ref[...] = src.T.astype(BF16)


def kernel(x_prompt, x_sample, mem_prompt, cache_fox_k, cache_fox_v, cache_fox_logf, state_ret, cache_mem_k, cache_mem_v,
           g_norm, g_mem_norm, w_in, b_f, b_merge, g_fox_q, g_fox_k, g_mem_q, g_mem_k, w_mem_kv,
           w_p_fox, w_p_ret, w_p_mem, w_out):
    depth = w_in.shape[0]
    assert depth == 1, "single-layer kernel"
    B, T, _ = x_prompt.shape
    nb, ts, _ = x_sample.shape
    past = cache_fox_k.shape[2]

    log_g = jnp.log1p(-jnp.exp2(-5.0 - jnp.arange(H_RET, dtype=F32)))
    head_of_lane = jnp.arange(MXU_TILE) // DH_FOX
    seg = (head_of_lane[:, None] == head_of_lane[None, :]).astype(BF16)
    w_t = jnp.swapaxes(w_in[0], 0, 1)
    w_main = _reorder_w_in(w_t)
    gn = g_norm[0].reshape(1, D_MODEL)
    bm = b_merge[0].reshape(1, N_BRANCH * D_MODEL)
    gfq = jnp.tile(g_fox_q[0], H_FOX).reshape(1, W_FOX)
    gfk = jnp.tile(g_fox_k[0], H_FOX).reshape(1, W_FOX)
    gmq = g_mem_q[0].reshape(1, DH_MEM)
    wpf = w_p_fox[0].astype(BF16)
    wpr = w_p_ret[0].astype(BF16)
    wpm = w_p_mem[0].astype(BF16)
    wo = w_out[0].astype(BF16)
    shared = (gn, bm, gfq, gfk, gmq, seg, w_main, wpf, wpr, wpm, wo)

    mk, mv, mkt, mvb = _memkv_call(mem_prompt, g_mem_norm[0], w_mem_kv[0], g_mem_k[0])

    cos_p, sin_p = _rope_tables(jnp.arange(T, dtype=jnp.int32))
    wqkvf_t = jnp.concatenate(
        [w_t[:3 * W_FOX + H_FOX], jnp.zeros((F_ROWS - H_FOX, D_MODEL), F32)], axis=0).astype(BF16)
    bf_col = jnp.concatenate([b_f[0], jnp.zeros((F_ROWS - H_FOX,), F32)]).reshape(F_ROWS, 1)
    y_p, fkt_p, fvt_p, lft_p, st_p = _prompt_call(
        x_prompt, cos_p, sin_p, mkt, mvb, wqkvf_t, bf_col, gfq.reshape(W_FOX, 1), gfk.reshape(W_FOX, 1), log_g,
        gn, bm, gmq, w_main, wpf, wpr, wpm, wo)

    def time_major(a_t):
        return jnp.transpose(a_t.reshape(a_t.shape[0], H_FOX, DH_FOX, a_t.shape[2]), (0, 3, 1, 2))[None]

    def time_minor(a):
        return jnp.transpose(a, (0, 2, 3, 1)).reshape(a.shape[0], W_FOX, a.shape[1])

    fk_p = time_major(fkt_p)
    fv_p = time_major(fvt_p)
    lf_p = jnp.swapaxes(lft_p, 1, 2)[None]

    pos_s = past + jnp.arange(ts, dtype=jnp.int32)
    cos_s, sin_s = _rope_tables(jnp.tile(pos_s, nb))
    y_s, fk_s, fv_s, lft_s, st_s = _sample_call(
        x_sample.reshape(nb * ts, D_MODEL), cos_s, sin_s,
        time_minor(cache_fox_k[0]), time_minor(cache_fox_v[0]),
        jnp.swapaxes(cache_fox_logf[0], 1, 2), state_ret[0],
        cache_mem_k[0].reshape(nb, N_MEM * H_MEM, DH_MEM), cache_mem_v[0].reshape(nb, N_MEM * H_MEM, DH_MEM),
        wqkvf_t[3 * W_FOX:], bf_col, log_g, *shared, nb=nb, ts=ts)
    lf_s = jnp.transpose(lft_s.reshape(H_FOX, nb, ts), (1, 2, 0))[None]

    return (y_p, y_s.reshape(nb, ts, D_MODEL),
            fk_p, fv_p, lf_p,
            st_p[None], mk.reshape(1, B, N_MEM, H_MEM, DH_MEM), mv.reshape(1, B, N_MEM, H_MEM, DH_MEM),
            fk_s.reshape(1, nb, ts, H_FOX, DH_FOX), fv_s.reshape(1, nb, ts, H_FOX, DH_FOX),
            lf_s, st_s[None])
```

```python
import functools

import jax
import jax.numpy as jnp
from jax import lax
from jax.experimental import pallas as pl
from jax.experimental.pallas import tpu as pltpu

D_MODEL = 1024
N_MEM = 256
H_FOX = 8
DH_FOX = 64
H_RET = 4
DK_RET = 128
DV_RET = 256
H_MEM = 4
DH_MEM = 128
W_FOX = H_FOX * DH_FOX
W_RET_QK = H_RET * DK_RET
W_RET_V = H_RET * DV_RET
W_MEM = H_MEM * DH_MEM
N_BRANCH = 3
ROPE_BASE = 10000.0
EPS = 1e-6
LOG2E = 1.4426950408889634

LANES = 128
MXU_TILE = 256
BF16_SUBLANES = 16
V_ROWS = DH_FOX + BF16_SUBLANES
F_ROWS = BF16_SUBLANES

OFF_FQ = 0
OFF_FK = OFF_FQ + W_FOX
OFF_FV = OFF_FK + W_FOX
OFF_FG = OFF_FV + W_FOX
OFF_RQ = OFF_FG + W_FOX
OFF_RK = OFF_RQ + W_RET_QK
OFF_RV = OFF_RK + W_RET_QK
OFF_RG = OFF_RV + W_RET_V
OFF_MQ = OFF_RG + W_RET_V
OFF_MG = OFF_MQ + W_MEM
OFF_GL = OFF_MG + W_MEM
W_MAIN = OFF_GL + N_BRANCH * D_MODEL
REORDER_ROWS = 512

PROMPT_BLOCK = 256
VMEM_LIMIT_BYTES = 60 * 1024 * 1024

F32 = jnp.float32
BF16 = jnp.bfloat16


def _mm(a, b):
    return jnp.dot(a, b, preferred_element_type=F32)


def _mm_nt(a, b):
    return lax.dot_general(a, b, (((1,), (1,)), ((), ())), preferred_element_type=F32)


def _idiv(x, d):
    assert d & (d - 1) == 0
    return lax.shift_right_logical(x, d.bit_length() - 1)


def _imod(x, d):
    assert d & (d - 1) == 0
    return jnp.bitwise_and(x, d - 1)


def _rms_rows(x, g):
    ms = jnp.mean(x * x, axis=-1, keepdims=True)
    return x * lax.rsqrt(ms + EPS) * g


def _head_sumsq(z, seg_ref):
    sq = (z * z).astype(BF16)
    return jnp.concatenate(
        [_mm(sq[:, c:c + MXU_TILE], seg_ref[...]) for c in range(0, W_FOX, MXU_TILE)], axis=1)


def _head_norm64(z, g, seg_ref):
    return z * lax.rsqrt(_head_sumsq(z, seg_ref) * (1.0 / DH_FOX) + EPS) * g


def _log_sigmoid(u):
    return jnp.minimum(u, 0.0) - jnp.log1p(jnp.exp(-jnp.abs(u)))


def _silu(u):
    return u * jax.nn.sigmoid(u)


def _split3(a):
    hi = a.astype(BF16)
    r1 = a - hi.astype(F32)
    mid = r1.astype(BF16)
    lo = (r1 - mid.astype(F32)).astype(BF16)
    return hi, mid, lo


def _cumsum_lanes(triu, a):
    hi, mid, lo = _split3(a)
    return _mm(hi, triu) + _mm(mid, triu) + _mm(lo, triu)


def _rope(x, cos, sin_signed):
    return x * cos + pltpu.roll(x, DK_RET // 2, 1) * sin_signed


def _group_norm(o):
    mu = jnp.mean(o, axis=-1, keepdims=True)
    d = o - mu
    var = jnp.mean(d * d, axis=-1, keepdims=True)
    return d * lax.rsqrt(var + EPS)


def _gate(h, w_ref, bm_ref, i):
    z = _mm(h, w_ref[:, OFF_GL + i * D_MODEL:OFF_GL + (i + 1) * D_MODEL])
    return jax.nn.sigmoid(z + bm_ref[:, i * D_MODEL:(i + 1) * D_MODEL])


def _memkv_kernel(mem_ref, gmn_ref, w_ref, gmk_ref, mk_ref, mv_ref, mkt_ref, mvb_ref):
    h = _rms_rows(mem_ref[0], gmn_ref[...]).astype(BF16)
    z = _mm(h, w_ref[...])
    ks = []
    for hh in range(H_MEM):
        ks.append(_rms_rows(z[:, hh * DH_MEM:(hh + 1) * DH_MEM], gmk_ref[...]))
    mk = jnp.concatenate(ks, axis=1)
    mv = z[:, W_MEM:]
    for hh in range(H_MEM):
        rows = pl.ds(hh, N_MEM, stride=H_MEM)
        mk_ref[0, rows, :] = ks[hh]
        mv_ref[0, rows, :] = mv[:, hh * DH_MEM:(hh + 1) * DH_MEM]
    mkt_ref[0] = mk.T.astype(BF16)
    mvb_ref[0] = mv.astype(BF16)


def _memkv_call(mem, g_mem_norm, w_mem_kv, g_mem_k):
    B = mem.shape[0]
    const = lambda b: (0, 0)
    per_b = lambda b: (b, 0, 0)
    return pl.pallas_call(
        _memkv_kernel,
        grid=(B,),
        in_specs=[
            pl.BlockSpec((1, N_MEM, D_MODEL), per_b),
            pl.BlockSpec((1, D_MODEL), const),
            pl.BlockSpec((D_MODEL, 2 * W_MEM), const),
            pl.BlockSpec((1, DH_MEM), const),
        ],
        out_specs=[
            pl.BlockSpec((1, N_MEM * H_MEM, DH_MEM), per_b),
            pl.BlockSpec((1, N_MEM * H_MEM, DH_MEM), per_b),
            pl.BlockSpec((1, W_MEM, N_MEM), per_b),
            pl.BlockSpec((1, N_MEM, W_MEM), per_b),
        ],
        out_shape=[
            jax.ShapeDtypeStruct((B, N_MEM * H_MEM, DH_MEM), F32),
            jax.ShapeDtypeStruct((B, N_MEM * H_MEM, DH_MEM), F32),
            jax.ShapeDtypeStruct((B, W_MEM, N_MEM), BF16),
            jax.ShapeDtypeStruct((B, N_MEM, W_MEM), BF16),
        ],
        compiler_params=pltpu.CompilerParams(dimension_semantics=("arbitrary",)),
        name="memkv",
    )(mem, g_mem_norm.reshape(1, D_MODEL), w_mem_kv.astype(BF16), g_mem_k.reshape(1, DH_MEM))


def _prompt_kernel(lg_ref, x_ref, cos_ref, sin_ref, mkt_ref, mvb_ref, gn_ref, bf_ref, bm_ref, gfq_ref, gfk_ref,
                   gmq_ref, w_ref, wpf_ref, wpr_ref, wpm_ref, wo_ref, wqkvf_ref,
                   y_ref, fkt_ref, fvt_ref, lft_ref, st_ref,
                   k_scr, vt_scr, qt_scr, m_scr, acc_scr, sa_scr, sb_scr, ccar_scr, dmask_scr, decq_scr, deck_scr,
                   ssc_scr, *, tb):
    b = pl.program_id(0)
    j = pl.program_id(1)
    t0 = pl.multiple_of(j * tb, tb)

    @pl.when(jnp.logical_and(b == 0, j == 0))
    def _init_tables():
        ri = lax.broadcasted_iota(jnp.int32, (tb, tb), 0)
        ci = lax.broadcasted_iota(jnp.int32, (tb, tb), 1)
        diff = (ri - ci).astype(F32)
        rowi = lax.broadcasted_iota(jnp.int32, (tb, 1), 0).astype(F32)
        coli = lax.broadcasted_iota(jnp.int32, (1, tb), 1).astype(F32)
        for hh in range(H_RET):
            lg = lg_ref[hh]
            dmask_scr[hh] = jnp.where(diff >= 0.0, jnp.exp(lg * jnp.maximum(diff, 0.0)), 0.0)
            decq_scr[hh] = jnp.exp(lg * (rowi + 1.0))
            deck_scr[hh] = jnp.exp(lg * (tb - 1.0 - coli))
            ssc_scr[hh] = jnp.exp(jnp.full((1, DV_RET), lg * tb, F32))

    @pl.when(j == 0)
    def _init_batch():
        ccar_scr[...] = jnp.zeros_like(ccar_scr)
        st_ref[...] = jnp.zeros_like(st_ref)

    h = _rms_rows(x_ref[0], gn_ref[...]).astype(BF16)

    def proj(off, width):
        return _mm(h, w_ref[:, off:off + width])

    def gate(i):
        return jax.nn.sigmoid(proj(OFF_GL + i * D_MODEL, D_MODEL) + bm_ref[:, i * D_MODEL:(i + 1) * D_MODEL])


    half = (3 * W_FOX + F_ROWS) // (2 * BF16_SUBLANES) * BF16_SUBLANES
    z_t = jnp.concatenate([_mm_nt(wqkvf_ref[:half, :], h), _mm_nt(wqkvf_ref[half:, :], h)], axis=0)
    zrq = proj(OFF_RQ, W_RET_QK)
    zrk = proj(OFF_RK, W_RET_QK)

    def head_norm_t(zh_t, g_col):
        parts = []
        for hh in range(H_FOX):
            zz = zh_t[hh * DH_FOX:(hh + 1) * DH_FOX]
            ms = jnp.sum(zz * zz, axis=0, keepdims=True) * (1.0 / DH_FOX)
            parts.append(zz * lax.rsqrt(ms + EPS))
        return jnp.concatenate(parts, axis=0) * g_col

    fq_t = head_norm_t(z_t[:W_FOX], gfq_ref[...]) * (DH_FOX ** -0.5 * LOG2E)
    fk_t = head_norm_t(z_t[W_FOX:2 * W_FOX], gfk_ref[...])
    fv_t = z_t[2 * W_FOX:3 * W_FOX]
    logf_t = _log_sigmoid(z_t[3 * W_FOX:] + bf_ref[...])
    ri = lax.broadcasted_iota(jnp.int32, (tb, tb), 0)
    ci = lax.broadcasted_iota(jnp.int32, (tb, tb), 1)
    key_le_query = ri <= ci
    c_t = _cumsum_lanes(jnp.where(key_le_query, 1.0, 0.0).astype(BF16), logf_t) + ccar_scr[...]
    ccar_scr[...] = c_t[:, tb - 1:tb]
    rv = proj(OFF_RV, W_RET_V).astype(BF16)
    fk = fk_t.T
    fkt_ref[0] = fk_t
    fvt_ref[0] = fv_t
    lft_ref[0] = logf_t[:H_FOX]
    hi, mid, lo = _split3(c_t[:H_FOX] * (-LOG2E))
    parts_t = jnp.concatenate(
        [hi.astype(F32), mid.astype(F32), lo.astype(F32), jnp.zeros((LANES - 3 * H_FOX, tb), F32)], axis=0)
    bias = pltpu.roll(parts_t.T, DH_FOX, 1)
    lane = lax.broadcasted_iota(jnp.int32, (tb, LANES), 1)
    vrow = lax.broadcasted_iota(jnp.int32, (V_ROWS - DH_FOX, tb), 0)
    ones_row = jnp.where(vrow == 0, 1.0, 0.0).astype(BF16)
    fv_tb = fv_t.astype(BF16)
    for hh in range(H_FOX):
        vt_scr[hh, :, pl.ds(t0, tb)] = jnp.concatenate([fv_tb[hh * DH_FOX:(hh + 1) * DH_FOX, :], ones_row], axis=0)
    qrow = lax.broadcasted_iota(jnp.int32, (LANES - DH_FOX, tb), 0)
    for hh in range(H_FOX):
        pair = fk[:, (hh // 2) * LANES:(hh // 2 + 1) * LANES]
        if hh % 2:
            pair = pltpu.roll(pair, DH_FOX, 1)
        k_scr[hh, pl.ds(t0, tb), :] = jnp.where(lane < DH_FOX, pair, bias).astype(BF16)
        ones = jnp.where(
            jnp.logical_or(qrow == hh, jnp.logical_or(qrow == H_FOX + hh, qrow == 2 * H_FOX + hh)), 1.0, 0.0)
        qt_scr[hh] = jnp.concatenate([fq_t[hh * DH_FOX:(hh + 1) * DH_FOX, :], ones], axis=0).astype(BF16)

    zrg = proj(OFF_RG, W_RET_V)
    for hh in range(H_FOX):
        m_scr[hh] = jnp.full((1, tb), -jnp.inf, F32)
        acc_scr[hh] = jnp.zeros((V_ROWS, tb), F32)

    def scores(buf, kb, hh):
        buf[hh] = _mm(k_scr[hh, kb * tb:(kb + 1) * tb, :], qt_scr[hh])

    def softmax(buf, hh, masked):
        s_t = buf[hh]
        if masked:
            keys = lax.broadcasted_iota(jnp.int32, (tb, tb), 0)
            queries = lax.broadcasted_iota(jnp.int32, (tb, tb), 1)
            s_t = jnp.where(keys <= queries, s_t, -jnp.inf)
        m_old = m_scr[hh]
        m_new = jnp.maximum(m_old, jnp.max(s_t, axis=0, keepdims=True))
        m_scr[hh] = m_new
        return hh, jnp.exp2((s_t - m_new).astype(BF16)), jnp.exp2(m_old - m_new)

    def accumulate(kb, pending):
        hh, p_t, alpha = pending
        acc_scr[hh] = alpha * acc_scr[hh] + _mm(vt_scr[hh, :, kb * tb:(kb + 1) * tb], p_t)

    def absorb(buf, kb, also, masked):
        pending = None
        for hh in range(H_FOX):
            cur = softmax(buf, hh, masked)
            if also is not None:
                also(hh)
            if pending is not None:
                accumulate(kb, pending)
            pending = cur
        accumulate(kb, pending)

    def attend(own):
        bufs = (sa_scr, sb_scr)
        for hh in range(H_FOX):
            scores(bufs[0], 0, hh)
        for kb in range(own + 1):
            nxt = None
            if kb < own:
                nxt = functools.partial(scores, bufs[(kb + 1) % 2], kb + 1)
            absorb(bufs[kb % 2], kb, nxt, masked=(kb == own))

    for own in range(k_scr.shape[1] // tb):
        pl.when(j == own)(functools.partial(attend, own))

    zfg = proj(OFF_FG, W_FOX)
    gate0 = gate(0)
    fo = jnp.concatenate(
        [acc_scr[hh, :DH_FOX, :] / acc_scr[hh, DH_FOX:DH_FOX + 1, :] for hh in range(H_FOX)], axis=0).T
    a_in = (fo * _silu(zfg)).astype(BF16)

    cos = cos_ref[...]
    sin = sin_ref[...]
    rq, rkt, rvs, states = [], [], [], []
    for hh in range(H_RET):
        cols = slice(hh * DK_RET, (hh + 1) * DK_RET)
        rq.append(_rope(zrq[:, cols], cos, sin).astype(BF16))
        rkt.append((_rope(zrk[:, cols], cos, sin) * (DK_RET ** -0.5)).T)
        rvs.append(rv[:, hh * DV_RET:(hh + 1) * DV_RET])
        states.append(st_ref[0, hh])
    inter = [_mm(rq[hh], states[hh].astype(BF16)) for hh in range(H_RET)]
    scores = [_mm(rq[hh], rkt[hh].astype(BF16)) for hh in range(H_RET)]
    zmq = proj(OFF_MQ, W_MEM)
    merged = gate0 * _mm(a_in, wpf_ref[...])
    ret_o = [inter[hh] * decq_scr[hh] + _mm((scores[hh] * dmask_scr[hh]).astype(BF16), rvs[hh])
             for hh in range(H_RET)]
    for hh in range(H_RET):
        st_ref[0, hh] = ssc_scr[hh] * states[hh] + _mm((rkt[hh] * deck_scr[hh]).astype(BF16), rvs[hh])
    gate1 = gate(1)
    zmg = proj(OFF_MG, W_MEM)

    mem_s = []
    for hh in range(H_MEM):
        cols = slice(hh * DH_MEM, (hh + 1) * DH_MEM)
        q = _rms_rows(zmq[:, cols], gmq_ref[...]).astype(BF16)
        mem_s.append(_mm(q, mkt_ref[0, cols, :]) * (DH_MEM ** -0.5))
    r_in = jnp.concatenate(
        [_group_norm(ret_o[hh]) * _silu(zrg[:, hh * DV_RET:(hh + 1) * DV_RET]) for hh in range(H_RET)],
        axis=1).astype(BF16)
    merged = merged + gate1 * _mm(r_in, wpr_ref[...])
    m_parts = []
    for hh in range(H_MEM):
        cols = slice(hh * DH_MEM, (hh + 1) * DH_MEM)
        p = jnp.exp(mem_s[hh] - jnp.max(mem_s[hh], axis=-1, keepdims=True))
        l = jnp.sum(p, axis=-1, keepdims=True)
        m_parts.append(_mm(p.astype(BF16), mvb_ref[0, :, cols]) / l)
    gate2 = gate(2)
    m_in = (jnp.concatenate(m_parts, axis=1) * _silu(zmg)).astype(BF16)
    merged = merged + gate2 * _mm(m_in, wpm_ref[...])

    y_ref[0] = x_ref[0] + _mm(merged.astype(BF16), wo_ref[...])


def _resident(shape):
    nd = len(shape)
    return pl.BlockSpec(shape, lambda b, j: (0,) * nd, pipeline_mode=pl.Buffered(1))


def _prompt_call(x, cos, sin, mkt, mvb, wqkvf_t, bf_col, gfq_col, gfk_col, log_g, g_norm, b_merge, gmq, w_main, wpf,
                 wpr, wpm, wo, tb=PROMPT_BLOCK):
    B, T, _ = x.shape
    nt = T // tb
    kernel = functools.partial(_prompt_kernel, tb=tb)
    in_specs = [
        pl.BlockSpec(memory_space=pltpu.SMEM),
        pl.BlockSpec((1, tb, D_MODEL), lambda b, j: (b, j, 0)),
        pl.BlockSpec((tb, DK_RET), lambda b, j: (j, 0)),
        pl.BlockSpec((tb, DK_RET), lambda b, j: (j, 0)),
        pl.BlockSpec((1, W_MEM, N_MEM), lambda b, j: (b, 0, 0)),
        pl.BlockSpec((1, N_MEM, W_MEM), lambda b, j: (b, 0, 0)),
        _resident((1, D_MODEL)),
        _resident((F_ROWS, 1)),
        _resident((1, N_BRANCH * D_MODEL)),
        _resident((W_FOX, 1)),
        _resident((W_FOX, 1)),
        _resident((1, DH_MEM)),
        _resident((D_MODEL, W_MAIN)),
        _resident((W_FOX, D_MODEL)),
        _resident((W_RET_V, D_MODEL)),
        _resident((W_MEM, D_MODEL)),
        _resident((D_MODEL, D_MODEL)),
        _resident((3 * W_FOX + F_ROWS, D_MODEL)),
    ]
    out_specs = [
        pl.BlockSpec((1, tb, D_MODEL), lambda b, j: (b, j, 0)),
        pl.BlockSpec((1, W_FOX, tb), lambda b, j: (b, 0, j)),
        pl.BlockSpec((1, W_FOX, tb), lambda b, j: (b, 0, j)),
        pl.BlockSpec((1, H_FOX, tb), lambda b, j: (b, 0, j)),
        pl.BlockSpec((1, H_RET, DK_RET, DV_RET), lambda b, j: (b, 0, 0, 0)),
    ]
    out_shape = [
        jax.ShapeDtypeStruct((B, T, D_MODEL), F32),
        jax.ShapeDtypeStruct((B, W_FOX, T), F32),
        jax.ShapeDtypeStruct((B, W_FOX, T), F32),
        jax.ShapeDtypeStruct((B, H_FOX, T), F32),
        jax.ShapeDtypeStruct((B, H_RET, DK_RET, DV_RET), F32),
    ]
    scratch = [
        pltpu.VMEM((H_FOX, T, LANES), BF16),
        pltpu.VMEM((H_FOX, V_ROWS, T), BF16),
        pltpu.VMEM((H_FOX, LANES, tb), BF16),
        pltpu.VMEM((H_FOX, 1, tb), F32),
        pltpu.VMEM((H_FOX, V_ROWS, tb), F32),
        pltpu.VMEM((H_FOX, tb, tb), F32),
        pltpu.VMEM((H_FOX, tb, tb), F32),
        pltpu.VMEM((F_ROWS, 1), F32),
        pltpu.VMEM((H_RET, tb, tb), F32),
        pltpu.VMEM((H_RET, tb, 1), F32),
        pltpu.VMEM((H_RET, 1, tb), F32),
        pltpu.VMEM((H_RET, 1, DV_RET), F32),
    ]
    return pl.pallas_call(
        kernel,
        grid=(B, nt),
        in_specs=in_specs,
        out_specs=out_specs,
        out_shape=out_shape,
        scratch_shapes=scratch,
        compiler_params=pltpu.CompilerParams(
            dimension_semantics=("arbitrary", "arbitrary"), vmem_limit_bytes=VMEM_LIMIT_BYTES),
        name="prompt",
    )(log_g, x, cos, sin, mkt, mvb, g_norm, bf_col, b_merge, gfq_col, gfk_col, gmq, w_main, wpf, wpr, wpm, wo, wqkvf_t)


def _sample_kernel(lg_ref, x_ref, cos_ref, sin_ref, ckt_ref, cvt_ref, clf_ref, st0_ref, cmk_ref, cmv_ref,
                   gn_ref, bf_ref, bm_ref, gfq_ref, gfk_ref, gmq_ref, seg_ref, w_ref, wpf_ref, wpr_ref, wpm_ref,
                   wo_ref, wft_ref,
                   y_ref, fk_ref, fv_ref, lft_ref, st_ref,
                   h_scr, fq_scr, knt_scr, vn_scr, ncn_scr, rq_scr, rkt_scr, rv_scr, mq_scr, fo_scr, ro_scr, mo_scr,
                   *, nb, ts, past):
    b = pl.program_id(0)
    n = nb * ts
    r0 = pl.multiple_of(b * ts, ts)

    @pl.when(b == 0)
    def _project_all():
        h = _rms_rows(x_ref[...], gn_ref[...]).astype(BF16)
        h_scr[...] = h
        fq_scr[...] = _head_norm64(_mm(h, w_ref[:, OFF_FQ:OFF_FQ + W_FOX]), gfq_ref[...], seg_ref) * (DH_FOX ** -0.5)
        fk = _head_norm64(_mm(h, w_ref[:, OFF_FK:OFF_FK + W_FOX]), gfk_ref[...], seg_ref)
        fv = _mm(h, w_ref[:, OFF_FV:OFF_FV + W_FOX])
        logf_t = _log_sigmoid(_mm_nt(wft_ref[...], h) + bf_ref[...])
        fk_ref[...] = fk
        fv_ref[...] = fv
        lft_ref[...] = logf_t[:H_FOX]
        knt_scr[...] = fk.T.astype(BF16)
        vn_scr[...] = fv.astype(BF16)
        ri = lax.broadcasted_iota(jnp.int32, (n, n), 0)
        ci = lax.broadcasted_iota(jnp.int32, (n, n), 1)
        same = _idiv(ri, ts) == _idiv(ci, ts)
        triu = jnp.where(jnp.logical_and(same, ri <= ci), 1.0, 0.0).astype(BF16)
        ncn_scr[...] = -_cumsum_lanes(triu, logf_t)
        cos = cos_ref[...]
        sin = sin_ref[...]
        zrq = _mm(h, w_ref[:, OFF_RQ:OFF_RQ + W_RET_QK])
        zrk = _mm(h, w_ref[:, OFF_RK:OFF_RK + W_RET_QK])
        for hh in range(H_RET):
            cols = slice(hh * DK_RET, (hh + 1) * DK_RET)
            rq_scr[:, cols] = _rope(zrq[:, cols], cos, sin)
            rkt_scr[cols, :] = (_rope(zrk[:, cols], cos, sin) * (DK_RET ** -0.5)).T
        rv_scr[...] = _mm(h, w_ref[:, OFF_RV:OFF_RV + W_RET_V]).astype(BF16)
        zmq = _mm(h, w_ref[:, OFF_MQ:OFF_MQ + W_MEM])
        for hh in range(H_MEM):
            cols = slice(hh * DH_MEM, (hh + 1) * DH_MEM)
            mq_scr[:, cols] = _rms_rows(zmq[:, cols], gmq_ref[...])

    nr = H_FOX * ts
    fq_b = fq_scr[pl.ds(r0, ts), :]
    row_head = _idiv(lax.broadcasted_iota(jnp.int32, (nr, W_FOX), 0), ts)
    col_head = _idiv(lax.broadcasted_iota(jnp.int32, (nr, W_FOX), 1), DH_FOX)
    head_sel = row_head == col_head
    q_bd = jnp.where(head_sel, jnp.concatenate([fq_b] * H_FOX, axis=0), 0.0).astype(BF16)

    cc = 256
    ri = lax.broadcasted_iota(jnp.int32, (cc, cc), 0)
    ci = lax.broadcasted_iota(jnp.int32, (cc, cc), 1)
    triu = jnp.where(ri <= ci, 1.0, 0.0).astype(BF16)
    carry = jnp.zeros((H_FOX, 1), F32)
    c_chunks = []
    for c in range(past // cc):
        cch = _cumsum_lanes(triu, clf_ref[0, :, c * cc:(c + 1) * cc]) + carry
        carry = cch[:, cc - 1:cc]
        c_chunks.append(cch)
    c_past = jnp.concatenate(c_chunks, axis=1)
    ncp = jnp.concatenate([jnp.broadcast_to(-c_past[hh:hh + 1, :], (ts, past)) for hh in range(H_FOX)], axis=0)
    ncn = ncn_scr[:H_FOX, :] - carry
    ncn = jnp.concatenate([jnp.broadcast_to(ncn[hh:hh + 1, :], (ts, n)) for hh in range(H_FOX)], axis=0)

    s_p = _mm(q_bd, ckt_ref[0].astype(BF16)) + ncp
    s_n = _mm(q_bd, knt_scr[...]) + ncn
    tok = lax.broadcasted_iota(jnp.int32, (nr, n), 1)
    qt = _imod(lax.broadcasted_iota(jnp.int32, (nr, n), 0), ts)
    valid = jnp.logical_and(_idiv(tok, ts) == b, _imod(tok, ts) <= qt)
    s_n = jnp.where(valid, s_n, -jnp.inf)
    m = jnp.maximum(jnp.max(s_p, axis=-1, keepdims=True), jnp.max(s_n, axis=-1, keepdims=True))
    p_p = jnp.exp(s_p - m)
    p_n = jnp.exp(s_n - m)
    l = jnp.sum(p_p, axis=-1, keepdims=True) + jnp.sum(p_n, axis=-1, keepdims=True)
    o_past = _mm(cvt_ref[0].astype(BF16), p_p.T.astype(BF16)).T
    o_bd = (o_past + _mm(p_n.astype(BF16), vn_scr[...])) / l
    o_bd = jnp.where(head_sel, o_bd, 0.0)
    fo = o_bd[0:ts, :]
    for hh in range(1, H_FOX):
        fo = fo + o_bd[hh * ts:(hh + 1) * ts, :]
    fo_scr[pl.ds(r0, ts), :] = fo

    tokr = lax.broadcasted_iota(jnp.int32, (ts, n), 1)
    qtr = lax.broadcasted_iota(jnp.int32, (ts, n), 0)
    in_b = _idiv(tokr, ts) == b
    dt = (qtr - _imod(tokr, ts)).astype(F32)
    tokc = lax.broadcasted_iota(jnp.int32, (1, n), 1)
    kpos = _imod(tokc, ts).astype(F32)
    rowi = lax.broadcasted_iota(jnp.int32, (ts, 1), 0).astype(F32)
    r_parts = []
    for hh in range(H_RET):
        lg = lg_ref[hh]
        q = rq_scr[pl.ds(r0, ts), hh * DK_RET:(hh + 1) * DK_RET].astype(BF16)
        kt = rkt_scr[hh * DK_RET:(hh + 1) * DK_RET, :]
        v = rv_scr[:, hh * DV_RET:(hh + 1) * DV_RET]
        state = st0_ref[0, hh]
        dmask = jnp.where(jnp.logical_and(in_b, dt >= 0.0), jnp.exp(lg * jnp.maximum(dt, 0.0)), 0.0)
        inter = _mm(q, state.astype(BF16)) * jnp.exp(lg * (rowi + 1.0))
        scores = _mm(q, kt.astype(BF16)) * dmask
        r_parts.append(inter + _mm(scores.astype(BF16), v))
        deck = jnp.where(_idiv(tokc, ts) == b, jnp.exp(lg * (ts - 1.0 - kpos)), 0.0)
        st_ref[0, hh] = jnp.exp(jnp.full((1, DV_RET), lg * ts, F32)) * state + _mm((kt * deck).astype(BF16), v)
    ro_scr[pl.ds(r0, ts), :] = jnp.concatenate(r_parts, axis=1)

    m_parts = []
    for hh in range(H_MEM):
        cols = slice(hh * DH_MEM, (hh + 1) * DH_MEM)
        q = mq_scr[pl.ds(r0, ts), cols].astype(BF16)
        rows = pl.ds(hh, N_MEM, stride=H_MEM)
        s = _mm_nt(q, cmk_ref[0, rows, :].astype(BF16)) * (DH_MEM ** -0.5)
        p = jnp.exp(s - jnp.max(s, axis=-1, keepdims=True))
        l = jnp.sum(p, axis=-1, keepdims=True)
        m_parts.append(_mm(p.astype(BF16), cmv_ref[0, rows, :].astype(BF16)) / l)
    mo_scr[pl.ds(r0, ts), :] = jnp.concatenate(m_parts, axis=1)

    @pl.when(b == nb - 1)
    def _output_all():
        h = h_scr[...]
        a_in = (fo_scr[...] * _silu(_mm(h, w_ref[:, OFF_FG:OFF_FG + W_FOX]))).astype(BF16)
        merged = _gate(h, w_ref, bm_ref, 0) * _mm(a_in, wpf_ref[...])
        zrg = _mm(h, w_ref[:, OFF_RG:OFF_RG + W_RET_V])
        ro = ro_scr[...]
        r_parts = []
        for hh in range(H_RET):
            cols = slice(hh * DV_RET, (hh + 1) * DV_RET)
            r_parts.append(_group_norm(ro[:, cols]) * _silu(zrg[:, cols]))
        r_in = jnp.concatenate(r_parts, axis=1).astype(BF16)
        merged = merged + _gate(h, w_ref, bm_ref, 1) * _mm(r_in, wpr_ref[...])
        m_in = (mo_scr[...] * _silu(_mm(h, w_ref[:, OFF_MG:OFF_MG + W_MEM]))).astype(BF16)
        merged = merged + _gate(h, w_ref, bm_ref, 2) * _mm(m_in, wpm_ref[...])
        y_ref[...] = x_ref[...] + _mm(merged.astype(BF16), wo_ref[...])


def _sample_call(x2d, cos, sin, ckt, cvt, clf_t, st0, cmk, cmv, wf_t, bf_col, log_g, g_norm, b_merge, gfq, gfk, gmq, seg,
                 w_main, wpf, wpr, wpm, wo, nb, ts):
    n = nb * ts
    past = ckt.shape[2]
    kernel = functools.partial(_sample_kernel, nb=nb, ts=ts, past=past)

    def res(shape):
        nd = len(shape)
        return pl.BlockSpec(shape, lambda b: (0,) * nd, pipeline_mode=pl.Buffered(1))

    in_specs = [
        pl.BlockSpec(memory_space=pltpu.SMEM),
        res((n, D_MODEL)),
        res((n, DK_RET)),
        res((n, DK_RET)),
        pl.BlockSpec((1, W_FOX, past), lambda b: (b, 0, 0)),
        pl.BlockSpec((1, W_FOX, past), lambda b: (b, 0, 0)),
        pl.BlockSpec((1, H_FOX, past), lambda b: (b, 0, 0)),
        pl.BlockSpec((1, H_RET, DK_RET, DV_RET), lambda b: (b, 0, 0, 0)),
        pl.BlockSpec((1, N_MEM * H_MEM, DH_MEM), lambda b: (b, 0, 0)),
        pl.BlockSpec((1, N_MEM * H_MEM, DH_MEM), lambda b: (b, 0, 0)),
        res((1, D_MODEL)),
        res((F_ROWS, 1)),
        res((1, N_BRANCH * D_MODEL)),
        res((1, W_FOX)),
        res((1, W_FOX)),
        res((1, DH_MEM)),
        res((MXU_TILE, MXU_TILE)),
        res((D_MODEL, W_MAIN)),
        res((W_FOX, D_MODEL)),
        res((W_RET_V, D_MODEL)),
        res((W_MEM, D_MODEL)),
        res((D_MODEL, D_MODEL)),
        res((F_ROWS, D_MODEL)),
    ]
    full = lambda shape: pl.BlockSpec(shape, lambda b: (0,) * len(shape))
    out_specs = [
        full((n, D_MODEL)),
        full((n, W_FOX)),
        full((n, W_FOX)),
        full((H_FOX, n)),
        pl.BlockSpec((1, H_RET, DK_RET, DV_RET), lambda b: (b, 0, 0, 0)),
    ]
    out_shape = [
        jax.ShapeDtypeStruct((n, D_MODEL), F32),
        jax.ShapeDtypeStruct((n, W_FOX), F32),
        jax.ShapeDtypeStruct((n, W_FOX), F32),
        jax.ShapeDtypeStruct((H_FOX, n), F32),
        jax.ShapeDtypeStruct((nb, H_RET, DK_RET, DV_RET), F32),
    ]
    scratch = [
        pltpu.VMEM((n, D_MODEL), BF16),
        pltpu.VMEM((n, W_FOX), F32),
        pltpu.VMEM((W_FOX, n), BF16),
        pltpu.VMEM((n, W_FOX), BF16),
        pltpu.VMEM((F_ROWS, n), F32),
        pltpu.VMEM((n, W_RET_QK), F32),
        pltpu.VMEM((W_RET_QK, n), F32),
        pltpu.VMEM((n, W_RET_V), BF16),
        pltpu.VMEM((n, W_MEM), F32),
        pltpu.VMEM((n, W_FOX), F32),
        pltpu.VMEM((n, W_RET_V), F32),
        pltpu.VMEM((n, W_MEM), F32),
    ]
    return pl.pallas_call(
        kernel,
        grid=(nb,),
        in_specs=in_specs,
        out_specs=out_specs,
        out_shape=out_shape,
        scratch_shapes=scratch,
        compiler_params=pltpu.CompilerParams(
            dimension_semantics=("arbitrary",), vmem_limit_bytes=VMEM_LIMIT_BYTES),
        name="sample",
    )(log_g, x2d, cos, sin, ckt, cvt, clf_t, st0, cmk, cmv, g_norm, bf_col, b_merge, gfq, gfk, gmq, seg,
      w_main, wpf, wpr, wpm, wo, wf_t)


def _rope_tables(pos):
    half = DK_RET // 2
    inv = ROPE_BASE ** (-jnp.arange(half, dtype=F32) / half)
    ang = pos.astype(F32)[:, None] * inv[None, :]
    cos = jnp.cos(ang)
    sin = jnp.sin(ang)
    return jnp.concatenate([cos, cos], axis=1), jnp.concatenate([-sin, sin], axis=1)


def _reorder_w_in(w_t):
    assert (3 * W_FOX) % REORDER_ROWS == 0 and W_MAIN % REORDER_ROWS == 0 and REORDER_ROWS % H_FOX == 0
    assert w_t.shape == (W_MAIN + H_FOX, D_MODEL)
    return pl.pallas_call(
        _reorder_kernel,
        grid=(W_MAIN // REORDER_ROWS,),
        in_specs=[
            pl.BlockSpec((REORDER_ROWS, D_MODEL), lambda c: (c, 0)),
            pl.BlockSpec((H_FOX, D_MODEL), lambda c: ((c + 1) * (REORDER_ROWS // H_FOX), 0)),
        ],
        out_specs=pl.BlockSpec((D_MODEL, REORDER_ROWS), lambda c: (0, c)),
        out_shape=jax.ShapeDtypeStruct((D_MODEL, W_MAIN), BF16),
        compiler_params=pltpu.CompilerParams(dimension_semantics=("arbitrary",)),
        name="reorder_w_in",
    )(w_t, w_t)


def _reorder_kernel(a_ref, b_ref, o_ref):
    c = pl.program_id(0)
    a = a_ref[...]
    shifted = jnp.concatenate([a[H_FOX:], b_ref[...]], axis=0)
    src = jnp.where(c < (3 * W_FOX) // REORDER_ROWS, a, shifted)
    o_ref[...] = src.T.astype(BF16)


def kernel(x_prompt, x_sample, mem_prompt, cache_fox_k, cache_fox_v, cache_fox_logf, state_ret, cache_mem_k, cache_mem_v,
           g_norm, g_mem_norm, w_in, b_f, b_merge, g_fox_q, g_fox_k, g_mem_q, g_mem_k, w_mem_kv,
           w_p_fox, w_p_ret, w_p_mem, w_out):
    depth = w_in.shape[0]
    assert depth == 1, "single-layer kernel"
    B, T, _ = x_prompt.shape
    nb, ts, _ = x_sample.shape
    past = cache_fox_k.shape[2]

    log_g = jnp.log1p(-jnp.exp2(-5.0 - jnp.arange(H_RET, dtype=F32)))
    head_of_lane = jnp.arange(MXU_TILE) // DH_FOX
    seg = (head_of_lane[:, None] == head_of_lane[None, :]).astype(BF16)
    w_t = jnp.swapaxes(w_in[0], 0, 1)
    w_main = _reorder_w_in(w_t)
    gn = g_norm[0].reshape(1, D_MODEL)
    bm = b_merge[0].reshape(1, N_BRANCH * D_MODEL)
    gfq = jnp.tile(g_fox_q[0], H_FOX).reshape(1, W_FOX)
    gfk = jnp.tile(g_fox_k[0], H_FOX).reshape(1, W_FOX)
    gmq = g_mem_q[0].reshape(1, DH_MEM)
    wpf = w_p_fox[0].astype(BF16)
    wpr = w_p_ret[0].astype(BF16)
    wpm = w_p_mem[0].astype(BF16)
    wo = w_out[0].astype(BF16)
    shared = (gn, bm, gfq, gfk, gmq, seg, w_main, wpf, wpr, wpm, wo)

    mk, mv, mkt, mvb = _memkv_call(mem_prompt, g_mem_norm[0], w_mem_kv[0], g_mem_k[0])

    cos_p, sin_p = _rope_tables(jnp.arange(T, dtype=jnp.int32))
    wqkvf_t = jnp.concatenate(
        [w_t[:3 * W_FOX + H_FOX], jnp.zeros((F_ROWS - H_FOX, D_MODEL), F32)], axis=0).astype(BF16)
    bf_col = jnp.concatenate([b_f[0], jnp.zeros((F_ROWS - H_FOX,), F32)]).reshape(F_ROWS, 1)
    y_p, fkt_p, fvt_p, lft_p, st_p = _prompt_call(
        x_prompt, cos_p, sin_p, mkt, mvb, wqkvf_t, bf_col, gfq.reshape(W_FOX, 1), gfk.reshape(W_FOX, 1), log_g,
        gn, bm, gmq, w_main, wpf, wpr, wpm, wo)

    def time_major(a_t):
        return jnp.transpose(a_t.reshape(a_t.shape[0], H_FOX, DH_FOX, a_t.shape[2]), (0, 3, 1, 2))[None]

    def time_minor(a):
        return jnp.transpose(a, (0, 2, 3, 1)).reshape(a.shape[0], W_FOX, a.shape[1])

    fk_p = time_major(fkt_p)
    fv_p = time_major(fvt_p)
    lf_p = jnp.swapaxes(lft_p, 1, 2)[None]

    pos_s = past + jnp.arange(ts, dtype=jnp.int32)
    cos_s, sin_s = _rope_tables(jnp.tile(pos_s, nb))
    y_s, fk_s, fv_s, lft_s, st_s = _sample_call(
        x_sample.reshape(nb * ts, D_MODEL), cos_s, sin_s,
        time_minor(cache_fox_k[0]), time_minor(cache_fox_v[0]),
        jnp.swapaxes(cache_fox_logf[0], 1, 2), state_ret[0],
        cache_mem_k[0].reshape(nb, N_MEM * H_MEM, DH_MEM), cache_mem_v[0].reshape(nb, N_MEM * H_MEM, DH_MEM),
        wqkvf_t[3 * W_FOX:], bf_col, log_g, *shared, nb=nb, ts=ts)
    lf_s = jnp.transpose(lft_s.reshape(H_FOX, nb, ts), (1, 2, 0))[None]

    return (y_p, y_s.reshape(nb, ts, D_MODEL),
            fk_p, fv_p, lf_p,
            st_p[None], mk.reshape(1, B, N_MEM, H_MEM, DH_MEM), mv.reshape(1, B, N_MEM, H_MEM, DH_MEM),
            fk_s.reshape(1, nb, ts, H_FOX, DH_FOX), fv_s.reshape(1, nb, ts, H_FOX, DH_FOX),
            lf_s, st_s[None])
```

```python
import functools

import jax
import jax.numpy as jnp
from jax import lax
from jax.experimental import pallas as pl
from jax.experimental.pallas import tpu as pltpu

D_MODEL = 1024
N_MEM = 256
H_FOX = 8
DH_FOX = 64
H_RET = 4
DK_RET = 128
DV_RET = 256
H_MEM = 4
DH_MEM = 128
W_FOX = H_FOX * DH_FOX
W_RET_QK = H_RET * DK_RET
W_RET_V = H_RET * DV_RET
W_MEM = H_MEM * DH_MEM
N_BRANCH = 3
ROPE_BASE = 10000.0
EPS = 1e-6
LOG2E = 1.4426950408889634

LANES = 128
MXU_TILE = 256
BF16_SUBLANES = 16
V_ROWS = DH_FOX + BF16_SUBLANES
F_ROWS = BF16_SUBLANES

OFF_FQ = 0
OFF_FK = OFF_FQ + W_FOX
OFF_FV = OFF_FK + W_FOX
OFF_FG = OFF_FV + W_FOX
OFF_RQ = OFF_FG + W_FOX
OFF_RK = OFF_RQ + W_RET_QK
OFF_RV = OFF_RK + W_RET_QK
OFF_RG = OFF_RV + W_RET_V
OFF_MQ = OFF_RG + W_RET_V
OFF_MG = OFF_MQ + W_MEM
OFF_GL = OFF_MG + W_MEM
W_MAIN = OFF_GL + N_BRANCH * D_MODEL
REORDER_ROWS = 512

PROMPT_BLOCK = 256
VMEM_LIMIT_BYTES = 60 * 1024 * 1024

F32 = jnp.float32
BF16 = jnp.bfloat16


def _mm(a, b):
    return jnp.dot(a, b, preferred_element_type=F32)


def _mm_nt(a, b):
    return lax.dot_general(a, b, (((1,), (1,)), ((), ())), preferred_element_type=F32)


def _idiv(x, d):
    assert d & (d - 1) == 0
    return lax.shift_right_logical(x, d.bit_length() - 1)


def _imod(x, d):
    assert d & (d - 1) == 0
    return jnp.bitwise_and(x, d - 1)


def _rms_rows(x, g):
    ms = jnp.mean(x * x, axis=-1, keepdims=True)
    return x * lax.rsqrt(ms + EPS) * g


def _head_sumsq(z, seg_ref):
    sq = (z * z).astype(BF16)
    return jnp.concatenate(
        [_mm(sq[:, c:c + MXU_TILE], seg_ref[...]) for c in range(0, W_FOX, MXU_TILE)], axis=1)


def _head_norm64(z, g, seg_ref):
    return z * lax.rsqrt(_head_sumsq(z, seg_ref) * (1.0 / DH_FOX) + EPS) * g


def _log_sigmoid(u):
    return jnp.minimum(u, 0.0) - jnp.log1p(jnp.exp(-jnp.abs(u)))


def _silu(u):
    return u * jax.nn.sigmoid(u)


def _split3(a):
    hi = a.astype(BF16)
    r1 = a - hi.astype(F32)
    mid = r1.astype(BF16)
    lo = (r1 - mid.astype(F32)).astype(BF16)
    return hi, mid, lo


def _cumsum_lanes(triu, a):
    hi, mid, lo = _split3(a)
    return _mm(hi, triu) + _mm(mid, triu) + _mm(lo, triu)


def _rope(x, cos, sin_signed):
    return x * cos + pltpu.roll(x, DK_RET // 2, 1) * sin_signed


def _group_norm(o):
    mu = jnp.mean(o, axis=-1, keepdims=True)
    d = o - mu
    var = jnp.mean(d * d, axis=-1, keepdims=True)
    return d * lax.rsqrt(var + EPS)


def _gate(h, w_ref, bm_ref, i):
    z = _mm(h, w_ref[:, OFF_GL + i * D_MODEL:OFF_GL + (i + 1) * D_MODEL])
    return jax.nn.sigmoid(z + bm_ref[:, i * D_MODEL:(i + 1) * D_MODEL])


def _memkv_kernel(mem_ref, gmn_ref, w_ref, gmk_ref, mk_ref, mv_ref, mkt_ref, mvb_ref):
    h = _rms_rows(mem_ref[0], gmn_ref[...]).astype(BF16)
    z = _mm(h, w_ref[...])
    ks = []
    for hh in range(H_MEM):
        ks.append(_rms_rows(z[:, hh * DH_MEM:(hh + 1) * DH_MEM], gmk_ref[...]))
    mk = jnp.concatenate(ks, axis=1)
    mv = z[:, W_MEM:]
    for hh in range(H_MEM):
        rows = pl.ds(hh, N_MEM, stride=H_MEM)
        mk_ref[0, rows, :] = ks[hh]
        mv_ref[0, rows, :] = mv[:, hh * DH_MEM:(hh + 1) * DH_MEM]
    mkt_ref[0] = mk.T.astype(BF16)
    mvb_ref[0] = mv.astype(BF16)


def _memkv_call(mem, g_mem_norm, w_mem_kv, g_mem_k):
    B = mem.shape[0]
    const = lambda b: (0, 0)
    per_b = lambda b: (b, 0, 0)
    return pl.pallas_call(
        _memkv_kernel,
        grid=(B,),
        in_specs=[
            pl.BlockSpec((1, N_MEM, D_MODEL), per_b),
            pl.BlockSpec((1, D_MODEL), const),
            pl.BlockSpec((D_MODEL, 2 * W_MEM), const),
            pl.BlockSpec((1, DH_MEM), const),
        ],
        out_specs=[
            pl.BlockSpec((1, N_MEM * H_MEM, DH_MEM), per_b),
            pl.BlockSpec((1, N_MEM * H_MEM, DH_MEM), per_b),
            pl.BlockSpec((1, W_MEM, N_MEM), per_b),
            pl.BlockSpec((1, N_MEM, W_MEM), per_b),
        ],
        out_shape=[
            jax.ShapeDtypeStruct((B, N_MEM * H_MEM, DH_MEM), F32),
            jax.ShapeDtypeStruct((B, N_MEM * H_MEM, DH_MEM), F32),
            jax.ShapeDtypeStruct((B, W_MEM, N_MEM), BF16),
            jax.ShapeDtypeStruct((B, N_MEM, W_MEM), BF16),
        ],
        compiler_params=pltpu.CompilerParams(dimension_semantics=("arbitrary",)),
        name="memkv",
    )(mem, g_mem_norm.reshape(1, D_MODEL), w_mem_kv.astype(BF16), g_mem_k.reshape(1, DH_MEM))


def _prompt_kernel(lg_ref, x_ref, cos_ref, sin_ref, mkt_ref, mvb_ref, gn_ref, bf_ref, bm_ref, gfq_ref, gfk_ref,
                   gmq_ref, w_ref, wpf_ref, wpr_ref, wpm_ref, wo_ref, wqkvf_ref,
                   y_ref, fkt_ref, fvt_ref, lft_ref, st_ref,
                   k_scr, vt_scr, qt_scr, m_scr, acc_scr, sa_scr, sb_scr, ccar_scr, dmask_scr, decq_scr, deck_scr,
                   ssc_scr, *, tb):
    b = pl.program_id(0)
    j = pl.program_id(1)
    t0 = pl.multiple_of(j * tb, tb)

    @pl.when(jnp.logical_and(b == 0, j == 0))
    def _init_tables():
        ri = lax.broadcasted_iota(jnp.int32, (tb, tb), 0)
        ci = lax.broadcasted_iota(jnp.int32, (tb, tb), 1)
        diff = (ri - ci).astype(F32)
        rowi = lax.broadcasted_iota(jnp.int32, (tb, 1), 0).astype(F32)
        coli = lax.broadcasted_iota(jnp.int32, (1, tb), 1).astype(F32)
        for hh in range(H_RET):
            lg = lg_ref[hh]
            dmask_scr[hh] = jnp.where(diff >= 0.0, jnp.exp(lg * jnp.maximum(diff, 0.0)), 0.0)
            decq_scr[hh] = jnp.exp(lg * (rowi + 1.0))
            deck_scr[hh] = jnp.exp(lg * (tb - 1.0 - coli))
            ssc_scr[hh] = jnp.exp(jnp.full((1, DV_RET), lg * tb, F32))

    @pl.when(j == 0)
    def _init_batch():
        ccar_scr[...] = jnp.zeros_like(ccar_scr)
        st_ref[...] = jnp.zeros_like(st_ref)

    h = _rms_rows(x_ref[0], gn_ref[...]).astype(BF16)

    def proj(off, width):
        return _mm(h, w_ref[:, off:off + width])

    def gate(i):
        return jax.nn.sigmoid(proj(OFF_GL + i * D_MODEL, D_MODEL) + bm_ref[:, i * D_MODEL:(i + 1) * D_MODEL])


    half = (3 * W_FOX + F_ROWS) // (2 * BF16_SUBLANES) * BF16_SUBLANES
    z_t = jnp.concatenate([_mm_nt(wqkvf_ref[:half, :], h), _mm_nt(wqkvf_ref[half:, :], h)], axis=0)
    zrq = proj(OFF_RQ, W_RET_QK)
    zrk = proj(OFF_RK, W_RET_QK)

    def head_norm_t(zh_t, g_col):
        parts = []
        for hh in range(H_FOX):
            zz = zh_t[hh * DH_FOX:(hh + 1) * DH_FOX]
            ms = jnp.sum(zz * zz, axis=0, keepdims=True) * (1.0 / DH_FOX)
            parts.append(zz * lax.rsqrt(ms + EPS))
        return jnp.concatenate(parts, axis=0) * g_col

    fq_t = head_norm_t(z_t[:W_FOX], gfq_ref[...]) * (DH_FOX ** -0.5 * LOG2E)
    fk_t = head_norm_t(z_t[W_FOX:2 * W_FOX], gfk_ref[...])
    fv_t = z_t[2 * W_FOX:3 * W_FOX]
    logf_t = _log_sigmoid(z_t[3 * W_FOX:] + bf_ref[...])
    ri = lax.broadcasted_iota(jnp.int32, (tb, tb), 0)
    ci = lax.broadcasted_iota(jnp.int32, (tb, tb), 1)
    key_le_query = ri <= ci
    c_t = _cumsum_lanes(jnp.where(key_le_query, 1.0, 0.0).astype(BF16), logf_t) + ccar_scr[...]
    ccar_scr[...] = c_t[:, tb - 1:tb]
    rv = proj(OFF_RV, W_RET_V).astype(BF16)
    fk = fk_t.T
    fkt_ref[0] = fk_t
    fvt_ref[0] = fv_t
    lft_ref[0] = logf_t[:H_FOX]
    hi, mid, lo = _split3(c_t[:H_FOX] * (-LOG2E))
    parts_t = jnp.concatenate(
        [hi.astype(F32), mid.astype(F32), lo.astype(F32), jnp.zeros((LANES - 3 * H_FOX, tb), F32)], axis=0)
    bias = pltpu.roll(parts_t.T, DH_FOX, 1)
    lane = lax.broadcasted_iota(jnp.int32, (tb, LANES), 1)
    vrow = lax.broadcasted_iota(jnp.int32, (V_ROWS - DH_FOX, tb), 0)
    ones_row = jnp.where(vrow == 0, 1.0, 0.0).astype(BF16)
    fv_tb = fv_t.astype(BF16)
    for hh in range(H_FOX):
        vt_scr[hh, :, pl.ds(t0, tb)] = jnp.concatenate([fv_tb[hh * DH_FOX:(hh + 1) * DH_FOX, :], ones_row], axis=0)
    qrow = lax.broadcasted_iota(jnp.int32, (LANES - DH_FOX, tb), 0)
    k_blk = []
    qt_blk = []
    for hh in range(H_FOX):
        pair = fk[:, (hh // 2) * LANES:(hh // 2 + 1) * LANES]
        if hh % 2:
            pair = pltpu.roll(pair, DH_FOX, 1)
        k_aug = jnp.where(lane < DH_FOX, pair, bias).astype(BF16)
        k_scr[hh, pl.ds(t0, tb), :] = k_aug
        k_blk.append(k_aug)
        ones = jnp.where(
            jnp.logical_or(qrow == hh, jnp.logical_or(qrow == H_FOX + hh, qrow == 2 * H_FOX + hh)), 1.0, 0.0)
        qt_aug = jnp.concatenate([fq_t[hh * DH_FOX:(hh + 1) * DH_FOX, :], ones], axis=0).astype(BF16)
        qt_scr[hh] = qt_aug
        qt_blk.append(qt_aug)

    s_all = [_mm(k_blk[hh], qt_blk[hh]) for hh in range(H_FOX)]
    zrg = proj(OFF_RG, W_RET_V)
    p_all = []
    for hh in range(H_FOX):
        s_t = jnp.where(key_le_query, s_all[hh], -jnp.inf)
        m = jnp.max(s_t, axis=0, keepdims=True)
        m_scr[hh] = m
        p_all.append(jnp.exp2((s_t - m).astype(BF16)))
    for hh in range(H_FOX):
        acc_scr[hh] = _mm(vt_scr[hh, :, pl.ds(t0, tb)], p_all[hh])

    def scores(buf, kb, hh):
        off = pl.multiple_of(kb * tb, tb)
        buf[hh] = _mm(k_scr[hh, pl.ds(off, tb), :], qt_scr[hh])

    def softmax(buf, hh):
        s_t = buf[hh]
        m_old = m_scr[hh]
        m_new = jnp.maximum(m_old, jnp.max(s_t, axis=0, keepdims=True))
        m_scr[hh] = m_new
        return hh, jnp.exp2((s_t - m_new).astype(BF16)), jnp.exp2(m_old - m_new)

    def accumulate(kb, pending):
        hh, p_t, alpha = pending
        off = pl.multiple_of(kb * tb, tb)
        acc_scr[hh] = alpha * acc_scr[hh] + _mm(vt_scr[hh, :, pl.ds(off, tb)], p_t)

    def absorb(buf, kb, also=None):
        pending = None
        for hh in range(H_FOX):
            cur = softmax(buf, hh)
            if also is not None:
                also(hh)
            if pending is not None:
                accumulate(kb, pending)
            pending = cur
        accumulate(kb, pending)

    def run_blocks(kb, count):
        bufs = (sa_scr, sb_scr)
        for hh in range(H_FOX):
            scores(bufs[0], kb, hh)
        for c in range(count):
            nxt = None
            if c + 1 < count:
                nxt = functools.partial(scores, bufs[(c + 1) % 2], kb + c + 1)
            absorb(bufs[c % 2], kb + c, also=nxt)

    def kv_quad(i, carry):
        run_blocks(4 * i, 4)
        return carry

    n_quads = lax.shift_right_logical(j, 2)
    lax.fori_loop(0, n_quads, kv_quad, 0)

    @pl.when(jnp.bitwise_and(j, 2) == 2)
    def _pair():
        run_blocks(4 * n_quads, 2)

    @pl.when(jnp.bitwise_and(j, 1) == 1)
    def _single():
        run_blocks(j - 1, 1)

    zfg = proj(OFF_FG, W_FOX)
    gate0 = gate(0)
    fo = jnp.concatenate(
        [acc_scr[hh, :DH_FOX, :] / acc_scr[hh, DH_FOX:DH_FOX + 1, :] for hh in range(H_FOX)], axis=0).T
    a_in = (fo * _silu(zfg)).astype(BF16)

    cos = cos_ref[...]
    sin = sin_ref[...]
    rq, rkt, rvs, states = [], [], [], []
    for hh in range(H_RET):
        cols = slice(hh * DK_RET, (hh + 1) * DK_RET)
        rq.append(_rope(zrq[:, cols], cos, sin).astype(BF16))
        rkt.append((_rope(zrk[:, cols], cos, sin) * (DK_RET ** -0.5)).T)
        rvs.append(rv[:, hh * DV_RET:(hh + 1) * DV_RET])
        states.append(st_ref[0, hh])
    inter = [_mm(rq[hh], states[hh].astype(BF16)) for hh in range(H_RET)]
    scores = [_mm(rq[hh], rkt[hh].astype(BF16)) for hh in range(H_RET)]
    zmq = proj(OFF_MQ, W_MEM)
    merged = gate0 * _mm(a_in, wpf_ref[...])
    ret_o = [inter[hh] * decq_scr[hh] + _mm((scores[hh] * dmask_scr[hh]).astype(BF16), rvs[hh])
             for hh in range(H_RET)]
    for hh in range(H_RET):
        st_ref[0, hh] = ssc_scr[hh] * states[hh] + _mm((rkt[hh] * deck_scr[hh]).astype(BF16), rvs[hh])
    gate1 = gate(1)
    zmg = proj(OFF_MG, W_MEM)

    mem_s = []
    for hh in range(H_MEM):
        cols = slice(hh * DH_MEM, (hh + 1) * DH_MEM)
        q = _rms_rows(zmq[:, cols], gmq_ref[...]).astype(BF16)
        mem_s.append(_mm(q, mkt_ref[0, cols, :]) * (DH_MEM ** -0.5))
    r_in = jnp.concatenate(
        [_group_norm(ret_o[hh]) * _silu(zrg[:, hh * DV_RET:(hh + 1) * DV_RET]) for hh in range(H_RET)],
        axis=1).astype(BF16)
    merged = merged + gate1 * _mm(r_in, wpr_ref[...])
    m_parts = []
    for hh in range(H_MEM):
        cols = slice(hh * DH_MEM, (hh + 1) * DH_MEM)
        p = jnp.exp(mem_s[hh] - jnp.max(mem_s[hh], axis=-1, keepdims=True))
        l = jnp.sum(p, axis=-1, keepdims=True)
        m_parts.append(_mm(p.astype(BF16), mvb_ref[0, :, cols]) / l)
    gate2 = gate(2)
    m_in = (jnp.concatenate(m_parts, axis=1) * _silu(zmg)).astype(BF16)
    merged = merged + gate2 * _mm(m_in, wpm_ref[...])

    y_ref[0] = x_ref[0] + _mm(merged.astype(BF16), wo_ref[...])


def _resident(shape):
    nd = len(shape)
    return pl.BlockSpec(shape, lambda b, j: (0,) * nd, pipeline_mode=pl.Buffered(1))


def _prompt_call(x, cos, sin, mkt, mvb, wqkvf_t, bf_col, gfq_col, gfk_col, log_g, g_norm, b_merge, gmq, w_main, wpf,
                 wpr, wpm, wo, tb=PROMPT_BLOCK):
    B, T, _ = x.shape
    nt = T // tb
    kernel = functools.partial(_prompt_kernel, tb=tb)
    in_specs = [
        pl.BlockSpec(memory_space=pltpu.SMEM),
        pl.BlockSpec((1, tb, D_MODEL), lambda b, j: (b, j, 0)),
        pl.BlockSpec((tb, DK_RET), lambda b, j: (j, 0)),
        pl.BlockSpec((tb, DK_RET), lambda b, j: (j, 0)),
        pl.BlockSpec((1, W_MEM, N_MEM), lambda b, j: (b, 0, 0)),
        pl.BlockSpec((1, N_MEM, W_MEM), lambda b, j: (b, 0, 0)),
        _resident((1, D_MODEL)),
        _resident((F_ROWS, 1)),
        _resident((1, N_BRANCH * D_MODEL)),
        _resident((W_FOX, 1)),
        _resident((W_FOX, 1)),
        _resident((1, DH_MEM)),
        _resident((D_MODEL, W_MAIN)),
        _resident((W_FOX, D_MODEL)),
        _resident((W_RET_V, D_MODEL)),
        _resident((W_MEM, D_MODEL)),
        _resident((D_MODEL, D_MODEL)),
        _resident((3 * W_FOX + F_ROWS, D_MODEL)),
    ]
    out_specs = [
        pl.BlockSpec((1, tb, D_MODEL), lambda b, j: (b, j, 0)),
        pl.BlockSpec((1, W_FOX, tb), lambda b, j: (b, 0, j)),
        pl.BlockSpec((1, W_FOX, tb), lambda b, j: (b, 0, j)),
        pl.BlockSpec((1, H_FOX, tb), lambda b, j: (b, 0, j)),
        pl.BlockSpec((1, H_RET, DK_RET, DV_RET), lambda b, j: (b, 0, 0, 0)),
    ]
    out_shape = [
        jax.ShapeDtypeStruct((B, T, D_MODEL), F32),
        jax.ShapeDtypeStruct((B, W_FOX, T), F32),
        jax.ShapeDtypeStruct((B, W_FOX, T), F32),
        jax.ShapeDtypeStruct((B, H_FOX, T), F32),
        jax.ShapeDtypeStruct((B, H_RET, DK_RET, DV_RET), F32),
    ]
    scratch = [
        pltpu.VMEM((H_FOX, T, LANES), BF16),
        pltpu.VMEM((H_FOX, V_ROWS, T), BF16),
        pltpu.VMEM((H_FOX, LANES, tb), BF16),
        pltpu.VMEM((H_FOX, 1, tb), F32),
        pltpu.VMEM((H_FOX, V_ROWS, tb), F32),
        pltpu.VMEM((H_FOX, tb, tb), F32),
        pltpu.VMEM((H_FOX, tb, tb), F32),
        pltpu.VMEM((F_ROWS, 1), F32),
        pltpu.VMEM((H_RET, tb, tb), F32),
        pltpu.VMEM((H_RET, tb, 1), F32),
        pltpu.VMEM((H_RET, 1, tb), F32),
        pltpu.VMEM((H_RET, 1, DV_RET), F32),
    ]
    return pl.pallas_call(
        kernel,
        grid=(B, nt),
        in_specs=in_specs,
        out_specs=out_specs,
        out_shape=out_shape,
        scratch_shapes=scratch,
        compiler_params=pltpu.CompilerParams(
            dimension_semantics=("arbitrary", "arbitrary"), vmem_limit_bytes=VMEM_LIMIT_BYTES),
        name="prompt",
    )(log_g, x, cos, sin, mkt, mvb, g_norm, bf_col, b_merge, gfq_col, gfk_col, gmq, w_main, wpf, wpr, wpm, wo, wqkvf_t)


def _sample_kernel(lg_ref, x_ref, cos_ref, sin_ref, ckt_ref, cvt_ref, clf_ref, st0_ref, cmk_ref, cmv_ref,
                   gn_ref, bf_ref, bm_ref, gfq_ref, gfk_ref, gmq_ref, seg_ref, w_ref, wpf_ref, wpr_ref, wpm_ref,
                   wo_ref, wft_ref,
                   y_ref, fk_ref, fv_ref, lft_ref, st_ref,
                   h_scr, fq_scr, knt_scr, vn_scr, ncn_scr, rq_scr, rkt_scr, rv_scr, mq_scr, fo_scr, ro_scr, mo_scr,
                   *, nb, ts, past):
    b = pl.program_id(0)
    n = nb * ts
    r0 = pl.multiple_of(b * ts, ts)

    @pl.when(b == 0)
    def _project_all():
        h = _rms_rows(x_ref[...], gn_ref[...]).astype(BF16)
        h_scr[...] = h
        fq_scr[...] = _head_norm64(_mm(h, w_ref[:, OFF_FQ:OFF_FQ + W_FOX]), gfq_ref[...], seg_ref) * (DH_FOX ** -0.5)
        fk = _head_norm64(_mm(h, w_ref[:, OFF_FK:OFF_FK + W_FOX]), gfk_ref[...], seg_ref)
        fv = _mm(h, w_ref[:, OFF_FV:OFF_FV + W_FOX])
        logf_t = _log_sigmoid(_mm_nt(wft_ref[...], h) + bf_ref[...])
        fk_ref[...] = fk
        fv_ref[...] = fv
        lft_ref[...] = logf_t[:H_FOX]
        knt_scr[...] = fk.T.astype(BF16)
        vn_scr[...] = fv.astype(BF16)
        ri = lax.broadcasted_iota(jnp.int32, (n, n), 0)
        ci = lax.broadcasted_iota(jnp.int32, (n, n), 1)
        same = _idiv(ri, ts) == _idiv(ci, ts)
        triu = jnp.where(jnp.logical_and(same, ri <= ci), 1.0, 0.0).astype(BF16)
        ncn_scr[...] = -_cumsum_lanes(triu, logf_t)
        cos = cos_ref[...]
        sin = sin_ref[...]
        zrq = _mm(h, w_ref[:, OFF_RQ:OFF_RQ + W_RET_QK])
        zrk = _mm(h, w_ref[:, OFF_RK:OFF_RK + W_RET_QK])
        for hh in range(H_RET):
            cols = slice(hh * DK_RET, (hh + 1) * DK_RET)
            rq_scr[:, cols] = _rope(zrq[:, cols], cos, sin)
            rkt_scr[cols, :] = (_rope(zrk[:, cols], cos, sin) * (DK_RET ** -0.5)).T
        rv_scr[...] = _mm(h, w_ref[:, OFF_RV:OFF_RV + W_RET_V]).astype(BF16)
        zmq = _mm(h, w_ref[:, OFF_MQ:OFF_MQ + W_MEM])
        for hh in range(H_MEM):
            cols = slice(hh * DH_MEM, (hh + 1) * DH_MEM)
            mq_scr[:, cols] = _rms_rows(zmq[:, cols], gmq_ref[...])

    nr = H_FOX * ts
    fq_b = fq_scr[pl.ds(r0, ts), :]
    row_head = _idiv(lax.broadcasted_iota(jnp.int32, (nr, W_FOX), 0), ts)
    col_head = _idiv(lax.broadcasted_iota(jnp.int32, (nr, W_FOX), 1), DH_FOX)
    head_sel = row_head == col_head
    q_bd = jnp.where(head_sel, jnp.concatenate([fq_b] * H_FOX, axis=0), 0.0).astype(BF16)

    cc = 256
    ri = lax.broadcasted_iota(jnp.int32, (cc, cc), 0)
    ci = lax.broadcasted_iota(jnp.int32, (cc, cc), 1)
    triu = jnp.where(ri <= ci, 1.0, 0.0).astype(BF16)
    carry = jnp.zeros((H_FOX, 1), F32)
    c_chunks = []
    for c in range(past // cc):
        cch = _cumsum_lanes(triu, clf_ref[0, :, c * cc:(c + 1) * cc]) + carry
        carry = cch[:, cc - 1:cc]
        c_chunks.append(cch)
    c_past = jnp.concatenate(c_chunks, axis=1)
    ncp = jnp.concatenate([jnp.broadcast_to(-c_past[hh:hh + 1, :], (ts, past)) for hh in range(H_FOX)], axis=0)
    ncn = ncn_scr[:H_FOX, :] - carry
    ncn = jnp.concatenate([jnp.broadcast_to(ncn[hh:hh + 1, :], (ts, n)) for hh in range(H_FOX)], axis=0)

    s_p = _mm(q_bd, ckt_ref[0].astype(BF16)) + ncp
    s_n = _mm(q_bd, knt_scr[...]) + ncn
    tok = lax.broadcasted_iota(jnp.int32, (nr, n), 1)
    qt = _imod(lax.broadcasted_iota(jnp.int32, (nr, n), 0), ts)
    valid = jnp.logical_and(_idiv(tok, ts) == b, _imod(tok, ts) <= qt)
    s_n = jnp.where(valid, s_n, -jnp.inf)
    m = jnp.maximum(jnp.max(s_p, axis=-1, keepdims=True), jnp.max(s_n, axis=-1, keepdims=True))
    p_p = jnp.exp(s_p - m)
    p_n = jnp.exp(s_n - m)
    l = jnp.sum(p_p, axis=-1, keepdims=True) + jnp.sum(p_n, axis=-1, keepdims=True)
    o_past = _mm_nt(p_p.astype(BF16), cvt_ref[0].astype(BF16))
    o_bd = (o_past + _mm(p_n.astype(BF16), vn_scr[...])) / l
    o_bd = jnp.where(head_sel, o_bd, 0.0)
    fo = o_bd[0:ts, :]
    for hh in range(1, H_FOX):
        fo = fo + o_bd[hh * ts:(hh + 1) * ts, :]
    fo_scr[pl.ds(r0, ts), :] = fo

    tokr = lax.broadcasted_iota(jnp.int32, (ts, n), 1)
    qtr = lax.broadcasted_iota(jnp.int32, (ts, n), 0)
    in_b = _idiv(tokr, ts) == b
    dt = (qtr - _imod(tokr, ts)).astype(F32)
    tokc = lax.broadcasted_iota(jnp.int32, (1, n), 1)
    kpos = _imod(tokc, ts).astype(F32)
    rowi = lax.broadcasted_iota(jnp.int32, (ts, 1), 0).astype(F32)
    r_parts = []
    for hh in range(H_RET):
        lg = lg_ref[hh]
        q = rq_scr[pl.ds(r0, ts), hh * DK_RET:(hh + 1) * DK_RET].astype(BF16)
        kt = rkt_scr[hh * DK_RET:(hh + 1) * DK_RET, :]
        v = rv_scr[:, hh * DV_RET:(hh + 1) * DV_RET]
        state = st0_ref[0, hh]
        dmask = jnp.where(jnp.logical_and(in_b, dt >= 0.0), jnp.exp(lg * jnp.maximum(dt, 0.0)), 0.0)
        inter = _mm(q, state.astype(BF16)) * jnp.exp(lg * (rowi + 1.0))
        scores = _mm(q, kt.astype(BF16)) * dmask
        r_parts.append(inter + _mm(scores.astype(BF16), v))
        deck = jnp.where(_idiv(tokc, ts) == b, jnp.exp(lg * (ts - 1.0 - kpos)), 0.0)
        st_ref[0, hh] = jnp.exp(jnp.full((1, DV_RET), lg * ts, F32)) * state + _mm((kt * deck).astype(BF16), v)
    ro_scr[pl.ds(r0, ts), :] = jnp.concatenate(r_parts, axis=1)

    m_parts = []
    for hh in range(H_MEM):
        cols = slice(hh * DH_MEM, (hh + 1) * DH_MEM)
        q = mq_scr[pl.ds(r0, ts), cols].astype(BF16)
        rows = pl.ds(hh, N_MEM, stride=H_MEM)
        s = _mm_nt(q, cmk_ref[0, rows, :].astype(BF16)) * (DH_MEM ** -0.5)
        p = jnp.exp(s - jnp.max(s, axis=-1, keepdims=True))
        l = jnp.sum(p, axis=-1, keepdims=True)
        m_parts.append(_mm(p.astype(BF16), cmv_ref[0, rows, :].astype(BF16)) / l)
    mo_scr[pl.ds(r0, ts), :] = jnp.concatenate(m_parts, axis=1)

    @pl.when(b == nb - 1)
    def _output_all():
        h = h_scr[...]
        a_in = (fo_scr[...] * _silu(_mm(h, w_ref[:, OFF_FG:OFF_FG + W_FOX]))).astype(BF16)
        merged = _gate(h, w_ref, bm_ref, 0) * _mm(a_in, wpf_ref[...])
        zrg = _mm(h, w_ref[:, OFF_RG:OFF_RG + W_RET_V])
        ro = ro_scr[...]
        r_parts = []
        for hh in range(H_RET):
            cols = slice(hh * DV_RET, (hh + 1) * DV_RET)
            r_parts.append(_group_norm(ro[:, cols]) * _silu(zrg[:, cols]))
        r_in = jnp.concatenate(r_parts, axis=1).astype(BF16)
        merged = merged + _gate(h, w_ref, bm_ref, 1) * _mm(r_in, wpr_ref[...])
        m_in = (mo_scr[...] * _silu(_mm(h, w_ref[:, OFF_MG:OFF_MG + W_MEM]))).astype(BF16)
        merged = merged + _gate(h, w_ref, bm_ref, 2) * _mm(m_in, wpm_ref[...])
        y_ref[...] = x_ref[...] + _mm(merged.astype(BF16), wo_ref[...])


def _sample_call(x2d, cos, sin, ckt, cvt, clf_t, st0, cmk, cmv, wf_t, bf_col, log_g, g_norm, b_merge, gfq, gfk, gmq, seg,
                 w_main, wpf, wpr, wpm, wo, nb, ts):
    n = nb * ts
    past = ckt.shape[2]
    kernel = functools.partial(_sample_kernel, nb=nb, ts=ts, past=past)

    def res(shape):
        nd = len(shape)
        return pl.BlockSpec(shape, lambda b: (0,) * nd, pipeline_mode=pl.Buffered(1))

    in_specs = [
        pl.BlockSpec(memory_space=pltpu.SMEM),
        res((n, D_MODEL)),
        res((n, DK_RET)),
        res((n, DK_RET)),
        pl.BlockSpec((1, W_FOX, past), lambda b: (b, 0, 0)),
        pl.BlockSpec((1, W_FOX, past), lambda b: (b, 0, 0)),
        pl.BlockSpec((1, H_FOX, past), lambda b: (b, 0, 0)),
        pl.BlockSpec((1, H_RET, DK_RET, DV_RET), lambda b: (b, 0, 0, 0)),
        pl.BlockSpec((1, N_MEM * H_MEM, DH_MEM), lambda b: (b, 0, 0)),
        pl.BlockSpec((1, N_MEM * H_MEM, DH_MEM), lambda b: (b, 0, 0)),
        res((1, D_MODEL)),
        res((F_ROWS, 1)),
        res((1, N_BRANCH * D_MODEL)),
        res((1, W_FOX)),
        res((1, W_FOX)),
        res((1, DH_MEM)),
        res((MXU_TILE, MXU_TILE)),
        res((D_MODEL, W_MAIN)),
        res((W_FOX, D_MODEL)),
        res((W_RET_V, D_MODEL)),
        res((W_MEM, D_MODEL)),
        res((D_MODEL, D_MODEL)),
        res((F_ROWS, D_MODEL)),
    ]
    full = lambda shape: pl.BlockSpec(shape, lambda b: (0,) * len(shape))
    out_specs = [
        full((n, D_MODEL)),
        full((n, W_FOX)),
        full((n, W_FOX)),
        full((H_FOX, n)),
        pl.BlockSpec((1, H_RET, DK_RET, DV_RET), lambda b: (b, 0, 0, 0)),
    ]
    out_shape = [
        jax.ShapeDtypeStruct((n, D_MODEL), F32),
        jax.ShapeDtypeStruct((n, W_FOX), F32),
        jax.ShapeDtypeStruct((n, W_FOX), F32),
        jax.ShapeDtypeStruct((H_FOX, n), F32),
        jax.ShapeDtypeStruct((nb, H_RET, DK_RET, DV_RET), F32),
    ]
    scratch = [
        pltpu.VMEM((n, D_MODEL), BF16),
        pltpu.VMEM((n, W_FOX), F32),
        pltpu.VMEM((W_FOX, n), BF16),
        pltpu.VMEM((n, W_FOX), BF16),
        pltpu.VMEM((F_ROWS, n), F32),
        pltpu.VMEM((n, W_RET_QK), F32),
        pltpu.VMEM((W_RET_QK, n), F32),
        pltpu.VMEM((n, W_RET_V), BF16),
        pltpu.VMEM((n, W_MEM), F32),
        pltpu.VMEM((n, W_FOX), F32),
        pltpu.VMEM((n, W_RET_V), F32),
        pltpu.VMEM((n, W_MEM), F32),
    ]
    return pl.pallas_call(
        kernel,
        grid=(nb,),
        in_specs=in_specs,
        out_specs=out_specs,
        out_shape=out_shape,
        scratch_shapes=scratch,
        compiler_params=pltpu.CompilerParams(
            dimension_semantics=("arbitrary",), vmem_limit_bytes=VMEM_LIMIT_BYTES),
        name="sample",
    )(log_g, x2d, cos, sin, ckt, cvt, clf_t, st0, cmk, cmv, g_norm, bf_col, b_merge, gfq, gfk, gmq, seg,
      w_main, wpf, wpr, wpm, wo, wf_t)


def _rope_tables(pos):
    half = DK_RET // 2
    inv = ROPE_BASE ** (-jnp.arange(half, dtype=F32) / half)
    ang = pos.astype(F32)[:, None] * inv[None, :]
    cos = jnp.cos(ang)
    sin = jnp.sin(ang)
    return jnp.concatenate([cos, cos], axis=1), jnp.concatenate([-sin, sin], axis=1)


def _reorder_w_in(w_t):
    assert (3 * W_FOX) % REORDER_ROWS == 0 and W_MAIN % REORDER_ROWS == 0 and REORDER_ROWS % H_FOX == 0
    assert w_t.shape == (W_MAIN + H_FOX, D_MODEL)
    return pl.pallas_call(
        _reorder_kernel,
        grid=(W_MAIN // REORDER_ROWS,),
        in_specs=[
            pl.BlockSpec((REORDER_ROWS, D_MODEL), lambda c: (c, 0)),
            pl.BlockSpec((H_FOX, D_MODEL), lambda c: ((c + 1) * (REORDER_ROWS // H_FOX), 0)),
        ],
        out_specs=pl.BlockSpec((D_MODEL, REORDER_ROWS), lambda c: (0, c)),
        out_shape=jax.ShapeDtypeStruct((D_MODEL, W_MAIN), BF16),
        compiler_params=pltpu.CompilerParams(dimension_semantics=("arbitrary",)),
        name="reorder_w_in",
    )(w_t, w_t)


def _reorder_kernel(a_ref, b_ref, o_ref):
    c = pl.program_id(0)
    a = a_ref[...]
    shifted = jnp.concatenate([a[H_FOX:], b_ref[...]], axis=0)
    src = jnp.where(c < (3 * W_FOX) // REORDER_ROWS, a, shifted)
    o_ref[...] = src.T.astype(BF16)


def kernel(x_prompt, x_sample, mem_prompt, cache_fox_k, cache_fox_v, cache_fox_logf, state_ret, cache_mem_k, cache_mem_v,
           g_norm, g_mem_norm, w_in, b_f, b_merge, g_fox_q, g_fox_k, g_mem_q, g_mem_k, w_mem_kv,
           w_p_fox, w_p_ret, w_p_mem, w_out):
    depth = w_in.shape[0]
    assert depth == 1, "single-layer kernel"
    B, T, _ = x_prompt.shape
    nb, ts, _ = x_sample.shape
    past = cache_fox_k.shape[2]

    log_g = jnp.log1p(-jnp.exp2(-5.0 - jnp.arange(H_RET, dtype=F32)))
    head_of_lane = jnp.arange(MXU_TILE) // DH_FOX
    seg = (head_of_lane[:, None] == head_of_lane[None, :]).astype(BF16)
    w_t = jnp.swapaxes(w_in[0], 0, 1)
    w_main = _reorder_w_in(w_t)
    gn = g_norm[0].reshape(1, D_MODEL)
    bm = b_merge[0].reshape(1, N_BRANCH * D_MODEL)
    gfq = jnp.tile(g_fox_q[0], H_FOX).reshape(1, W_FOX)
    gfk = jnp.tile(g_fox_k[0], H_FOX).reshape(1, W_FOX)
    gmq = g_mem_q[0].reshape(1, DH_MEM)
    wpf = w_p_fox[0].astype(BF16)
    wpr = w_p_ret[0].astype(BF16)
    wpm = w_p_mem[0].astype(BF16)
    wo = w_out[0].astype(BF16)
    shared = (gn, bm, gfq, gfk, gmq, seg, w_main, wpf, wpr, wpm, wo)

    mk, mv, mkt, mvb = _memkv_call(mem_prompt, g_mem_norm[0], w_mem_kv[0], g_mem_k[0])

    cos_p, sin_p = _rope_tables(jnp.arange(T, dtype=jnp.int32))
    wqkvf_t = jnp.concatenate(
        [w_t[:3 * W_FOX + H_FOX], jnp.zeros((F_ROWS - H_FOX, D_MODEL), F32)], axis=0).astype(BF16)
    bf_col = jnp.concatenate([b_f[0], jnp.zeros((F_ROWS - H_FOX,), F32)]).reshape(F_ROWS, 1)
    y_p, fkt_p, fvt_p, lft_p, st_p = _prompt_call(
        x_prompt, cos_p, sin_p, mkt, mvb, wqkvf_t, bf_col, gfq.reshape(W_FOX, 1), gfk.reshape(W_FOX, 1), log_g,
        gn, bm, gmq, w_main, wpf, wpr, wpm, wo)

    def time_major(a_t):
        return jnp.transpose(a_t.reshape(a_t.shape[0], H_FOX, DH_FOX, a_t.shape[2]), (0, 3, 1, 2))[None]

    def time_minor(a):
        return jnp.transpose(a, (0, 2, 3, 1)).reshape(a.shape[0], W_FOX, a.shape[1])

    fk_p = time_major(fkt_p)
    fv_p = time_major(fvt_p)
    lf_p = jnp.swapaxes(lft_p, 1, 2)[None]

    pos_s = past + jnp.arange(ts, dtype=jnp.int32)
    cos_s, sin_s = _rope_tables(jnp.tile(pos_s, nb))
    y_s, fk_s, fv_s, lft_s, st_s = _sample_call(
        x_sample.reshape(nb * ts, D_MODEL), cos_s, sin_s,
        time_minor(cache_fox_k[0]), time_minor(cache_fox_v[0]),
        jnp.swapaxes(cache_fox_logf[0], 1, 2), state_ret[0],
        cache_mem_k[0].reshape(nb, N_MEM * H_MEM, DH_MEM), cache_mem_v[0].reshape(nb, N_MEM * H_MEM, DH_MEM),
        wqkvf_t[3 * W_FOX:], bf_col, log_g, *shared, nb=nb, ts=ts)
    lf_s = jnp.transpose(lft_s.reshape(H_FOX, nb, ts), (1, 2, 0))[None]

    return (y_p, y_s.reshape(nb, ts, D_MODEL),
            fk_p, fv_p, lf_p,
            st_p[None], mk.reshape(1, B, N_MEM, H_MEM, DH_MEM), mv.reshape(1, B, N_MEM, H_MEM, DH_MEM),
            fk_s.reshape(1, nb, ts, H_FOX, DH_FOX), fv_s.reshape(1, nb, ts, H_FOX, DH_FOX),
            lf_s, st_s[None])
```

```python
import functools

import jax
import jax.numpy as jnp
from jax import lax
from jax.experimental import pallas as pl
from jax.experimental.pallas import tpu as pltpu

D_MODEL = 1024
N_MEM = 256
H_FOX = 8
DH_FOX = 64
H_RET = 4
DK_RET = 128
DV_RET = 256
H_MEM = 4
DH_MEM = 128
W_FOX = H_FOX * DH_FOX
W_RET_QK = H_RET * DK_RET
W_RET_V = H_RET * DV_RET
W_MEM = H_MEM * DH_MEM
N_BRANCH = 3
ROPE_BASE = 10000.0
EPS = 1e-6
LOG2E = 1.4426950408889634

LANES = 128
MXU_TILE = 256
BF16_SUBLANES = 16
V_ROWS = DH_FOX + BF16_SUBLANES
F_ROWS = BF16_SUBLANES

OFF_FQ = 0
OFF_FK = OFF_FQ + W_FOX
OFF_FV = OFF_FK + W_FOX
OFF_FG = OFF_FV + W_FOX
OFF_RQ = OFF_FG + W_FOX
OFF_RK = OFF_RQ + W_RET_QK
OFF_RV = OFF_RK + W_RET_QK
OFF_RG = OFF_RV + W_RET_V
OFF_MQ = OFF_RG + W_RET_V
OFF_MG = OFF_MQ + W_MEM
OFF_GL = OFF_MG + W_MEM
W_MAIN = OFF_GL + N_BRANCH * D_MODEL
REORDER_ROWS = 512

PROMPT_BLOCK = 256
VMEM_LIMIT_BYTES = 60 * 1024 * 1024

F32 = jnp.float32
BF16 = jnp.bfloat16


def _mm(a, b):
    return jnp.dot(a, b, preferred_element_type=F32)


def _mm_nt(a, b):
    return lax.dot_general(a, b, (((1,), (1,)), ((), ())), preferred_element_type=F32)


def _idiv(x, d):
    assert d & (d - 1) == 0
    return lax.shift_right_logical(x, d.bit_length() - 1)


def _imod(x, d):
    assert d & (d - 1) == 0
    return jnp.bitwise_and(x, d - 1)


def _rms_rows(x, g):
    ms = jnp.mean(x * x, axis=-1, keepdims=True)
    return x * lax.rsqrt(ms + EPS) * g


def _head_sumsq(z, seg_ref):
    sq = (z * z).astype(BF16)
    return jnp.concatenate(
        [_mm(sq[:, c:c + MXU_TILE], seg_ref[...]) for c in range(0, W_FOX, MXU_TILE)], axis=1)


def _head_norm64(z, g, seg_ref):
    return z * lax.rsqrt(_head_sumsq(z, seg_ref) * (1.0 / DH_FOX) + EPS) * g


def _log_sigmoid(u):
    return jnp.minimum(u, 0.0) - jnp.log1p(jnp.exp(-jnp.abs(u)))


def _silu(u):
    return u * jax.nn.sigmoid(u)


def _split3(a):
    hi = a.astype(BF16)
    r1 = a - hi.astype(F32)
    mid = r1.astype(BF16)
    lo = (r1 - mid.astype(F32)).astype(BF16)
    return hi, mid, lo


def _cumsum_lanes(triu, a):
    hi, mid, lo = _split3(a)
    return _mm(hi, triu) + _mm(mid, triu) + _mm(lo, triu)


def _rope(x, cos, sin_signed):
    return x * cos + pltpu.roll(x, DK_RET // 2, 1) * sin_signed


def _group_norm(o):
    mu = jnp.mean(o, axis=-1, keepdims=True)
    d = o - mu
    var = jnp.mean(d * d, axis=-1, keepdims=True)
    return d * lax.rsqrt(var + EPS)


def _gate(h, w_ref, bm_ref, i):
    z = _mm(h, w_ref[:, OFF_GL + i * D_MODEL:OFF_GL + (i + 1) * D_MODEL])
    return jax.nn.sigmoid(z + bm_ref[:, i * D_MODEL:(i + 1) * D_MODEL])


def _memkv_kernel(mem_ref, gmn_ref, w_ref, gmk_ref, mk_ref, mv_ref, mkt_ref, mvb_ref):
    h = _rms_rows(mem_ref[0], gmn_ref[...]).astype(BF16)
    z = _mm(h, w_ref[...])
    ks = []
    for hh in range(H_MEM):
        ks.append(_rms_rows(z[:, hh * DH_MEM:(hh + 1) * DH_MEM], gmk_ref[...]))
    mk = jnp.concatenate(ks, axis=1)
    mv = z[:, W_MEM:]
    for hh in range(H_MEM):
        rows = pl.ds(hh, N_MEM, stride=H_MEM)
        mk_ref[0, rows, :] = ks[hh]
        mv_ref[0, rows, :] = mv[:, hh * DH_MEM:(hh + 1) * DH_MEM]
    mkt_ref[0] = mk.T.astype(BF16)
    mvb_ref[0] = mv.astype(BF16)


def _memkv_call(mem, g_mem_norm, w_mem_kv, g_mem_k):
    B = mem.shape[0]
    const = lambda b: (0, 0)
    per_b = lambda b: (b, 0, 0)
    return pl.pallas_call(
        _memkv_kernel,
        grid=(B,),
        in_specs=[
            pl.BlockSpec((1, N_MEM, D_MODEL), per_b),
            pl.BlockSpec((1, D_MODEL), const),
            pl.BlockSpec((D_MODEL, 2 * W_MEM), const),
            pl.BlockSpec((1, DH_MEM), const),
        ],
        out_specs=[
            pl.BlockSpec((1, N_MEM * H_MEM, DH_MEM), per_b),
            pl.BlockSpec((1, N_MEM * H_MEM, DH_MEM), per_b),
            pl.BlockSpec((1, W_MEM, N_MEM), per_b),
            pl.BlockSpec((1, N_MEM, W_MEM), per_b),
        ],
        out_shape=[
            jax.ShapeDtypeStruct((B, N_MEM * H_MEM, DH_MEM), F32),
            jax.ShapeDtypeStruct((B, N_MEM * H_MEM, DH_MEM), F32),
            jax.ShapeDtypeStruct((B, W_MEM, N_MEM), BF16),
            jax.ShapeDtypeStruct((B, N_MEM, W_MEM), BF16),
        ],
        compiler_params=pltpu.CompilerParams(dimension_semantics=("arbitrary",)),
        name="memkv",
    )(mem, g_mem_norm.reshape(1, D_MODEL), w_mem_kv.astype(BF16), g_mem_k.reshape(1, DH_MEM))


def _prompt_kernel(lg_ref, x_ref, cos_ref, sin_ref, mkt_ref, mvb_ref, gn_ref, bf_ref, bm_ref, gfq_ref, gfk_ref,
                   gmq_ref, w_ref, wpf_ref, wpr_ref, wpm_ref, wo_ref, wqkvf_ref,
                   y_ref, fkt_ref, fvt_ref, lft_ref, st_ref,
                   k_scr, vt_scr, qt_scr, m_scr, acc_scr, sa_scr, sb_scr, ccar_scr, dmask_scr, decq_scr, deck_scr,
                   ssc_scr, *, tb):
    b = pl.program_id(0)
    j = pl.program_id(1)
    t0 = pl.multiple_of(j * tb, tb)

    @pl.when(jnp.logical_and(b == 0, j == 0))
    def _init_tables():
        ri = lax.broadcasted_iota(jnp.int32, (tb, tb), 0)
        ci = lax.broadcasted_iota(jnp.int32, (tb, tb), 1)
        diff = (ri - ci).astype(F32)
        rowi = lax.broadcasted_iota(jnp.int32, (tb, 1), 0).astype(F32)
        coli = lax.broadcasted_iota(jnp.int32, (1, tb), 1).astype(F32)
        for hh in range(H_RET):
            lg = lg_ref[hh]
            dmask_scr[hh] = jnp.where(diff >= 0.0, jnp.exp(lg * jnp.maximum(diff, 0.0)), 0.0)
            decq_scr[hh] = jnp.exp(lg * (rowi + 1.0))
            deck_scr[hh] = jnp.exp(lg * (tb - 1.0 - coli))
            ssc_scr[hh] = jnp.exp(jnp.full((1, DV_RET), lg * tb, F32))

    @pl.when(j == 0)
    def _init_batch():
        ccar_scr[...] = jnp.zeros_like(ccar_scr)
        st_ref[...] = jnp.zeros_like(st_ref)

    h = _rms_rows(x_ref[0], gn_ref[...]).astype(BF16)

    def proj(off, width):
        return _mm(h, w_ref[:, off:off + width])

    def gate(i):
        return jax.nn.sigmoid(proj(OFF_GL + i * D_MODEL, D_MODEL) + bm_ref[:, i * D_MODEL:(i + 1) * D_MODEL])


    half = (3 * W_FOX + F_ROWS) // (2 * BF16_SUBLANES) * BF16_SUBLANES
    z_t = jnp.concatenate([_mm_nt(wqkvf_ref[:half, :], h), _mm_nt(wqkvf_ref[half:, :], h)], axis=0)
    zrq = proj(OFF_RQ, W_RET_QK)
    zrk = proj(OFF_RK, W_RET_QK)

    def head_norm_t(zh_t, g_col):
        parts = []
        for hh in range(H_FOX):
            zz = zh_t[hh * DH_FOX:(hh + 1) * DH_FOX]
            ms = jnp.sum(zz * zz, axis=0, keepdims=True) * (1.0 / DH_FOX)
            parts.append(zz * lax.rsqrt(ms + EPS))
        return jnp.concatenate(parts, axis=0) * g_col

    fq_t = head_norm_t(z_t[:W_FOX], gfq_ref[...]) * (DH_FOX ** -0.5 * LOG2E)
    fk_t = head_norm_t(z_t[W_FOX:2 * W_FOX], gfk_ref[...])
    fv_t = z_t[2 * W_FOX:3 * W_FOX]
    logf_t = _log_sigmoid(z_t[3 * W_FOX:] + bf_ref[...])
    ri = lax.broadcasted_iota(jnp.int32, (tb, tb), 0)
    ci = lax.broadcasted_iota(jnp.int32, (tb, tb), 1)
    key_le_query = ri <= ci
    c_t = _cumsum_lanes(jnp.where(key_le_query, 1.0, 0.0).astype(BF16), logf_t) + ccar_scr[...]
    ccar_scr[...] = c_t[:, tb - 1:tb]
    rv = proj(OFF_RV, W_RET_V).astype(BF16)
    fk = fk_t.T
    fkt_ref[0] = fk_t
    fvt_ref[0] = fv_t
    lft_ref[0] = logf_t[:H_FOX]
    hi, mid, lo = _split3(c_t[:H_FOX] * (-LOG2E))
    parts_t = jnp.concatenate(
        [hi.astype(F32), mid.astype(F32), lo.astype(F32), jnp.zeros((LANES - 3 * H_FOX, tb), F32)], axis=0)
    bias = pltpu.roll(parts_t.T, DH_FOX, 1)
    lane = lax.broadcasted_iota(jnp.int32, (tb, LANES), 1)
    vrow = lax.broadcasted_iota(jnp.int32, (V_ROWS - DH_FOX, tb), 0)
    ones_row = jnp.where(vrow == 0, 1.0, 0.0).astype(BF16)
    fv_tb = fv_t.astype(BF16)
    for hh in range(H_FOX):
        vt_scr[hh, :, pl.ds(t0, tb)] = jnp.concatenate([fv_tb[hh * DH_FOX:(hh + 1) * DH_FOX, :], ones_row], axis=0)
    qrow = lax.broadcasted_iota(jnp.int32, (LANES - DH_FOX, tb), 0)
    k_blk = []
    qt_blk = []
    for hh in range(H_FOX):
        pair = fk[:, (hh // 2) * LANES:(hh // 2 + 1) * LANES]
        if hh % 2:
            pair = pltpu.roll(pair, DH_FOX, 1)
        k_aug = jnp.where(lane < DH_FOX, pair, bias).astype(BF16)
        k_scr[hh, pl.ds(t0, tb), :] = k_aug
        k_blk.append(k_aug)
        ones = jnp.where(
            jnp.logical_or(qrow == hh, jnp.logical_or(qrow == H_FOX + hh, qrow == 2 * H_FOX + hh)), 1.0, 0.0)
        qt_aug = jnp.concatenate([fq_t[hh * DH_FOX:(hh + 1) * DH_FOX, :], ones], axis=0).astype(BF16)
        qt_scr[hh] = qt_aug
        qt_blk.append(qt_aug)

    s_all = [_mm(k_blk[hh], qt_blk[hh]) for hh in range(H_FOX)]
    zrg = proj(OFF_RG, W_RET_V)
    p_all = []
    for hh in range(H_FOX):
        s_t = jnp.where(key_le_query, s_all[hh], -jnp.inf)
        m = jnp.max(s_t, axis=0, keepdims=True)
        m_scr[hh] = m
        p_all.append(jnp.exp2((s_t - m).astype(BF16)))
    for hh in range(H_FOX):
        acc_scr[hh] = _mm(vt_scr[hh, :, pl.ds(t0, tb)], p_all[hh])

    def scores(buf, kb, hh):
        off = pl.multiple_of(kb * tb, tb)
        buf[hh] = _mm(k_scr[hh, pl.ds(off, tb), :], qt_scr[hh])

    def softmax(buf, hh):
        s_t = buf[hh]
        m_old = m_scr[hh]
        m_new = jnp.maximum(m_old, jnp.max(s_t, axis=0, keepdims=True))
        m_scr[hh] = m_new
        return hh, jnp.exp2((s_t - m_new).astype(BF16)), jnp.exp2(m_old - m_new)

    def accumulate(kb, pending):
        hh, p_t, alpha = pending
        off = pl.multiple_of(kb * tb, tb)
        acc_scr[hh] = alpha * acc_scr[hh] + _mm(vt_scr[hh, :, pl.ds(off, tb)], p_t)

    def absorb(buf, kb, also=None):
        pending = None
        for hh in range(H_FOX):
            cur = softmax(buf, hh)
            if also is not None:
                also(hh)
            if pending is not None:
                accumulate(kb, pending)
            pending = cur
        accumulate(kb, pending)

    def run_blocks(kb, count):
        bufs = (sa_scr, sb_scr)
        for hh in range(H_FOX):
            scores(bufs[0], kb, hh)
        for c in range(count):
            nxt = None
            if c + 1 < count:
                nxt = functools.partial(scores, bufs[(c + 1) % 2], kb + c + 1)
            absorb(bufs[c % 2], kb + c, also=nxt)

    def kv_quad(i, carry):
        run_blocks(4 * i, 4)
        return carry

    n_quads = lax.shift_right_logical(j, 2)
    lax.fori_loop(0, n_quads, kv_quad, 0)
    for rest in range(1, 4):
        pl.when(jnp.bitwise_and(j, 3) == rest)(functools.partial(run_blocks, 4 * n_quads, rest))

    zfg = proj(OFF_FG, W_FOX)
    gate0 = gate(0)
    fo = jnp.concatenate(
        [acc_scr[hh, :DH_FOX, :] / acc_scr[hh, DH_FOX:DH_FOX + 1, :] for hh in range(H_FOX)], axis=0).T
    a_in = (fo * _silu(zfg)).astype(BF16)

    cos = cos_ref[...]
    sin = sin_ref[...]
    rq, rkt, rvs, states = [], [], [], []
    for hh in range(H_RET):
        cols = slice(hh * DK_RET, (hh + 1) * DK_RET)
        rq.append(_rope(zrq[:, cols], cos, sin).astype(BF16))
        rkt.append((_rope(zrk[:, cols], cos, sin) * (DK_RET ** -0.5)).T)
        rvs.append(rv[:, hh * DV_RET:(hh + 1) * DV_RET])
        states.append(st_ref[0, hh])
    inter = [_mm(rq[hh], states[hh].astype(BF16)) for hh in range(H_RET)]
    scores = [_mm(rq[hh], rkt[hh].astype(BF16)) for hh in range(H_RET)]
    zmq = proj(OFF_MQ, W_MEM)
    merged = gate0 * _mm(a_in, wpf_ref[...])
    ret_o = [inter[hh] * decq_scr[hh] + _mm((scores[hh] * dmask_scr[hh]).astype(BF16), rvs[hh])
             for hh in range(H_RET)]
    for hh in range(H_RET):
        st_ref[0, hh] = ssc_scr[hh] * states[hh] + _mm((rkt[hh] * deck_scr[hh]).astype(BF16), rvs[hh])
    gate1 = gate(1)
    zmg = proj(OFF_MG, W_MEM)

    mem_s = []
    for hh in range(H_MEM):
        cols = slice(hh * DH_MEM, (hh + 1) * DH_MEM)
        q = _rms_rows(zmq[:, cols], gmq_ref[...]).astype(BF16)
        mem_s.append(_mm(q, mkt_ref[0, cols, :]) * (DH_MEM ** -0.5))
    r_in = jnp.concatenate(
        [_group_norm(ret_o[hh]) * _silu(zrg[:, hh * DV_RET:(hh + 1) * DV_RET]) for hh in range(H_RET)],
        axis=1).astype(BF16)
    merged = merged + gate1 * _mm(r_in, wpr_ref[...])
    m_parts = []
    for hh in range(H_MEM):
        cols = slice(hh * DH_MEM, (hh + 1) * DH_MEM)
        p = jnp.exp(mem_s[hh] - jnp.max(mem_s[hh], axis=-1, keepdims=True))
        l = jnp.sum(p, axis=-1, keepdims=True)
        m_parts.append(_mm(p.astype(BF16), mvb_ref[0, :, cols]) / l)
    gate2 = gate(2)
    m_in = (jnp.concatenate(m_parts, axis=1) * _silu(zmg)).astype(BF16)
    merged = merged + gate2 * _mm(m_in, wpm_ref[...])

    y_ref[0] = x_ref[0] + _mm(merged.astype(BF16), wo_ref[...])


def _resident(shape):
    nd = len(shape)
    return pl.BlockSpec(shape, lambda b, j: (0,) * nd, pipeline_mode=pl.Buffered(1))


def _prompt_call(x, cos, sin, mkt, mvb, wqkvf_t, bf_col, gfq_col, gfk_col, log_g, g_norm, b_merge, gmq, w_main, wpf,
                 wpr, wpm, wo, tb=PROMPT_BLOCK):
    B, T, _ = x.shape
    nt = T // tb
    kernel = functools.partial(_prompt_kernel, tb=tb)
    in_specs = [
        pl.BlockSpec(memory_space=pltpu.SMEM),
        pl.BlockSpec((1, tb, D_MODEL), lambda b, j: (b, j, 0)),
        pl.BlockSpec((tb, DK_RET), lambda b, j: (j, 0)),
        pl.BlockSpec((tb, DK_RET), lambda b, j: (j, 0)),
        pl.BlockSpec((1, W_MEM, N_MEM), lambda b, j: (b, 0, 0)),
        pl.BlockSpec((1, N_MEM, W_MEM), lambda b, j: (b, 0, 0)),
        _resident((1, D_MODEL)),
        _resident((F_ROWS, 1)),
        _resident((1, N_BRANCH * D_MODEL)),
        _resident((W_FOX, 1)),
        _resident((W_FOX, 1)),
        _resident((1, DH_MEM)),
        _resident((D_MODEL, W_MAIN)),
        _resident((W_FOX, D_MODEL)),
        _resident((W_RET_V, D_MODEL)),
        _resident((W_MEM, D_MODEL)),
        _resident((D_MODEL, D_MODEL)),
        _resident((3 * W_FOX + F_ROWS, D_MODEL)),
    ]
    out_specs = [
        pl.BlockSpec((1, tb, D_MODEL), lambda b, j: (b, j, 0)),
        pl.BlockSpec((1, W_FOX, tb), lambda b, j: (b, 0, j)),
        pl.BlockSpec((1, W_FOX, tb), lambda b, j: (b, 0, j)),
        pl.BlockSpec((1, H_FOX, tb), lambda b, j: (b, 0, j)),
        pl.BlockSpec((1, H_RET, DK_RET, DV_RET), lambda b, j: (b, 0, 0, 0)),
    ]
    out_shape = [
        jax.ShapeDtypeStruct((B, T, D_MODEL), F32),
        jax.ShapeDtypeStruct((B, W_FOX, T), F32),
        jax.ShapeDtypeStruct((B, W_FOX, T), F32),
        jax.ShapeDtypeStruct((B, H_FOX, T), F32),
        jax.ShapeDtypeStruct((B, H_RET, DK_RET, DV_RET), F32),
    ]
    scratch = [
        pltpu.VMEM((H_FOX, T, LANES), BF16),
        pltpu.VMEM((H_FOX, V_ROWS, T), BF16),
        pltpu.VMEM((H_FOX, LANES, tb), BF16),
        pltpu.VMEM((H_FOX, 1, tb), F32),
        pltpu.VMEM((H_FOX, V_ROWS, tb), F32),
        pltpu.VMEM((H_FOX, tb, tb), F32),
        pltpu.VMEM((H_FOX, tb, tb), F32),
        pltpu.VMEM((F_ROWS, 1), F32),
        pltpu.VMEM((H_RET, tb, tb), F32),
        pltpu.VMEM((H_RET, tb, 1), F32),
        pltpu.VMEM((H_RET, 1, tb), F32),
        pltpu.VMEM((H_RET, 1, DV_RET), F32),
    ]
    return pl.pallas_call(
        kernel,
        grid=(B, nt),
        in_specs=in_specs,
        out_specs=out_specs,
        out_shape=out_shape,
        scratch_shapes=scratch,
        compiler_params=pltpu.CompilerParams(
            dimension_semantics=("arbitrary", "arbitrary"), vmem_limit_bytes=VMEM_LIMIT_BYTES),
        name="prompt",
    )(log_g, x, cos, sin, mkt, mvb, g_norm, bf_col, b_merge, gfq_col, gfk_col, gmq, w_main, wpf, wpr, wpm, wo, wqkvf_t)


def _sample_kernel(lg_ref, x_ref, cos_ref, sin_ref, ckt_ref, cvt_ref, clf_ref, st0_ref, cmk_ref, cmv_ref,
                   gn_ref, bf_ref, bm_ref, gfq_ref, gfk_ref, gmq_ref, seg_ref, w_ref, wpf_ref, wpr_ref, wpm_ref,
                   wo_ref, wft_ref,
                   y_ref, fk_ref, fv_ref, lft_ref, st_ref,
                   h_scr, fq_scr, knt_scr, vn_scr, ncn_scr, rq_scr, rkt_scr, rv_scr, mq_scr, fo_scr, ro_scr, mo_scr,
                   *, nb, ts, past):
    b = pl.program_id(0)
    n = nb * ts
    r0 = pl.multiple_of(b * ts, ts)

    @pl.when(b == 0)
    def _project_all():
        h = _rms_rows(x_ref[...], gn_ref[...]).astype(BF16)
        h_scr[...] = h
        fq_scr[...] = _head_norm64(_mm(h, w_ref[:, OFF_FQ:OFF_FQ + W_FOX]), gfq_ref[...], seg_ref) * (DH_FOX ** -0.5)
        fk = _head_norm64(_mm(h, w_ref[:, OFF_FK:OFF_FK + W_FOX]), gfk_ref[...], seg_ref)
        fv = _mm(h, w_ref[:, OFF_FV:OFF_FV + W_FOX])
        logf_t = _log_sigmoid(_mm_nt(wft_ref[...], h) + bf_ref[...])
        fk_ref[...] = fk
        fv_ref[...] = fv
        lft_ref[...] = logf_t[:H_FOX]
        knt_scr[...] = fk.T.astype(BF16)
        vn_scr[...] = fv.astype(BF16)
        ri = lax.broadcasted_iota(jnp.int32, (n, n), 0)
        ci = lax.broadcasted_iota(jnp.int32, (n, n), 1)
        same = _idiv(ri, ts) == _idiv(ci, ts)
        triu = jnp.where(jnp.logical_and(same, ri <= ci), 1.0, 0.0).astype(BF16)
        ncn_scr[...] = -_cumsum_lanes(triu, logf_t)
        cos = cos_ref[...]
        sin = sin_ref[...]
        zrq = _mm(h, w_ref[:, OFF_RQ:OFF_RQ + W_RET_QK])
        zrk = _mm(h, w_ref[:, OFF_RK:OFF_RK + W_RET_QK])
        for hh in range(H_RET):
            cols = slice(hh * DK_RET, (hh + 1) * DK_RET)
            rq_scr[:, cols] = _rope(zrq[:, cols], cos, sin)
            rkt_scr[cols, :] = (_rope(zrk[:, cols], cos, sin) * (DK_RET ** -0.5)).T
        rv_scr[...] = _mm(h, w_ref[:, OFF_RV:OFF_RV + W_RET_V]).astype(BF16)
        zmq = _mm(h, w_ref[:, OFF_MQ:OFF_MQ + W_MEM])
        for hh in range(H_MEM):
            cols = slice(hh * DH_MEM, (hh + 1) * DH_MEM)
            mq_scr[:, cols] = _rms_rows(zmq[:, cols], gmq_ref[...])

    nr = H_FOX * ts
    fq_b = fq_scr[pl.ds(r0, ts), :]
    row_head = _idiv(lax.broadcasted_iota(jnp.int32, (nr, W_FOX), 0), ts)
    col_head = _idiv(lax.broadcasted_iota(jnp.int32, (nr, W_FOX), 1), DH_FOX)
    head_sel = row_head == col_head
    q_bd = jnp.where(head_sel, jnp.concatenate([fq_b] * H_FOX, axis=0), 0.0).astype(BF16)

    cc = 256
    ri = lax.broadcasted_iota(jnp.int32, (cc, cc), 0)
    ci = lax.broadcasted_iota(jnp.int32, (cc, cc), 1)
    triu = jnp.where(ri <= ci, 1.0, 0.0).astype(BF16)
    carry = jnp.zeros((H_FOX, 1), F32)
    c_chunks = []
    for c in range(past // cc):
        cch = _cumsum_lanes(triu, clf_ref[0, :, c * cc:(c + 1) * cc]) + carry
        carry = cch[:, cc - 1:cc]
        c_chunks.append(cch)
    c_past = jnp.concatenate(c_chunks, axis=1)
    ncp = jnp.concatenate([jnp.broadcast_to(-c_past[hh:hh + 1, :], (ts, past)) for hh in range(H_FOX)], axis=0)
    ncn = ncn_scr[:H_FOX, :] - carry
    ncn = jnp.concatenate([jnp.broadcast_to(ncn[hh:hh + 1, :], (ts, n)) for hh in range(H_FOX)], axis=0)

    s_p = _mm(q_bd, ckt_ref[0].astype(BF16)) + ncp
    s_n = _mm(q_bd, knt_scr[...]) + ncn
    tok = lax.broadcasted_iota(jnp.int32, (nr, n), 1)
    qt = _imod(lax.broadcasted_iota(jnp.int32, (nr, n), 0), ts)
    valid = jnp.logical_and(_idiv(tok, ts) == b, _imod(tok, ts) <= qt)
    s_n = jnp.where(valid, s_n, -jnp.inf)
    m = jnp.maximum(jnp.max(s_p, axis=-1, keepdims=True), jnp.max(s_n, axis=-1, keepdims=True))
    p_p = jnp.exp(s_p - m)
    p_n = jnp.exp(s_n - m)
    l = jnp.sum(p_p, axis=-1, keepdims=True) + jnp.sum(p_n, axis=-1, keepdims=True)
    o_past = _mm_nt(p_p.astype(BF16), cvt_ref[0].astype(BF16))
    o_bd = (o_past + _mm(p_n.astype(BF16), vn_scr[...])) / l
    o_bd = jnp.where(head_sel, o_bd, 0.0)
    fo = o_bd[0:ts, :]
    for hh in range(1, H_FOX):
        fo = fo + o_bd[hh * ts:(hh + 1) * ts, :]
    fo_scr[pl.ds(r0, ts), :] = fo

    tokr = lax.broadcasted_iota(jnp.int32, (ts, n), 1)
    qtr = lax.broadcasted_iota(jnp.int32, (ts, n), 0)
    in_b = _idiv(tokr, ts) == b
    dt = (qtr - _imod(tokr, ts)).astype(F32)
    tokc = lax.broadcasted_iota(jnp.int32, (1, n), 1)
    kpos = _imod(tokc, ts).astype(F32)
    rowi = lax.broadcasted_iota(jnp.int32, (ts, 1), 0).astype(F32)
    r_parts = []
    for hh in range(H_RET):
        lg = lg_ref[hh]
        q = rq_scr[pl.ds(r0, ts), hh * DK_RET:(hh + 1) * DK_RET].astype(BF16)
        kt = rkt_scr[hh * DK_RET:(hh + 1) * DK_RET, :]
        v = rv_scr[:, hh * DV_RET:(hh + 1) * DV_RET]
        state = st0_ref[0, hh]
        dmask = jnp.where(jnp.logical_and(in_b, dt >= 0.0), jnp.exp(lg * jnp.maximum(dt, 0.0)), 0.0)
        inter = _mm(q, state.astype(BF16)) * jnp.exp(lg * (rowi + 1.0))
        scores = _mm(q, kt.astype(BF16)) * dmask
        r_parts.append(inter + _mm(scores.astype(BF16), v))
        deck = jnp.where(_idiv(tokc, ts) == b, jnp.exp(lg * (ts - 1.0 - kpos)), 0.0)
        st_ref[0, hh] = jnp.exp(jnp.full((1, DV_RET), lg * ts, F32)) * state + _mm((kt * deck).astype(BF16), v)
    ro_scr[pl.ds(r0, ts), :] = jnp.concatenate(r_parts, axis=1)

    m_parts = []
    for hh in range(H_MEM):
        cols = slice(hh * DH_MEM, (hh + 1) * DH_MEM)
        q = mq_scr[pl.ds(r0, ts), cols].astype(BF16)
        rows = pl.ds(hh, N_MEM, stride=H_MEM)
        s = _mm_nt(q, cmk_ref[0, rows, :].astype(BF16)) * (DH_MEM ** -0.5)
        p = jnp.exp(s - jnp.max(s, axis=-1, keepdims=True))
        l = jnp.sum(p, axis=-1, keepdims=True)
        m_parts.append(_mm(p.astype(BF16), cmv_ref[0, rows, :].astype(BF16)) / l)
    mo_scr[pl.ds(r0, ts), :] = jnp.concatenate(m_parts, axis=1)

    @pl.when(b == nb - 1)
    def _output_all():
        h = h_scr[...]
        a_in = (fo_scr[...] * _silu(_mm(h, w_ref[:, OFF_FG:OFF_FG + W_FOX]))).astype(BF16)
        merged = _gate(h, w_ref, bm_ref, 0) * _mm(a_in, wpf_ref[...])
        zrg = _mm(h, w_ref[:, OFF_RG:OFF_RG + W_RET_V])
        ro = ro_scr[...]
        r_parts = []
        for hh in range(H_RET):
            cols = slice(hh * DV_RET, (hh + 1) * DV_RET)
            r_parts.append(_group_norm(ro[:, cols]) * _silu(zrg[:, cols]))
        r_in = jnp.concatenate(r_parts, axis=1).astype(BF16)
        merged = merged + _gate(h, w_ref, bm_ref, 1) * _mm(r_in, wpr_ref[...])
        m_in = (mo_scr[...] * _silu(_mm(h, w_ref[:, OFF_MG:OFF_MG + W_MEM]))).astype(BF16)
        merged = merged + _gate(h, w_ref, bm_ref, 2) * _mm(m_in, wpm_ref[...])
        y_ref[...] = x_ref[...] + _mm(merged.astype(BF16), wo_ref[...])


def _sample_call(x2d, cos, sin, ckt, cvt, clf_t, st0, cmk, cmv, wf_t, bf_col, log_g, g_norm, b_merge, gfq, gfk, gmq, seg,
                 w_main, wpf, wpr, wpm, wo, nb, ts):
    n = nb * ts
    past = ckt.shape[2]
    kernel = functools.partial(_sample_kernel, nb=nb, ts=ts, past=past)

    def res(shape):
        nd = len(shape)
        return pl.BlockSpec(shape, lambda b: (0,) * nd, pipeline_mode=pl.Buffered(1))

    in_specs = [
        pl.BlockSpec(memory_space=pltpu.SMEM),
        res((n, D_MODEL)),
        res((n, DK_RET)),
        res((n, DK_RET)),
        pl.BlockSpec((1, W_FOX, past), lambda b: (b, 0, 0)),
        pl.BlockSpec((1, W_FOX, past), lambda b: (b, 0, 0)),
        pl.BlockSpec((1, H_FOX, past), lambda b: (b, 0, 0)),
        pl.BlockSpec((1, H_RET, DK_RET, DV_RET), lambda b: (b, 0, 0, 0)),
        pl.BlockSpec((1, N_MEM * H_MEM, DH_MEM), lambda b: (b, 0, 0)),
        pl.BlockSpec((1, N_MEM * H_MEM, DH_MEM), lambda b: (b, 0, 0)),
        res((1, D_MODEL)),
        res((F_ROWS, 1)),
        res((1, N_BRANCH * D_MODEL)),
        res((1, W_FOX)),
        res((1, W_FOX)),
        res((1, DH_MEM)),
        res((MXU_TILE, MXU_TILE)),
        res((D_MODEL, W_MAIN)),
        res((W_FOX, D_MODEL)),
        res((W_RET_V, D_MODEL)),
        res((W_MEM, D_MODEL)),
        res((D_MODEL, D_MODEL)),
        res((F_ROWS, D_MODEL)),
    ]
    full = lambda shape: pl.BlockSpec(shape, lambda b: (0,) * len(shape))
    out_specs = [
        full((n, D_MODEL)),
        full((n, W_FOX)),
        full((n, W_FOX)),
        full((H_FOX, n)),
        pl.BlockSpec((1, H_RET, DK_RET, DV_RET), lambda b: (b, 0, 0, 0)),
    ]
    out_shape = [
        jax.ShapeDtypeStruct((n, D_MODEL), F32),
        jax.ShapeDtypeStruct((n, W_FOX), F32),
        jax.ShapeDtypeStruct((n, W_FOX), F32),
        jax.ShapeDtypeStruct((H_FOX, n), F32),
        jax.ShapeDtypeStruct((nb, H_RET, DK_RET, DV_RET), F32),
    ]
    scratch = [
        pltpu.VMEM((n, D_MODEL), BF16),
        pltpu.VMEM((n, W_FOX), F32),
        pltpu.VMEM((W_FOX, n), BF16),
        pltpu.VMEM((n, W_FOX), BF16),
        pltpu.VMEM((F_ROWS, n), F32),
        pltpu.VMEM((n, W_RET_QK), F32),
        pltpu.VMEM((W_RET_QK, n), F32),
        pltpu.VMEM((n, W_RET_V), BF16),
        pltpu.VMEM((n, W_MEM), F32),
        pltpu.VMEM((n, W_FOX), F32),
        pltpu.VMEM((n, W_RET_V), F32),
        pltpu.VMEM((n, W_MEM), F32),
    ]
    return pl.pallas_call(
        kernel,
        grid=(nb,),
        in_specs=in_specs,
        out_specs=out_specs,
        out_shape=out_shape,
        scratch_shapes=scratch,
        compiler_params=pltpu.CompilerParams(
            dimension_semantics=("arbitrary",), vmem_limit_bytes=VMEM_LIMIT_BYTES),
        name="sample",
    )(log_g, x2d, cos, sin, ckt, cvt, clf_t, st0, cmk, cmv, g_norm, bf_col, b_merge, gfq, gfk, gmq, seg,
      w_main, wpf, wpr, wpm, wo, wf_t)


def _rope_tables(pos):
    half = DK_RET // 2
    inv = ROPE_BASE ** (-jnp.arange(half, dtype=F32) / half)
    ang = pos.astype(F32)[:, None] * inv[None, :]
    cos = jnp.cos(ang)
    sin = jnp.sin(ang)
    return jnp.concatenate([cos, cos], axis=1), jnp.concatenate([-sin, sin], axis=1)


def _reorder_w_in(w_t):
    assert (3 * W_FOX) % REORDER_ROWS == 0 and W_MAIN % REORDER_ROWS == 0 and REORDER_ROWS % H_FOX == 0
    assert w_t.shape == (W_MAIN + H_FOX, D_MODEL)
    return pl.pallas_call(
        _reorder_kernel,
        grid=(W_MAIN // REORDER_ROWS,),
        in_specs=[
            pl.BlockSpec((REORDER_ROWS, D_MODEL), lambda c: (c, 0)),
            pl.BlockSpec((H_FOX, D_MODEL), lambda c: ((c + 1) * (REORDER_ROWS // H_FOX), 0)),
        ],
        out_specs=pl.BlockSpec((D_MODEL, REORDER_ROWS), lambda c: (0, c)),
        out_shape=jax.ShapeDtypeStruct((D_MODEL, W_MAIN), BF16),
        compiler_params=pltpu.CompilerParams(dimension_semantics=("arbitrary",)),
        name="reorder_w_in",
    )(w_t, w_t)


def _reorder_kernel(a_ref, b_ref, o_ref):
    c = pl.program_id(0)
    a = a_ref[...]
    shifted = jnp.concatenate([a[H_FOX:], b_ref[...]], axis=0)
    src = jnp.where(c < (3 * W_FOX) // REORDER_ROWS, a, shifted)
    o_ref[...] = src.T.astype(BF16)


def kernel(x_prompt, x_sample, mem_prompt, cache_fox_k, cache_fox_v, cache_fox_logf, state_ret, cache_mem_k, cache_mem_v,
           g_norm, g_mem_norm, w_in, b_f, b_merge, g_fox_q, g_fox_k, g_mem_q, g_mem_k, w_mem_kv,
           w_p_fox, w_p_ret, w_p_mem, w_out):
    depth = w_in.shape[0]
    assert depth == 1, "single-layer kernel"
    B, T, _ = x_prompt.shape
    nb, ts, _ = x_sample.shape
    past = cache_fox_k.shape[2]

    log_g = jnp.log1p(-jnp.exp2(-5.0 - jnp.arange(H_RET, dtype=F32)))
    head_of_lane = jnp.arange(MXU_TILE) // DH_FOX
    seg = (head_of_lane[:, None] == head_of_lane[None, :]).astype(BF16)
    w_t = jnp.swapaxes(w_in[0], 0, 1)
    w_main = _reorder_w_in(w_t)
    gn = g_norm[0].reshape(1, D_MODEL)
    bm = b_merge[0].reshape(1, N_BRANCH * D_MODEL)
    gfq = jnp.tile(g_fox_q[0], H_FOX).reshape(1, W_FOX)
    gfk = jnp.tile(g_fox_k[0], H_FOX).reshape(1, W_FOX)
    gmq = g_mem_q[0].reshape(1, DH_MEM)
    wpf = w_p_fox[0].astype(BF16)
    wpr = w_p_ret[0].astype(BF16)
    wpm = w_p_mem[0].astype(BF16)
    wo = w_out[0].astype(BF16)
    shared = (gn, bm, gfq, gfk, gmq, seg, w_main, wpf, wpr, wpm, wo)

    mk, mv, mkt, mvb = _memkv_call(mem_prompt, g_mem_norm[0], w_mem_kv[0], g_mem_k[0])

    cos_p, sin_p = _rope_tables(jnp.arange(T, dtype=jnp.int32))
    wqkvf_t = jnp.concatenate(
        [w_t[:3 * W_FOX + H_FOX], jnp.zeros((F_ROWS - H_FOX, D_MODEL), F32)], axis=0).astype(BF16)
    bf_col = jnp.concatenate([b_f[0], jnp.zeros((F_ROWS - H_FOX,), F32)]).reshape(F_ROWS, 1)
    y_p, fkt_p, fvt_p, lft_p, st_p = _prompt_call(
        x_prompt, cos_p, sin_p, mkt, mvb, wqkvf_t, bf_col, gfq.reshape(W_FOX, 1), gfk.reshape(W_FOX, 1), log_g,
        gn, bm, gmq, w_main, wpf, wpr, wpm, wo)

    def time_major(a_t):
        return jnp.transpose(a_t.reshape(a_t.shape[0], H_FOX, DH_FOX, a_t.shape[2]), (0, 3, 1, 2))[None]

    def time_minor(a):
        return jnp.transpose(a, (0, 2, 3, 1)).reshape(a.shape[0], W_FOX, a.shape[1])

    fk_p = time_major(fkt_p)
    fv_p = time_major(fvt_p)
    lf_p = jnp.swapaxes(lft_p, 1, 2)[None]

    pos_s = past + jnp.arange(ts, dtype=jnp.int32)
    cos_s, sin_s = _rope_tables(jnp.tile(pos_s, nb))
    y_s, fk_s, fv_s, lft_s, st_s = _sample_call(
        x_sample.reshape(nb * ts, D_MODEL), cos_s, sin_s,
        time_minor(cache_fox_k[0]), time_minor(cache_fox_v[0]),
        jnp.swapaxes(cache_fox_logf[0], 1, 2), state_ret[0],
        cache_mem_k[0].reshape(nb, N_MEM * H_MEM, DH_MEM), cache_mem_v[0].reshape(nb, N_MEM * H_MEM, DH_MEM),
        wqkvf_t[3 * W_FOX:], bf_col, log_g, *shared, nb=nb, ts=ts)
    lf_s = jnp.transpose(lft_s.reshape(H_FOX, nb, ts), (1, 2, 0))[None]

    return (y_p, y_s.reshape(nb, ts, D_MODEL),
            fk_p, fv_p, lf_p,
            st_p[None], mk.reshape(1, B, N_MEM, H_MEM, DH_MEM), mv.reshape(1, B, N_MEM, H_MEM, DH_MEM),
            fk_s.reshape(1, nb, ts, H_FOX, DH_FOX), fv_s.reshape(1, nb, ts, H_FOX, DH_FOX),
            lf_s, st_s[None])
```

```python
import functools

import jax
import jax.numpy as jnp
from jax import lax
from jax.experimental import pallas as pl
from jax.experimental.pallas import tpu as pltpu

D_MODEL = 1024
N_MEM = 256
H_FOX = 8
DH_FOX = 64
H_RET = 4
DK_RET = 128
DV_RET = 256
H_MEM = 4
DH_MEM = 128
W_FOX = H_FOX * DH_FOX
W_RET_QK = H_RET * DK_RET
W_RET_V = H_RET * DV_RET
W_MEM = H_MEM * DH_MEM
N_BRANCH = 3
ROPE_BASE = 10000.0
EPS = 1e-6
LOG2E = 1.4426950408889634

LANES = 128
MXU_TILE = 256
BF16_SUBLANES = 16
V_ROWS = DH_FOX + BF16_SUBLANES
F_ROWS = BF16_SUBLANES

OFF_FQ = 0
OFF_FK = OFF_FQ + W_FOX
OFF_FV = OFF_FK + W_FOX
OFF_FG = OFF_FV + W_FOX
OFF_RQ = OFF_FG + W_FOX
OFF_RK = OFF_RQ + W_RET_QK
OFF_RV = OFF_RK + W_RET_QK
OFF_RG = OFF_RV + W_RET_V
OFF_MQ = OFF_RG + W_RET_V
OFF_MG = OFF_MQ + W_MEM
OFF_GL = OFF_MG + W_MEM
W_MAIN = OFF_GL + N_BRANCH * D_MODEL
REORDER_ROWS = 1536

PROMPT_BLOCK = 256
VMEM_LIMIT_BYTES = 60 * 1024 * 1024

F32 = jnp.float32
BF16 = jnp.bfloat16


def _mm(a, b):
    return jnp.dot(a, b, preferred_element_type=F32)


def _mm_nt(a, b):
    return lax.dot_general(a, b, (((1,), (1,)), ((), ())), preferred_element_type=F32)


def _idiv(x, d):
    assert d & (d - 1) == 0
    return lax.shift_right_logical(x, d.bit_length() - 1)


def _imod(x, d):
    assert d & (d - 1) == 0
    return jnp.bitwise_and(x, d - 1)


def _rms_rows(x, g):
    ms = jnp.mean(x * x, axis=-1, keepdims=True)
    return x * lax.rsqrt(ms + EPS) * g


def _head_sumsq(z, seg_ref):
    sq = (z * z).astype(BF16)
    return jnp.concatenate(
        [_mm(sq[:, c:c + MXU_TILE], seg_ref[...]) for c in range(0, W_FOX, MXU_TILE)], axis=1)


def _head_norm64(z, g, seg_ref):
    return z * lax.rsqrt(_head_sumsq(z, seg_ref) * (1.0 / DH_FOX) + EPS) * g


def _log_sigmoid(u):
    return jnp.minimum(u, 0.0) - jnp.log1p(jnp.exp(-jnp.abs(u)))


def _silu(u):
    return u * jax.nn.sigmoid(u)


def _split3(a):
    hi = a.astype(BF16)
    r1 = a - hi.astype(F32)
    mid = r1.astype(BF16)
    lo = (r1 - mid.astype(F32)).astype(BF16)
    return hi, mid, lo


def _cumsum_lanes(triu, a):
    hi, mid, lo = _split3(a)
    return _mm(hi, triu) + _mm(mid, triu) + _mm(lo, triu)


def _rope(x, cos, sin_signed):
    return x * cos + pltpu.roll(x, DK_RET // 2, 1) * sin_signed


def _group_norm(o):
    mu = jnp.mean(o, axis=-1, keepdims=True)
    d = o - mu
    var = jnp.mean(d * d, axis=-1, keepdims=True)
    return d * lax.rsqrt(var + EPS)


def _gate(h, w_ref, bm_ref, i):
    z = _mm(h, w_ref[:, OFF_GL + i * D_MODEL:OFF_GL + (i + 1) * D_MODEL])
    return jax.nn.sigmoid(z + bm_ref[:, i * D_MODEL:(i + 1) * D_MODEL])


def _memkv_kernel(mem_ref, gmn_ref, w_ref, gmk_ref, mk_ref, mv_ref, mkt_ref, mvb_ref):
    for i in range(mem_ref.shape[0]):
        h = _rms_rows(mem_ref[i], gmn_ref[...]).astype(BF16)
        z = _mm(h, w_ref[...])
        ks = []
        for hh in range(H_MEM):
            ks.append(_rms_rows(z[:, hh * DH_MEM:(hh + 1) * DH_MEM], gmk_ref[...]))
        mk = jnp.concatenate(ks, axis=1)
        mv = z[:, W_MEM:]
        for hh in range(H_MEM):
            rows = pl.ds(hh, N_MEM, stride=H_MEM)
            mk_ref[i, rows, :] = ks[hh]
            mv_ref[i, rows, :] = mv[:, hh * DH_MEM:(hh + 1) * DH_MEM]
        mkt_ref[i] = mk.T.astype(BF16)
        mvb_ref[i] = mv.astype(BF16)


def _memkv_call(mem, g_mem_norm, w_mem_kv, g_mem_k):
    B = mem.shape[0]
    nbk = 2 if B % 2 == 0 else 1
    const = lambda b: (0, 0)
    per_b = lambda b: (b, 0, 0)
    return pl.pallas_call(
        _memkv_kernel,
        grid=(B // nbk,),
        in_specs=[
            pl.BlockSpec((nbk, N_MEM, D_MODEL), per_b),
            pl.BlockSpec((1, D_MODEL), const),
            pl.BlockSpec((D_MODEL, 2 * W_MEM), const),
            pl.BlockSpec((1, DH_MEM), const),
        ],
        out_specs=[
            pl.BlockSpec((nbk, N_MEM * H_MEM, DH_MEM), per_b),
            pl.BlockSpec((nbk, N_MEM * H_MEM, DH_MEM), per_b),
            pl.BlockSpec((nbk, W_MEM, N_MEM), per_b),
            pl.BlockSpec((nbk, N_MEM, W_MEM), per_b),
        ],
        out_shape=[
            jax.ShapeDtypeStruct((B, N_MEM * H_MEM, DH_MEM), F32),
            jax.ShapeDtypeStruct((B, N_MEM * H_MEM, DH_MEM), F32),
            jax.ShapeDtypeStruct((B, W_MEM, N_MEM), BF16),
            jax.ShapeDtypeStruct((B, N_MEM, W_MEM), BF16),
        ],
        compiler_params=pltpu.CompilerParams(dimension_semantics=("arbitrary",)),
        name="memkv",
    )(mem, g_mem_norm.reshape(1, D_MODEL), w_mem_kv.astype(BF16), g_mem_k.reshape(1, DH_MEM))


def _prompt_kernel(lg_ref, x_ref, cos_ref, sin_ref, mkt_ref, mvb_ref, gn_ref, bf_ref, bm_ref, gfq_ref, gfk_ref,
                   gmq_ref, w_ref, wpf_ref, wpr_ref, wpm_ref, wo_ref, wqkvf_ref,
                   y_ref, fkt_ref, fvt_ref, lft_ref, st_ref,
                   k_scr, vt_scr, qt_scr, m_scr, acc_scr, sa_scr, sb_scr, ccar_scr, dmask_scr, decq_scr, deck_scr,
                   ssc_scr, *, tb):
    b = pl.program_id(0)
    j = pl.program_id(1)
    t0 = pl.multiple_of(j * tb, tb)

    @pl.when(jnp.logical_and(b == 0, j == 0))
    def _init_tables():
        ri = lax.broadcasted_iota(jnp.int32, (tb, tb), 0)
        ci = lax.broadcasted_iota(jnp.int32, (tb, tb), 1)
        diff = (ri - ci).astype(F32)
        rowi = lax.broadcasted_iota(jnp.int32, (tb, 1), 0).astype(F32)
        coli = lax.broadcasted_iota(jnp.int32, (1, tb), 1).astype(F32)
        for hh in range(H_RET):
            lg = lg_ref[hh]
            dmask_scr[hh] = jnp.where(diff >= 0.0, jnp.exp(lg * jnp.maximum(diff, 0.0)), 0.0)
            decq_scr[hh] = jnp.exp(lg * (rowi + 1.0))
            deck_scr[hh] = jnp.exp(lg * (tb - 1.0 - coli))
            ssc_scr[hh] = jnp.exp(jnp.full((1, DV_RET), lg * tb, F32))

    @pl.when(j == 0)
    def _init_batch():
        ccar_scr[...] = jnp.zeros_like(ccar_scr)
        st_ref[...] = jnp.zeros_like(st_ref)

    h = _rms_rows(x_ref[0], gn_ref[...]).astype(BF16)

    def proj(off, width):
        return _mm(h, w_ref[:, off:off + width])

    def gate(i):
        return jax.nn.sigmoid(proj(OFF_GL + i * D_MODEL, D_MODEL) + bm_ref[:, i * D_MODEL:(i + 1) * D_MODEL])


    half = (3 * W_FOX + F_ROWS) // (2 * BF16_SUBLANES) * BF16_SUBLANES
    z_t = jnp.concatenate([_mm_nt(wqkvf_ref[:half, :], h), _mm_nt(wqkvf_ref[half:, :], h)], axis=0)
    zrq = proj(OFF_RQ, W_RET_QK)
    zrk = proj(OFF_RK, W_RET_QK)

    def head_norm_t(zh_t, g_col):
        parts = []
        for hh in range(H_FOX):
            zz = zh_t[hh * DH_FOX:(hh + 1) * DH_FOX]
            ms = jnp.sum(zz * zz, axis=0, keepdims=True) * (1.0 / DH_FOX)
            parts.append(zz * lax.rsqrt(ms + EPS))
        return jnp.concatenate(parts, axis=0) * g_col

    fq_t = head_norm_t(z_t[:W_FOX], gfq_ref[...]) * (DH_FOX ** -0.5 * LOG2E)
    fk_t = head_norm_t(z_t[W_FOX:2 * W_FOX], gfk_ref[...])
    fv_t = z_t[2 * W_FOX:3 * W_FOX]
    logf_t = _log_sigmoid(z_t[3 * W_FOX:] + bf_ref[...])
    ri = lax.broadcasted_iota(jnp.int32, (tb, tb), 0)
    ci = lax.broadcasted_iota(jnp.int32, (tb, tb), 1)
    key_le_query = ri <= ci
    c_t = _cumsum_lanes(jnp.where(key_le_query, 1.0, 0.0).astype(BF16), logf_t) + ccar_scr[...]
    ccar_scr[...] = c_t[:, tb - 1:tb]
    rv = proj(OFF_RV, W_RET_V).astype(BF16)
    fk = fk_t.T
    fkt_ref[0] = fk_t
    fvt_ref[0] = fv_t
    lft_ref[0] = logf_t[:H_FOX]
    hi, mid, lo = _split3(c_t[:H_FOX] * (-LOG2E))
    parts_t = jnp.concatenate(
        [hi.astype(F32), mid.astype(F32), lo.astype(F32), jnp.zeros((LANES - 3 * H_FOX, tb), F32)], axis=0)
    bias = pltpu.roll(parts_t.T, DH_FOX, 1)
    lane = lax.broadcasted_iota(jnp.int32, (tb, LANES), 1)
    vrow = lax.broadcasted_iota(jnp.int32, (V_ROWS - DH_FOX, tb), 0)
    ones_row = jnp.where(vrow == 0, 1.0, 0.0).astype(BF16)
    fv_tb = fv_t.astype(BF16)
    for hh in range(H_FOX):
        vt_scr[hh, :, pl.ds(t0, tb)] = jnp.concatenate([fv_tb[hh * DH_FOX:(hh + 1) * DH_FOX, :], ones_row], axis=0)
    qrow = lax.broadcasted_iota(jnp.int32, (LANES - DH_FOX, tb), 0)
    k_blk = []
    qt_blk = []
    for hh in range(H_FOX):
        pair = fk[:, (hh // 2) * LANES:(hh // 2 + 1) * LANES]
        if hh % 2:
            pair = pltpu.roll(pair, DH_FOX, 1)
        k_aug = jnp.where(lane < DH_FOX, pair, bias).astype(BF16)
        k_scr[hh, pl.ds(t0, tb), :] = k_aug
        k_blk.append(k_aug)
        ones = jnp.where(
            jnp.logical_or(qrow == hh, jnp.logical_or(qrow == H_FOX + hh, qrow == 2 * H_FOX + hh)), 1.0, 0.0)
        qt_aug = jnp.concatenate([fq_t[hh * DH_FOX:(hh + 1) * DH_FOX, :], ones], axis=0).astype(BF16)
        qt_scr[hh] = qt_aug
        qt_blk.append(qt_aug)

    s_all = [_mm(k_blk[hh], qt_blk[hh]) for hh in range(H_FOX)]
    zrg = proj(OFF_RG, W_RET_V)
    p_all = []
    for hh in range(H_FOX):
        s_t = jnp.where(key_le_query, s_all[hh], -jnp.inf)
        m = jnp.max(s_t, axis=0, keepdims=True)
        m_scr[hh] = m
        p_all.append(jnp.exp2((s_t - m).astype(BF16)))
    for hh in range(H_FOX):
        acc_scr[hh] = _mm(vt_scr[hh, :, pl.ds(t0, tb)], p_all[hh])

    def scores(buf, kb, hh):
        off = pl.multiple_of(kb * tb, tb)
        buf[hh] = _mm(k_scr[hh, pl.ds(off, tb), :], qt_scr[hh])

    def softmax(buf, hh):
        s_t = buf[hh]
        m_old = m_scr[hh]
        m_new = jnp.maximum(m_old, jnp.max(s_t, axis=0, keepdims=True))
        m_scr[hh] = m_new
        return hh, jnp.exp2((s_t - m_new).astype(BF16)), jnp.exp2(m_old - m_new)

    def accumulate(kb, pending):
        hh, p_t, alpha = pending
        off = pl.multiple_of(kb * tb, tb)
        acc_scr[hh] = alpha * acc_scr[hh] + _mm(vt_scr[hh, :, pl.ds(off, tb)], p_t)

    def absorb(buf, kb, also=None):
        pending = None
        for hh in range(H_FOX):
            cur = softmax(buf, hh)
            if also is not None:
                also(hh)
            if pending is not None:
                accumulate(kb, pending)
            pending = cur
        accumulate(kb, pending)

    def run_blocks(kb, count):
        bufs = (sa_scr, sb_scr)
        for hh in range(H_FOX):
            scores(bufs[0], kb, hh)
        for c in range(count):
            nxt = None
            if c + 1 < count:
                nxt = functools.partial(scores, bufs[(c + 1) % 2], kb + c + 1)
            absorb(bufs[c % 2], kb + c, also=nxt)

    def kv_quad(i, carry):
        run_blocks(4 * i, 4)
        return carry

    n_quads = lax.shift_right_logical(j, 2)
    lax.fori_loop(0, n_quads, kv_quad, 0)
    for rest in range(1, 4):
        pl.when(jnp.bitwise_and(j, 3) == rest)(functools.partial(run_blocks, 4 * n_quads, rest))

    zfg = proj(OFF_FG, W_FOX)
    gate0 = gate(0)
    fo = jnp.concatenate(
        [acc_scr[hh, :DH_FOX, :] / acc_scr[hh, DH_FOX:DH_FOX + 1, :] for hh in range(H_FOX)], axis=0).T
    a_in = (fo * _silu(zfg)).astype(BF16)

    cos = cos_ref[...]
    sin = sin_ref[...]
    rq, rkt, rvs, states = [], [], [], []
    for hh in range(H_RET):
        cols = slice(hh * DK_RET, (hh + 1) * DK_RET)
        rq.append(_rope(zrq[:, cols], cos, sin).astype(BF16))
        rkt.append((_rope(zrk[:, cols], cos, sin) * (DK_RET ** -0.5)).T)
        rvs.append(rv[:, hh * DV_RET:(hh + 1) * DV_RET])
        states.append(st_ref[0, hh])
    inter = [_mm(rq[hh], states[hh].astype(BF16)) for hh in range(H_RET)]
    scores = [_mm(rq[hh], rkt[hh].astype(BF16)) for hh in range(H_RET)]
    zmq = proj(OFF_MQ, W_MEM)
    merged = gate0 * _mm(a_in, wpf_ref[...])
    ret_o = [inter[hh] * decq_scr[hh] + _mm((scores[hh] * dmask_scr[hh]).astype(BF16), rvs[hh])
             for hh in range(H_RET)]
    for hh in range(H_RET):
        st_ref[0, hh] = ssc_scr[hh] * states[hh] + _mm((rkt[hh] * deck_scr[hh]).astype(BF16), rvs[hh])
    gate1 = gate(1)
    zmg = proj(OFF_MG, W_MEM)

    mem_s = []
    for hh in range(H_MEM):
        cols = slice(hh * DH_MEM, (hh + 1) * DH_MEM)
        q = _rms_rows(zmq[:, cols], gmq_ref[...]).astype(BF16)
        mem_s.append(_mm(q, mkt_ref[0, cols, :]) * (DH_MEM ** -0.5))
    r_in = jnp.concatenate(
        [_group_norm(ret_o[hh]) * _silu(zrg[:, hh * DV_RET:(hh + 1) * DV_RET]) for hh in range(H_RET)],
        axis=1).astype(BF16)
    merged = merged + gate1 * _mm(r_in, wpr_ref[...])
    m_parts = []
    for hh in range(H_MEM):
        cols = slice(hh * DH_MEM, (hh + 1) * DH_MEM)
        p = jnp.exp(mem_s[hh] - jnp.max(mem_s[hh], axis=-1, keepdims=True))
        l = jnp.sum(p, axis=-1, keepdims=True)
        m_parts.append(_mm(p.astype(BF16), mvb_ref[0, :, cols]) / l)
    gate2 = gate(2)
    m_in = (jnp.concatenate(m_parts, axis=1) * _silu(zmg)).astype(BF16)
    merged = merged + gate2 * _mm(m_in, wpm_ref[...])

    y_ref[0] = x_ref[0] + _mm(merged.astype(BF16), wo_ref[...])


def _resident(shape):
    nd = len(shape)
    return pl.BlockSpec(shape, lambda b, j: (0,) * nd, pipeline_mode=pl.Buffered(1))


def _prompt_call(x, cos, sin, mkt, mvb, wqkvf_t, bf_col, gfq_col, gfk_col, log_g, g_norm, b_merge, gmq, w_main, wpf,
                 wpr, wpm, wo, tb=PROMPT_BLOCK):
    B, T, _ = x.shape
    nt = T // tb
    kernel = functools.partial(_prompt_kernel, tb=tb)
    in_specs = [
        pl.BlockSpec(memory_space=pltpu.SMEM),
        pl.BlockSpec((1, tb, D_MODEL), lambda b, j: (b, j, 0)),
        pl.BlockSpec((tb, DK_RET), lambda b, j: (j, 0)),
        pl.BlockSpec((tb, DK_RET), lambda b, j: (j, 0)),
        pl.BlockSpec((1, W_MEM, N_MEM), lambda b, j: (b, 0, 0)),
        pl.BlockSpec((1, N_MEM, W_MEM), lambda b, j: (b, 0, 0)),
        _resident((1, D_MODEL)),
        _resident((F_ROWS, 1)),
        _resident((1, N_BRANCH * D_MODEL)),
        _resident((W_FOX, 1)),
        _resident((W_FOX, 1)),
        _resident((1, DH_MEM)),
        _resident((D_MODEL, W_MAIN)),
        _resident((W_FOX, D_MODEL)),
        _resident((W_RET_V, D_MODEL)),
        _resident((W_MEM, D_MODEL)),
        _resident((D_MODEL, D_MODEL)),
        _resident((3 * W_FOX + F_ROWS, D_MODEL)),
    ]
    out_specs = [
        pl.BlockSpec((1, tb, D_MODEL), lambda b, j: (b, j, 0)),
        pl.BlockSpec((1, W_FOX, tb), lambda b, j: (b, 0, j)),
        pl.BlockSpec((1, W_FOX, tb), lambda b, j: (b, 0, j)),
        pl.BlockSpec((1, H_FOX, tb), lambda b, j: (b, 0, j)),
        pl.BlockSpec((1, H_RET, DK_RET, DV_RET), lambda b, j: (b, 0, 0, 0)),
    ]
    out_shape = [
        jax.ShapeDtypeStruct((B, T, D_MODEL), F32),
        jax.ShapeDtypeStruct((B, W_FOX, T), F32),
        jax.ShapeDtypeStruct((B, W_FOX, T), F32),
        jax.ShapeDtypeStruct((B, H_FOX, T), F32),
        jax.ShapeDtypeStruct((B, H_RET, DK_RET, DV_RET), F32),
    ]
    scratch = [
        pltpu.VMEM((H_FOX, T, LANES), BF16),
        pltpu.VMEM((H_FOX, V_ROWS, T), BF16),
        pltpu.VMEM((H_FOX, LANES, tb), BF16),
        pltpu.VMEM((H_FOX, 1, tb), F32),
        pltpu.VMEM((H_FOX, V_ROWS, tb), F32),
        pltpu.VMEM((H_FOX, tb, tb), F32),
        pltpu.VMEM((H_FOX, tb, tb), F32),
        pltpu.VMEM((F_ROWS, 1), F32),
        pltpu.VMEM((H_RET, tb, tb), F32),
        pltpu.VMEM((H_RET, tb, 1), F32),
        pltpu.VMEM((H_RET, 1, tb), F32),
        pltpu.VMEM((H_RET, 1, DV_RET), F32),
    ]
    return pl.pallas_call(
        kernel,
        grid=(B, nt),
        in_specs=in_specs,
        out_specs=out_specs,
        out_shape=out_shape,
        scratch_shapes=scratch,
        compiler_params=pltpu.CompilerParams(
            dimension_semantics=("arbitrary", "arbitrary"), vmem_limit_bytes=VMEM_LIMIT_BYTES),
        name="prompt",
    )(log_g, x, cos, sin, mkt, mvb, g_norm, bf_col, b_merge, gfq_col, gfk_col, gmq, w_main, wpf, wpr, wpm, wo, wqkvf_t)


def _sample_kernel(lg_ref, x_ref, cos_ref, sin_ref, ckt_ref, cvt_ref, clf_ref, st0_ref, cmk_ref, cmv_ref,
                   gn_ref, bf_ref, bm_ref, gfq_ref, gfk_ref, gmq_ref, seg_ref, w_ref, wpf_ref, wpr_ref, wpm_ref,
                   wo_ref, wft_ref,
                   y_ref, fk_ref, fv_ref, lft_ref, st_ref,
                   h_scr, fq_scr, knt_scr, vn_scr, ncn_scr, rq_scr, rkt_scr, rv_scr, mq_scr, fo_scr, ro_scr, mo_scr,
                   *, nb, ts, past):
    b = pl.program_id(0)
    n = nb * ts
    r0 = pl.multiple_of(b * ts, ts)

    @pl.when(b == 0)
    def _project_all():
        h = _rms_rows(x_ref[...], gn_ref[...]).astype(BF16)
        h_scr[...] = h
        fq_scr[...] = _head_norm64(_mm(h, w_ref[:, OFF_FQ:OFF_FQ + W_FOX]), gfq_ref[...], seg_ref) * (DH_FOX ** -0.5)
        fk = _head_norm64(_mm(h, w_ref[:, OFF_FK:OFF_FK + W_FOX]), gfk_ref[...], seg_ref)
        fv = _mm(h, w_ref[:, OFF_FV:OFF_FV + W_FOX])
        logf_t = _log_sigmoid(_mm_nt(wft_ref[...], h) + bf_ref[...])
        fk_ref[...] = fk
        fv_ref[...] = fv
        lft_ref[...] = logf_t[:H_FOX]
        knt_scr[...] = fk.T.astype(BF16)
        vn_scr[...] = fv.astype(BF16)
        ri = lax.broadcasted_iota(jnp.int32, (n, n), 0)
        ci = lax.broadcasted_iota(jnp.int32, (n, n), 1)
        same = _idiv(ri, ts) == _idiv(ci, ts)
        triu = jnp.where(jnp.logical_and(same, ri <= ci), 1.0, 0.0).astype(BF16)
        ncn_scr[...] = -_cumsum_lanes(triu, logf_t)
        cos = cos_ref[...]
        sin = sin_ref[...]
        zrq = _mm(h, w_ref[:, OFF_RQ:OFF_RQ + W_RET_QK])
        zrk = _mm(h, w_ref[:, OFF_RK:OFF_RK + W_RET_QK])
        for hh in range(H_RET):
            cols = slice(hh * DK_RET, (hh + 1) * DK_RET)
            rq_scr[:, cols] = _rope(zrq[:, cols], cos, sin)
            rkt_scr[cols, :] = (_rope(zrk[:, cols], cos, sin) * (DK_RET ** -0.5)).T
        rv_scr[...] = _mm(h, w_ref[:, OFF_RV:OFF_RV + W_RET_V]).astype(BF16)
        zmq = _mm(h, w_ref[:, OFF_MQ:OFF_MQ + W_MEM])
        for hh in range(H_MEM):
            cols = slice(hh * DH_MEM, (hh + 1) * DH_MEM)
            mq_scr[:, cols] = _rms_rows(zmq[:, cols], gmq_ref[...])

    nr = H_FOX * ts
    fq_b = fq_scr[pl.ds(r0, ts), :]
    row_head = _idiv(lax.broadcasted_iota(jnp.int32, (nr, W_FOX), 0), ts)
    col_head = _idiv(lax.broadcasted_iota(jnp.int32, (nr, W_FOX), 1), DH_FOX)
    head_sel = row_head == col_head
    q_bd = jnp.where(head_sel, jnp.concatenate([fq_b] * H_FOX, axis=0), 0.0).astype(BF16)

    cc = 256
    ri = lax.broadcasted_iota(jnp.int32, (cc, cc), 0)
    ci = lax.broadcasted_iota(jnp.int32, (cc, cc), 1)
    triu = jnp.where(ri <= ci, 1.0, 0.0).astype(BF16)
    carry = jnp.zeros((H_FOX, 1), F32)
    c_chunks = []
    for c in range(past // cc):
        cch = _cumsum_lanes(triu, clf_ref[0, :, c * cc:(c + 1) * cc]) + carry
        carry = cch[:, cc - 1:cc]
        c_chunks.append(cch)
    c_past = jnp.concatenate(c_chunks, axis=1)
    ncp = jnp.concatenate([jnp.broadcast_to(-c_past[hh:hh + 1, :], (ts, past)) for hh in range(H_FOX)], axis=0)
    ncn = ncn_scr[:H_FOX, :] - carry
    ncn = jnp.concatenate([jnp.broadcast_to(ncn[hh:hh + 1, :], (ts, n)) for hh in range(H_FOX)], axis=0)

    s_p = _mm(q_bd, ckt_ref[0].astype(BF16)) + ncp
    s_n = _mm(q_bd, knt_scr[...]) + ncn
    tok = lax.broadcasted_iota(jnp.int32, (nr, n), 1)
    qt = _imod(lax.broadcasted_iota(jnp.int32, (nr, n), 0), ts)
    valid = jnp.logical_and(_idiv(tok, ts) == b, _imod(tok, ts) <= qt)
    s_n = jnp.where(valid, s_n, -jnp.inf)
    m = jnp.maximum(jnp.max(s_p, axis=-1, keepdims=True), jnp.max(s_n, axis=-1, keepdims=True))
    p_p = jnp.exp(s_p - m)
    p_n = jnp.exp(s_n - m)
    l = jnp.sum(p_p, axis=-1, keepdims=True) + jnp.sum(p_n, axis=-1, keepdims=True)
    o_past = _mm_nt(p_p.astype(BF16), cvt_ref[0].astype(BF16))
    o_bd = (o_past + _mm(p_n.astype(BF16), vn_scr[...])) / l
    o_bd = jnp.where(head_sel, o_bd, 0.0)
    fo = o_bd[0:ts, :]
    for hh in range(1, H_FOX):
        fo = fo + o_bd[hh * ts:(hh + 1) * ts, :]
    fo_scr[pl.ds(r0, ts), :] = fo

    tokr = lax.broadcasted_iota(jnp.int32, (ts, n), 1)
    qtr = lax.broadcasted_iota(jnp.int32, (ts, n), 0)
    in_b = _idiv(tokr, ts) == b
    dt = (qtr - _imod(tokr, ts)).astype(F32)
    tokc = lax.broadcasted_iota(jnp.int32, (1, n), 1)
    kpos = _imod(tokc, ts).astype(F32)
    rowi = lax.broadcasted_iota(jnp.int32, (ts, 1), 0).astype(F32)
    r_parts = []
    for hh in range(H_RET):
        lg = lg_ref[hh]
        q = rq_scr[pl.ds(r0, ts), hh * DK_RET:(hh + 1) * DK_RET].astype(BF16)
        kt = rkt_scr[hh * DK_RET:(hh + 1) * DK_RET, :]
        v = rv_scr[:, hh * DV_RET:(hh + 1) * DV_RET]
        state = st0_ref[0, hh]
        dmask = jnp.where(jnp.logical_and(in_b, dt >= 0.0), jnp.exp(lg * jnp.maximum(dt, 0.0)), 0.0)
        inter = _mm(q, state.astype(BF16)) * jnp.exp(lg * (rowi + 1.0))
        scores = _mm(q, kt.astype(BF16)) * dmask
        r_parts.append(inter + _mm(scores.astype(BF16), v))
        deck = jnp.where(_idiv(tokc, ts) == b, jnp.exp(lg * (ts - 1.0 - kpos)), 0.0)
        st_ref[0, hh] = jnp.exp(jnp.full((1, DV_RET), lg * ts, F32)) * state + _mm((kt * deck).astype(BF16), v)
    ro_scr[pl.ds(r0, ts), :] = jnp.concatenate(r_parts, axis=1)

    m_parts = []
    for hh in range(H_MEM):
        cols = slice(hh * DH_MEM, (hh + 1) * DH_MEM)
        q = mq_scr[pl.ds(r0, ts), cols].astype(BF16)
        rows = pl.ds(hh, N_MEM, stride=H_MEM)
        s = _mm_nt(q, cmk_ref[0, rows, :].astype(BF16)) * (DH_MEM ** -0.5)
        p = jnp.exp(s - jnp.max(s, axis=-1, keepdims=True))
        l = jnp.sum(p, axis=-1, keepdims=True)
        m_parts.append(_mm(p.astype(BF16), cmv_ref[0, rows, :].astype(BF16)) / l)
    mo_scr[pl.ds(r0, ts), :] = jnp.concatenate(m_parts, axis=1)

    @pl.when(b == nb - 1)
    def _output_all():
        h = h_scr[...]
        a_in = (fo_scr[...] * _silu(_mm(h, w_ref[:, OFF_FG:OFF_FG + W_FOX]))).astype(BF16)
        merged = _gate(h, w_ref, bm_ref, 0) * _mm(a_in, wpf_ref[...])
        zrg = _mm(h, w_ref[:, OFF_RG:OFF_RG + W_RET_V])
        ro = ro_scr[...]
        r_parts = []
        for hh in range(H_RET):
            cols = slice(hh * DV_RET, (hh + 1) * DV_RET)
            r_parts.append(_group_norm(ro[:, cols]) * _silu(zrg[:, cols]))
        r_in = jnp.concatenate(r_parts, axis=1).astype(BF16)
        merged = merged + _gate(h, w_ref, bm_ref, 1) * _mm(r_in, wpr_ref[...])
        m_in = (mo_scr[...] * _silu(_mm(h, w_ref[:, OFF_MG:OFF_MG + W_MEM]))).astype(BF16)
        merged = merged + _gate(h, w_ref, bm_ref, 2) * _mm(m_in, wpm_ref[...])
        y_ref[...] = x_ref[...] + _mm(merged.astype(BF16), wo_ref[...])


def _sample_call(x2d, cos, sin, ckt, cvt, clf_t, st0, cmk, cmv, wf_t, bf_col, log_g, g_norm, b_merge, gfq, gfk, gmq, seg,
                 w_main, wpf, wpr, wpm, wo, nb, ts):
    n = nb * ts
    past = ckt.shape[2]
    kernel = functools.partial(_sample_kernel, nb=nb, ts=ts, past=past)

    def res(shape):
        nd = len(shape)
        return pl.BlockSpec(shape, lambda b: (0,) * nd, pipeline_mode=pl.Buffered(1))

    in_specs = [
        pl.BlockSpec(memory_space=pltpu.SMEM),
        res((n, D_MODEL)),
        res((n, DK_RET)),
        res((n, DK_RET)),
        pl.BlockSpec((1, W_FOX, past), lambda b: (b, 0, 0)),
        pl.BlockSpec((1, W_FOX, past), lambda b: (b, 0, 0)),
        pl.BlockSpec((1, H_FOX, past), lambda b: (b, 0, 0)),
        pl.BlockSpec((1, H_RET, DK_RET, DV_RET), lambda b: (b, 0, 0, 0)),
        pl.BlockSpec((1, N_MEM * H_MEM, DH_MEM), lambda b: (b, 0, 0)),
        pl.BlockSpec((1, N_MEM * H_MEM, DH_MEM), lambda b: (b, 0, 0)),
        res((1, D_MODEL)),
        res((F_ROWS, 1)),
        res((1, N_BRANCH * D_MODEL)),
        res((1, W_FOX)),
        res((1, W_FOX)),
        res((1, DH_MEM)),
        res((MXU_TILE, MXU_TILE)),
        res((D_MODEL, W_MAIN)),
        res((W_FOX, D_MODEL)),
        res((W_RET_V, D_MODEL)),
        res((W_MEM, D_MODEL)),
        res((D_MODEL, D_MODEL)),
        res((F_ROWS, D_MODEL)),
    ]
    full = lambda shape: pl.BlockSpec(shape, lambda b: (0,) * len(shape))
    out_specs = [
        full((n, D_MODEL)),
        full((n, W_FOX)),
        full((n, W_FOX)),
        full((H_FOX, n)),
        pl.BlockSpec((1, H_RET, DK_RET, DV_RET), lambda b: (b, 0, 0, 0)),
    ]
    out_shape = [
        jax.ShapeDtypeStruct((n, D_MODEL), F32),
        jax.ShapeDtypeStruct((n, W_FOX), F32),
        jax.ShapeDtypeStruct((n, W_FOX), F32),
        jax.ShapeDtypeStruct((H_FOX, n), F32),
        jax.ShapeDtypeStruct((nb, H_RET, DK_RET, DV_RET), F32),
    ]
    scratch = [
        pltpu.VMEM((n, D_MODEL), BF16),
        pltpu.VMEM((n, W_FOX), F32),
        pltpu.VMEM((W_FOX, n), BF16),
        pltpu.VMEM((n, W_FOX), BF16),
        pltpu.VMEM((F_ROWS, n), F32),
        pltpu.VMEM((n, W_RET_QK), F32),
        pltpu.VMEM((W_RET_QK, n), F32),
        pltpu.VMEM((n, W_RET_V), BF16),
        pltpu.VMEM((n, W_MEM), F32),
        pltpu.VMEM((n, W_FOX), F32),
        pltpu.VMEM((n, W_RET_V), F32),
        pltpu.VMEM((n, W_MEM), F32),
    ]
    return pl.pallas_call(
        kernel,
        grid=(nb,),
        in_specs=in_specs,
        out_specs=out_specs,
        out_shape=out_shape,
        scratch_shapes=scratch,
        compiler_params=pltpu.CompilerParams(
            dimension_semantics=("arbitrary",), vmem_limit_bytes=VMEM_LIMIT_BYTES),
        name="sample",
    )(log_g, x2d, cos, sin, ckt, cvt, clf_t, st0, cmk, cmv, g_norm, bf_col, b_merge, gfq, gfk, gmq, seg,
      w_main, wpf, wpr, wpm, wo, wf_t)


def _rope_tables(pos):
    half = DK_RET // 2
    inv = ROPE_BASE ** (-jnp.arange(half, dtype=F32) / half)
    ang = pos.astype(F32)[:, None] * inv[None, :]
    cos = jnp.cos(ang)
    sin = jnp.sin(ang)
    return jnp.concatenate([cos, cos], axis=1), jnp.concatenate([-sin, sin], axis=1)


def _reorder_w_in(w_t):
    assert (3 * W_FOX) % REORDER_ROWS == 0 and W_MAIN % REORDER_ROWS == 0 and REORDER_ROWS % H_FOX == 0
    assert w_t.shape == (W_MAIN + H_FOX, D_MODEL)
    return pl.pallas_call(
        _reorder_kernel,
        grid=(W_MAIN // REORDER_ROWS,),
        in_specs=[
            pl.BlockSpec((REORDER_ROWS, D_MODEL), lambda c: (c, 0)),
            pl.BlockSpec((H_FOX, D_MODEL), lambda c: ((c + 1) * (REORDER_ROWS // H_FOX), 0)),
        ],
        out_specs=pl.BlockSpec((D_MODEL, REORDER_ROWS), lambda c: (0, c)),
        out_shape=jax.ShapeDtypeStruct((D_MODEL, W_MAIN), BF16),
        compiler_params=pltpu.CompilerParams(
            dimension_semantics=("arbitrary",), vmem_limit_bytes=VMEM_LIMIT_BYTES),
        name="reorder_w_in",
    )(w_t, w_t)


def _reorder_kernel(a_ref, b_ref, o_ref):
    c = pl.program_id(0)
    a = a_ref[...]
    shifted = jnp.concatenate([a[H_FOX:], b_ref[...]], axis=0)
    src = jnp.where(c < (3 * W_FOX) // REORDER_ROWS, a, shifted)
    o_ref[...] = src.T.astype(BF16)


def kernel(x_prompt, x_sample, mem_prompt, cache_fox_k, cache_fox_v, cache_fox_logf, state_ret, cache_mem_k, cache_mem_v,
           g_norm, g_mem_norm, w_in, b_f, b_merge, g_fox_q, g_fox_k, g_mem_q, g_mem_k, w_mem_kv,
           w_p_fox, w_p_ret, w_p_mem, w_out):
    depth = w_in.shape[0]
    assert depth == 1, "single-layer kernel"
    B, T, _ = x_prompt.shape
    nb, ts, _ = x_sample.shape
    past = cache_fox_k.shape[2]

    log_g = jnp.log1p(-jnp.exp2(-5.0 - jnp.arange(H_RET, dtype=F32)))
    head_of_lane = jnp.arange(MXU_TILE) // DH_FOX
    seg = (head_of_lane[:, None] == head_of_lane[None, :]).astype(BF16)
    w_t = jnp.swapaxes(w_in[0], 0, 1)
    w_main = _reorder_w_in(w_t)
    gn = g_norm[0].reshape(1, D_MODEL)
    bm = b_merge[0].reshape(1, N_BRANCH * D_MODEL)
    gfq = jnp.tile(g_fox_q[0], H_FOX).reshape(1, W_FOX)
    gfk = jnp.tile(g_fox_k[0], H_FOX).reshape(1, W_FOX)
    gmq = g_mem_q[0].reshape(1, DH_MEM)
    wpf = w_p_fox[0].astype(BF16)
    wpr = w_p_ret[0].astype(BF16)
    wpm = w_p_mem[0].astype(BF16)
    wo = w_out[0].astype(BF16)
    shared = (gn, bm, gfq, gfk, gmq, seg, w_main, wpf, wpr, wpm, wo)

    mk, mv, mkt, mvb = _memkv_call(mem_prompt, g_mem_norm[0], w_mem_kv[0], g_mem_k[0])

    cos_p, sin_p = _rope_tables(jnp.arange(T, dtype=jnp.int32))
    wqkvf_t = jnp.concatenate(
        [w_t[:3 * W_FOX + H_FOX], jnp.zeros((F_ROWS - H_FOX, D_MODEL), F32)], axis=0).astype(BF16)
    bf_col = jnp.concatenate([b_f[0], jnp.zeros((F_ROWS - H_FOX,), F32)]).reshape(F_ROWS, 1)
    y_p, fkt_p, fvt_p, lft_p, st_p = _prompt_call(
        x_prompt, cos_p, sin_p, mkt, mvb, wqkvf_t, bf_col, gfq.reshape(W_FOX, 1), gfk.reshape(W_FOX, 1), log_g,
        gn, bm, gmq, w_main, wpf, wpr, wpm, wo)

    def time_major(a_t):
        return jnp.transpose(a_t.reshape(a_t.shape[0], H_FOX, DH_FOX, a_t.shape[2]), (0, 3, 1, 2))[None]

    def time_minor(a):
        return jnp.transpose(a, (0, 2, 3, 1)).reshape(a.shape[0], W_FOX, a.shape[1])

    fk_p = time_major(fkt_p)
    fv_p = time_major(fvt_p)
    lf_p = jnp.swapaxes(lft_p, 1, 2)[None]

    pos_s = past + jnp.arange(ts, dtype=jnp.int32)
    cos_s, sin_s = _rope_tables(jnp.tile(pos_s, nb))
    y_s, fk_s, fv_s, lft_s, st_s = _sample_call(
        x_sample.reshape(nb * ts, D_MODEL), cos_s, sin_s,
        time_minor(cache_fox_k[0]), time_minor(cache_fox_v[0]),
        jnp.swapaxes(cache_fox_logf[0], 1, 2), state_ret[0],
        cache_mem_k[0].reshape(nb, N_MEM * H_MEM, DH_MEM), cache_mem_v[0].reshape(nb, N_MEM * H_MEM, DH_MEM),
        wqkvf_t[3 * W_FOX:], bf_col, log_g, *shared, nb=nb, ts=ts)
    lf_s = jnp.transpose(lft_s.reshape(H_FOX, nb, ts), (1, 2, 0))[None]

    return (y_p, y_s.reshape(nb, ts, D_MODEL),
            fk_p, fv_p, lf_p,
            st_p[None], mk.reshape(1, B, N_MEM, H_MEM, DH_MEM), mv.reshape(1, B, N_MEM, H_MEM, DH_MEM),
            fk_s.reshape(1, nb, ts, H_FOX, DH_FOX), fv_s.reshape(1, nb, ts, H_FOX, DH_FOX),
            lf_s, st_s[None])
```

```python
import functools

import jax
import jax.numpy as jnp
from jax import lax
from jax.experimental import pallas as pl
from jax.experimental.pallas import tpu as pltpu

D_MODEL = 1024
N_MEM = 256
H_FOX = 8
DH_FOX = 64
H_RET = 4
DK_RET = 128
DV_RET = 256
H_MEM = 4
DH_MEM = 128
W_FOX = H_FOX * DH_FOX
W_RET_QK = H_RET * DK_RET
W_RET_V = H_RET * DV_RET
W_MEM = H_MEM * DH_MEM
N_BRANCH = 3
ROPE_BASE = 10000.0
EPS = 1e-6
LOG2E = 1.4426950408889634

LANES = 128
MXU_TILE = 256
BF16_SUBLANES = 16
V_ROWS = DH_FOX + BF16_SUBLANES
F_ROWS = BF16_SUBLANES

OFF_FQ = 0
OFF_FK = OFF_FQ + W_FOX
OFF_FV = OFF_FK + W_FOX
OFF_FG = OFF_FV + W_FOX
OFF_RQ = OFF_FG + W_FOX
OFF_RK = OFF_RQ + W_RET_QK
OFF_RV = OFF_RK + W_RET_QK
OFF_RG = OFF_RV + W_RET_V
OFF_MQ = OFF_RG + W_RET_V
OFF_MG = OFF_MQ + W_MEM
OFF_GL = OFF_MG + W_MEM
W_MAIN = OFF_GL + N_BRANCH * D_MODEL
REORDER_ROWS = 1536

PROMPT_BLOCK = 256
VMEM_LIMIT_BYTES = 60 * 1024 * 1024

F32 = jnp.float32
BF16 = jnp.bfloat16


def _mm(a, b):
    return jnp.dot(a, b, preferred_element_type=F32)


def _mm_nt(a, b):
    return lax.dot_general(a, b, (((1,), (1,)), ((), ())), preferred_element_type=F32)


def _idiv(x, d):
    assert d & (d - 1) == 0
    return lax.shift_right_logical(x, d.bit_length() - 1)


def _imod(x, d):
    assert d & (d - 1) == 0
    return jnp.bitwise_and(x, d - 1)


def _rms_rows(x, g):
    ms = jnp.mean(x * x, axis=-1, keepdims=True)
    return x * lax.rsqrt(ms + EPS) * g


def _head_sumsq(z, seg_ref):
    sq = (z * z).astype(BF16)
    return jnp.concatenate(
        [_mm(sq[:, c:c + MXU_TILE], seg_ref[...]) for c in range(0, W_FOX, MXU_TILE)], axis=1)


def _head_norm64(z, g, seg_ref):
    return z * lax.rsqrt(_head_sumsq(z, seg_ref) * (1.0 / DH_FOX) + EPS) * g


def _log_sigmoid(u):
    return jnp.minimum(u, 0.0) - jnp.log1p(jnp.exp(-jnp.abs(u)))


def _silu(u):
    return u * jax.nn.sigmoid(u)


def _split3(a):
    hi = a.astype(BF16)
    r1 = a - hi.astype(F32)
    mid = r1.astype(BF16)
    lo = (r1 - mid.astype(F32)).astype(BF16)
    return hi, mid, lo


def _cumsum_lanes(triu, a):
    hi, mid, lo = _split3(a)
    return _mm(hi, triu) + _mm(mid, triu) + _mm(lo, triu)


def _rope(x, cos, sin_signed):
    return x * cos + pltpu.roll(x, DK_RET // 2, 1) * sin_signed


def _group_norm(o):
    mu = jnp.mean(o, axis=-1, keepdims=True)
    d = o - mu
    var = jnp.mean(d * d, axis=-1, keepdims=True)
    return d * lax.rsqrt(var + EPS)


def _gate(h, w_ref, bm_ref, i):
    z = _mm(h, w_ref[:, OFF_GL + i * D_MODEL:OFF_GL + (i + 1) * D_MODEL])
    return jax.nn.sigmoid(z + bm_ref[:, i * D_MODEL:(i + 1) * D_MODEL])


def _memkv_kernel(mem_ref, gmn_ref, w_ref, gmk_ref, mk_ref, mv_ref, mkt_ref, mvb_ref):
    for i in range(mem_ref.shape[0]):
        h = _rms_rows(mem_ref[i], gmn_ref[...]).astype(BF16)
        z = _mm(h, w_ref[...])
        ks = []
        for hh in range(H_MEM):
            ks.append(_rms_rows(z[:, hh * DH_MEM:(hh + 1) * DH_MEM], gmk_ref[...]))
        mk = jnp.concatenate(ks, axis=1)
        mv = z[:, W_MEM:]
        for hh in range(H_MEM):
            rows = pl.ds(hh, N_MEM, stride=H_MEM)
            mk_ref[i, rows, :] = ks[hh]
            mv_ref[i, rows, :] = mv[:, hh * DH_MEM:(hh + 1) * DH_MEM]
        mkt_ref[i] = mk.T.astype(BF16)
        mvb_ref[i] = mv.astype(BF16)


def _memkv_call(mem, g_mem_norm, w_mem_kv, g_mem_k):
    B = mem.shape[0]
    nbk = 2 if B % 2 == 0 else 1
    const = lambda b: (0, 0)
    per_b = lambda b: (b, 0, 0)
    return pl.pallas_call(
        _memkv_kernel,
        grid=(B // nbk,),
        in_specs=[
            pl.BlockSpec((nbk, N_MEM, D_MODEL), per_b),
            pl.BlockSpec((1, D_MODEL), const),
            pl.BlockSpec((D_MODEL, 2 * W_MEM), const),
            pl.BlockSpec((1, DH_MEM), const),
        ],
        out_specs=[
            pl.BlockSpec((nbk, N_MEM * H_MEM, DH_MEM), per_b),
            pl.BlockSpec((nbk, N_MEM * H_MEM, DH_MEM), per_b),
            pl.BlockSpec((nbk, W_MEM, N_MEM), per_b),
            pl.BlockSpec((nbk, N_MEM, W_MEM), per_b),
        ],
        out_shape=[
            jax.ShapeDtypeStruct((B, N_MEM * H_MEM, DH_MEM), F32),
            jax.ShapeDtypeStruct((B, N_MEM * H_MEM, DH_MEM), F32),
            jax.ShapeDtypeStruct((B, W_MEM, N_MEM), BF16),
            jax.ShapeDtypeStruct((B, N_MEM, W_MEM), BF16),
        ],
        compiler_params=pltpu.CompilerParams(dimension_semantics=("arbitrary",)),
        name="memkv",
    )(mem, g_mem_norm.reshape(1, D_MODEL), w_mem_kv.astype(BF16), g_mem_k.reshape(1, DH_MEM))


def _prompt_kernel(lg_ref, x_ref, cos_ref, sin_ref, mkt_ref, mvb_ref, gn_ref, bf_ref, bm_ref, gfq_ref, gfk_ref,
                   gmq_ref, w_ref, wpf_ref, wpr_ref, wpm_ref, wo_ref, wqkvf_ref,
                   y_ref, fkt_ref, fvt_ref, lft_ref, st_ref,
                   k_scr, vt_scr, qt_scr, m_scr, acc_scr, sa_scr, sb_scr, ccar_scr, dmask_scr, decq_scr, deck_scr,
                   ssc_scr, *, tb):
    b = pl.program_id(0)
    j = pl.program_id(1)
    t0 = pl.multiple_of(j * tb, tb)

    @pl.when(jnp.logical_and(b == 0, j == 0))
    def _init_tables():
        ri = lax.broadcasted_iota(jnp.int32, (tb, tb), 0)
        ci = lax.broadcasted_iota(jnp.int32, (tb, tb), 1)
        diff = (ri - ci).astype(F32)
        rowi = lax.broadcasted_iota(jnp.int32, (tb, 1), 0).astype(F32)
        coli = lax.broadcasted_iota(jnp.int32, (1, tb), 1).astype(F32)
        for hh in range(H_RET):
            lg = lg_ref[hh]
            dmask_scr[hh] = jnp.where(diff >= 0.0, jnp.exp(lg * jnp.maximum(diff, 0.0)), 0.0)
            decq_scr[hh] = jnp.exp(lg * (rowi + 1.0))
            deck_scr[hh] = jnp.exp(lg * (tb - 1.0 - coli))
            ssc_scr[hh] = jnp.exp(jnp.full((1, DV_RET), lg * tb, F32))

    @pl.when(j == 0)
    def _init_batch():
        ccar_scr[...] = jnp.zeros_like(ccar_scr)
        st_ref[...] = jnp.zeros_like(st_ref)

    h = _rms_rows(x_ref[0], gn_ref[...]).astype(BF16)

    def proj(off, width):
        return _mm(h, w_ref[:, off:off + width])

    def gate(i):
        return jax.nn.sigmoid(proj(OFF_GL + i * D_MODEL, D_MODEL) + bm_ref[:, i * D_MODEL:(i + 1) * D_MODEL])


    half = (3 * W_FOX + F_ROWS) // (2 * BF16_SUBLANES) * BF16_SUBLANES
    z_t = jnp.concatenate([_mm_nt(wqkvf_ref[:half, :], h), _mm_nt(wqkvf_ref[half:, :], h)], axis=0)
    zrq = proj(OFF_RQ, W_RET_QK)
    zrk = proj(OFF_RK, W_RET_QK)

    def head_norm_t(zh_t, g_col):
        parts = []
        for hh in range(H_FOX):
            zz = zh_t[hh * DH_FOX:(hh + 1) * DH_FOX]
            ms = jnp.sum(zz * zz, axis=0, keepdims=True) * (1.0 / DH_FOX)
            parts.append(zz * lax.rsqrt(ms + EPS))
        return jnp.concatenate(parts, axis=0) * g_col

    fq_t = head_norm_t(z_t[:W_FOX], gfq_ref[...]) * (DH_FOX ** -0.5 * LOG2E)
    fk_t = head_norm_t(z_t[W_FOX:2 * W_FOX], gfk_ref[...])
    fv_t = z_t[2 * W_FOX:3 * W_FOX]
    logf_t = _log_sigmoid(z_t[3 * W_FOX:] + bf_ref[...])
    ri = lax.broadcasted_iota(jnp.int32, (tb, tb), 0)
    ci = lax.broadcasted_iota(jnp.int32, (tb, tb), 1)
    key_le_query = ri <= ci
    c_t = _cumsum_lanes(jnp.where(key_le_query, 1.0, 0.0).astype(BF16), logf_t) + ccar_scr[...]
    ccar_scr[...] = c_t[:, tb - 1:tb]
    rv = proj(OFF_RV, W_RET_V).astype(BF16)
    fk = fk_t.T
    fkt_ref[0] = fk_t
    fvt_ref[0] = fv_t
    lft_ref[0] = logf_t[:H_FOX]
    hi, mid, lo = _split3(c_t[:H_FOX] * (-LOG2E))
    parts_t = jnp.concatenate(
        [hi.astype(F32), mid.astype(F32), lo.astype(F32), jnp.zeros((LANES - 3 * H_FOX, tb), F32)], axis=0)
    bias = pltpu.roll(parts_t.T, DH_FOX, 1)
    lane = lax.broadcasted_iota(jnp.int32, (tb, LANES), 1)
    vrow = lax.broadcasted_iota(jnp.int32, (V_ROWS - DH_FOX, tb), 0)
    ones_row = jnp.where(vrow == 0, 1.0, 0.0).astype(BF16)
    fv_tb = fv_t.astype(BF16)
    for hh in range(H_FOX):
        vt_scr[hh, :, pl.ds(t0, tb)] = jnp.concatenate([fv_tb[hh * DH_FOX:(hh + 1) * DH_FOX, :], ones_row], axis=0)
    qrow = lax.broadcasted_iota(jnp.int32, (LANES - DH_FOX, tb), 0)
    k_blk = []
    qt_blk = []
    for hh in range(H_FOX):
        pair = fk[:, (hh // 2) * LANES:(hh // 2 + 1) * LANES]
        if hh % 2:
            pair = pltpu.roll(pair, DH_FOX, 1)
        k_aug = jnp.where(lane < DH_FOX, pair, bias).astype(BF16)
        k_scr[hh, pl.ds(t0, tb), :] = k_aug
        k_blk.append(k_aug)
        ones = jnp.where(
            jnp.logical_or(qrow == hh, jnp.logical_or(qrow == H_FOX + hh, qrow == 2 * H_FOX + hh)), 1.0, 0.0)
        qt_aug = jnp.concatenate([fq_t[hh * DH_FOX:(hh + 1) * DH_FOX, :], ones], axis=0).astype(BF16)
        qt_scr[hh] = qt_aug
        qt_blk.append(qt_aug)

    s_all = [_mm(k_blk[hh], qt_blk[hh]) for hh in range(H_FOX)]
    zrg = proj(OFF_RG, W_RET_V)
    p_all = []
    for hh in range(H_FOX):
        s_t = jnp.where(key_le_query, s_all[hh], -jnp.inf)
        m = jnp.max(s_t, axis=0, keepdims=True)
        m_scr[hh] = m
        p_all.append(jnp.exp2((s_t - m).astype(BF16)))
    for hh in range(H_FOX):
        acc_scr[hh] = _mm(vt_scr[hh, :, pl.ds(t0, tb)], p_all[hh])

    def scores(buf, kb, hh):
        off = pl.multiple_of(kb * tb, tb)
        buf[hh] = _mm(k_scr[hh, pl.ds(off, tb), :], qt_scr[hh])

    def softmax(buf, hh):
        s_t = buf[hh]
        m_old = m_scr[hh]
        m_new = jnp.maximum(m_old, jnp.max(s_t, axis=0, keepdims=True))
        m_scr[hh] = m_new
        return hh, jnp.exp2((s_t - m_new).astype(BF16)), jnp.exp2(m_old - m_new)

    def accumulate(kb, pending):
        hh, p_t, alpha = pending
        off = pl.multiple_of(kb * tb, tb)
        acc_scr[hh] = alpha * acc_scr[hh] + _mm(vt_scr[hh, :, pl.ds(off, tb)], p_t)

    def absorb(buf, kb, also=None):
        pending = None
        for hh in range(H_FOX):
            cur = softmax(buf, hh)
            if also is not None:
                also(hh)
            if pending is not None:
                accumulate(kb, pending)
            pending = cur
        accumulate(kb, pending)

    def run_blocks(kb, count):
        bufs = (sa_scr, sb_scr)
        for hh in range(H_FOX):
            scores(bufs[0], kb, hh)
        for c in range(count):
            nxt = None
            if c + 1 < count:
                nxt = functools.partial(scores, bufs[(c + 1) % 2], kb + c + 1)
            absorb(bufs[c % 2], kb + c, also=nxt)

    def kv_quad(i, carry):
        run_blocks(4 * i, 4)
        return carry

    n_quads = lax.shift_right_logical(j, 2)
    lax.fori_loop(0, n_quads, kv_quad, 0)
    for rest in range(1, 4):
        pl.when(jnp.bitwise_and(j, 3) == rest)(functools.partial(run_blocks, 4 * n_quads, rest))

    zfg = proj(OFF_FG, W_FOX)
    gate0 = gate(0)
    fo = jnp.concatenate(
        [acc_scr[hh, :DH_FOX, :] / acc_scr[hh, DH_FOX:DH_FOX + 1, :] for hh in range(H_FOX)], axis=0).T
    a_in = (fo * _silu(zfg)).astype(BF16)

    cos = cos_ref[...]
    sin = sin_ref[...]
    rq, rkt, rvs, states = [], [], [], []
    for hh in range(H_RET):
        cols = slice(hh * DK_RET, (hh + 1) * DK_RET)
        rq.append(_rope(zrq[:, cols], cos, sin).astype(BF16))
        rkt.append((_rope(zrk[:, cols], cos, sin) * (DK_RET ** -0.5)).T)
        rvs.append(rv[:, hh * DV_RET:(hh + 1) * DV_RET])
        states.append(st_ref[0, hh])
    inter = [_mm(rq[hh], states[hh].astype(BF16)) for hh in range(H_RET)]
    scores = [_mm(rq[hh], rkt[hh].astype(BF16)) for hh in range(H_RET)]
    zmq = proj(OFF_MQ, W_MEM)
    merged = gate0 * _mm(a_in, wpf_ref[...])
    ret_o = [inter[hh] * decq_scr[hh] + _mm((scores[hh] * dmask_scr[hh]).astype(BF16), rvs[hh])
             for hh in range(H_RET)]
    for hh in range(H_RET):
        st_ref[0, hh] = ssc_scr[hh] * states[hh] + _mm((rkt[hh] * deck_scr[hh]).astype(BF16), rvs[hh])
    gate1 = gate(1)
    zmg = proj(OFF_MG, W_MEM)

    mem_s = []
    for hh in range(H_MEM):
        cols = slice(hh * DH_MEM, (hh + 1) * DH_MEM)
        q = _rms_rows(zmq[:, cols], gmq_ref[...]).astype(BF16)
        mem_s.append(_mm(q, mkt_ref[0, cols, :]) * (DH_MEM ** -0.5))
    r_in = jnp.concatenate(
        [_group_norm(ret_o[hh]) * _silu(zrg[:, hh * DV_RET:(hh + 1) * DV_RET]) for hh in range(H_RET)],
        axis=1).astype(BF16)
    merged = merged + gate1 * _mm(r_in, wpr_ref[...])
    m_parts = []
    for hh in range(H_MEM):
        cols = slice(hh * DH_MEM, (hh + 1) * DH_MEM)
        p = jnp.exp(mem_s[hh] - jnp.max(mem_s[hh], axis=-1, keepdims=True))
        l = jnp.sum(p, axis=-1, keepdims=True)
        m_parts.append(_mm(p.astype(BF16), mvb_ref[0, :, cols]) / l)
    gate2 = gate(2)
    m_in = (jnp.concatenate(m_parts, axis=1) * _silu(zmg)).astype(BF16)
    merged = merged + gate2 * _mm(m_in, wpm_ref[...])

    y_ref[0] = x_ref[0] + _mm(merged.astype(BF16), wo_ref[...])


def _resident(shape):
    nd = len(shape)
    return pl.BlockSpec(shape, lambda b, j: (0,) * nd, pipeline_mode=pl.Buffered(1))


def _prompt_call(x, cos, sin, mkt, mvb, wqkvf_t, bf_col, gfq_col, gfk_col, log_g, g_norm, b_merge, gmq, w_main, wpf,
                 wpr, wpm, wo, tb=PROMPT_BLOCK):
    B, T, _ = x.shape
    nt = T // tb
    kernel = functools.partial(_prompt_kernel, tb=tb)
    in_specs = [
        pl.BlockSpec(memory_space=pltpu.SMEM),
        pl.BlockSpec((1, tb, D_MODEL), lambda b, j: (b, j, 0)),
        pl.BlockSpec((tb, DK_RET), lambda b, j: (j, 0)),
        pl.BlockSpec((tb, DK_RET), lambda b, j: (j, 0)),
        pl.BlockSpec((1, W_MEM, N_MEM), lambda b, j: (b, 0, 0)),
        pl.BlockSpec((1, N_MEM, W_MEM), lambda b, j: (b, 0, 0)),
        _resident((1, D_MODEL)),
        _resident((F_ROWS, 1)),
        _resident((1, N_BRANCH * D_MODEL)),
        _resident((W_FOX, 1)),
        _resident((W_FOX, 1)),
        _resident((1, DH_MEM)),
        _resident((D_MODEL, W_MAIN)),
        _resident((W_FOX, D_MODEL)),
        _resident((W_RET_V, D_MODEL)),
        _resident((W_MEM, D_MODEL)),
        _resident((D_MODEL, D_MODEL)),
        _resident((3 * W_FOX + F_ROWS, D_MODEL)),
    ]
    out_specs = [
        pl.BlockSpec((1, tb, D_MODEL), lambda b, j: (b, j, 0)),
        pl.BlockSpec((1, W_FOX, tb), lambda b, j: (b, 0, j)),
        pl.BlockSpec((1, W_FOX, tb), lambda b, j: (b, 0, j)),
        pl.BlockSpec((1, H_FOX, tb), lambda b, j: (b, 0, j)),
        pl.BlockSpec((1, H_RET, DK_RET, DV_RET), lambda b, j: (b, 0, 0, 0)),
    ]
    out_shape = [
        jax.ShapeDtypeStruct((B, T, D_MODEL), F32),
        jax.ShapeDtypeStruct((B, W_FOX, T), F32),
        jax.ShapeDtypeStruct((B, W_FOX, T), F32),
        jax.ShapeDtypeStruct((B, H_FOX, T), F32),
        jax.ShapeDtypeStruct((B, H_RET, DK_RET, DV_RET), F32),
    ]
    scratch = [
        pltpu.VMEM((H_FOX, T, LANES), BF16),
        pltpu.VMEM((H_FOX, V_ROWS, T), BF16),
        pltpu.VMEM((H_FOX, LANES, tb), BF16),
        pltpu.VMEM((H_FOX, 1, tb), F32),
        pltpu.VMEM((H_FOX, V_ROWS, tb), F32),
        pltpu.VMEM((H_FOX, tb, tb), F32),
        pltpu.VMEM((H_FOX, tb, tb), F32),
        pltpu.VMEM((F_ROWS, 1), F32),
        pltpu.VMEM((H_RET, tb, tb), F32),
        pltpu.VMEM((H_RET, tb, 1), F32),
        pltpu.VMEM((H_RET, 1, tb), F32),
        pltpu.VMEM((H_RET, 1, DV_RET), F32),
    ]
    return pl.pallas_call(
        kernel,
        grid=(B, nt),
        in_specs=in_specs,
        out_specs=out_specs,
        out_shape=out_shape,
        scratch_shapes=scratch,
        compiler_params=pltpu.CompilerParams(
            dimension_semantics=("arbitrary", "arbitrary"), vmem_limit_bytes=VMEM_LIMIT_BYTES),
        name="prompt",
    )(log_g, x, cos, sin, mkt, mvb, g_norm, bf_col, b_merge, gfq_col, gfk_col, gmq, w_main, wpf, wpr, wpm, wo, wqkvf_t)


def _sample_kernel(lg_ref, x_ref, cos_ref, sin_ref, ckt_ref, cvt_ref, clf_ref, st0_ref, cmk_ref, cmv_ref,
                   gn_ref, bf_ref, bm_ref, gfq_ref, gfk_ref, gmq_ref, seg_ref, w_ref, wpf_ref, wpr_ref, wpm_ref,
                   wo_ref, wft_ref,
                   y_ref, fk_ref, fv_ref, lft_ref, st_ref,
                   h_scr, fq_scr, knt_scr, vn_scr, ncn_scr, rq_scr, rkt_scr, rv_scr, mq_scr, fo_scr, ro_scr, mo_scr,
                   *, nb, ts, past):
    b = pl.program_id(0)
    n = nb * ts
    r0 = pl.multiple_of(b * ts, ts)

    @pl.when(b == 0)
    def _project_all():
        h = _rms_rows(x_ref[...], gn_ref[...]).astype(BF16)
        h_scr[...] = h
        fq_scr[...] = _head_norm64(_mm(h, w_ref[:, OFF_FQ:OFF_FQ + W_FOX]), gfq_ref[...], seg_ref) * (DH_FOX ** -0.5)
        fk = _head_norm64(_mm(h, w_ref[:, OFF_FK:OFF_FK + W_FOX]), gfk_ref[...], seg_ref)
        fv = _mm(h, w_ref[:, OFF_FV:OFF_FV + W_FOX])
        logf_t = _log_sigmoid(_mm_nt(wft_ref[...], h) + bf_ref[...])
        fk_ref[...] = fk
        fv_ref[...] = fv
        lft_ref[...] = logf_t[:H_FOX]
        knt_scr[...] = fk.T.astype(BF16)
        vn_scr[...] = fv.astype(BF16)
        ri = lax.broadcasted_iota(jnp.int32, (n, n), 0)
        ci = lax.broadcasted_iota(jnp.int32, (n, n), 1)
        same = _idiv(ri, ts) == _idiv(ci, ts)
        triu = jnp.where(jnp.logical_and(same, ri <= ci), 1.0, 0.0).astype(BF16)
        ncn_scr[...] = -_cumsum_lanes(triu, logf_t)
        cos = cos_ref[...]
        sin = sin_ref[...]
        zrq = _mm(h, w_ref[:, OFF_RQ:OFF_RQ + W_RET_QK])
        zrk = _mm(h, w_ref[:, OFF_RK:OFF_RK + W_RET_QK])
        for hh in range(H_RET):
            cols = slice(hh * DK_RET, (hh + 1) * DK_RET)
            rq_scr[:, cols] = _rope(zrq[:, cols], cos, sin)
            rkt_scr[cols, :] = (_rope(zrk[:, cols], cos, sin) * (DK_RET ** -0.5)).T
        rv_scr[...] = _mm(h, w_ref[:, OFF_RV:OFF_RV + W_RET_V]).astype(BF16)
        zmq = _mm(h, w_ref[:, OFF_MQ:OFF_MQ + W_MEM])
        for hh in range(H_MEM):
            cols = slice(hh * DH_MEM, (hh + 1) * DH_MEM)
            mq_scr[:, cols] = _rms_rows(zmq[:, cols], gmq_ref[...])

    nr = H_FOX * ts
    fq_b = fq_scr[pl.ds(r0, ts), :]
    row_head = _idiv(lax.broadcasted_iota(jnp.int32, (nr, W_FOX), 0), ts)
    col_head = _idiv(lax.broadcasted_iota(jnp.int32, (nr, W_FOX), 1), DH_FOX)
    head_sel = row_head == col_head
    q_bd = jnp.where(head_sel, jnp.concatenate([fq_b] * H_FOX, axis=0), 0.0).astype(BF16)

    cc = 256
    ri = lax.broadcasted_iota(jnp.int32, (cc, cc), 0)
    ci = lax.broadcasted_iota(jnp.int32, (cc, cc), 1)
    triu = jnp.where(ri <= ci, 1.0, 0.0).astype(BF16)
    carry = jnp.zeros((H_FOX, 1), F32)
    c_chunks = []
    for c in range(past // cc):
        cch = _cumsum_lanes(triu, clf_ref[0, :, c * cc:(c + 1) * cc]) + carry
        carry = cch[:, cc - 1:cc]
        c_chunks.append(cch)
    c_past = jnp.concatenate(c_chunks, axis=1)
    ncp = jnp.concatenate([jnp.broadcast_to(-c_past[hh:hh + 1, :], (ts, past)) for hh in range(H_FOX)], axis=0)
    ncn = ncn_scr[:H_FOX, :] - carry
    ncn = jnp.concatenate([jnp.broadcast_to(ncn[hh:hh + 1, :], (ts, n)) for hh in range(H_FOX)], axis=0)

    s_p = _mm(q_bd, ckt_ref[0].astype(BF16)) + ncp
    s_n = _mm(q_bd, knt_scr[...]) + ncn
    tok = lax.broadcasted_iota(jnp.int32, (nr, n), 1)
    qt = _imod(lax.broadcasted_iota(jnp.int32, (nr, n), 0), ts)
    valid = jnp.logical_and(_idiv(tok, ts) == b, _imod(tok, ts) <= qt)
    s_n = jnp.where(valid, s_n, -jnp.inf)
    m = jnp.maximum(jnp.max(s_p, axis=-1, keepdims=True), jnp.max(s_n, axis=-1, keepdims=True))
    p_p = jnp.exp(s_p - m)
    p_n = jnp.exp(s_n - m)
    l = jnp.sum(p_p, axis=-1, keepdims=True) + jnp.sum(p_n, axis=-1, keepdims=True)
    o_past = _mm_nt(p_p.astype(BF16), cvt_ref[0].astype(BF16))
    o_bd = (o_past + _mm(p_n.astype(BF16), vn_scr[...])) / l
    o_bd = jnp.where(head_sel, o_bd, 0.0)
    fo = o_bd[0:ts, :]
    for hh in range(1, H_FOX):
        fo = fo + o_bd[hh * ts:(hh + 1) * ts, :]
    fo_scr[pl.ds(r0, ts), :] = fo

    tokr = lax.broadcasted_iota(jnp.int32, (ts, n), 1)
    qtr = lax.broadcasted_iota(jnp.int32, (ts, n), 0)
    in_b = _idiv(tokr, ts) == b
    dt = (qtr - _imod(tokr, ts)).astype(F32)
    tokc = lax.broadcasted_iota(jnp.int32, (1, n), 1)
    kpos = _imod(tokc, ts).astype(F32)
    rowi = lax.broadcasted_iota(jnp.int32, (ts, 1), 0).astype(F32)
    r_parts = []
    for hh in range(H_RET):
        lg = lg_ref[hh]
        q = rq_scr[pl.ds(r0, ts), hh * DK_RET:(hh + 1) * DK_RET].astype(BF16)
        kt = rkt_scr[hh * DK_RET:(hh + 1) * DK_RET, :]
        v = rv_scr[:, hh * DV_RET:(hh + 1) * DV_RET]
        state = st0_ref[0, hh]
        dmask = jnp.where(jnp.logical_and(in_b, dt >= 0.0), jnp.exp(lg * jnp.maximum(dt, 0.0)), 0.0)
        inter = _mm(q, state.astype(BF16)) * jnp.exp(lg * (rowi + 1.0))
        scores = _mm(q, kt.astype(BF16)) * dmask
        r_parts.append(inter + _mm(scores.astype(BF16), v))
        deck = jnp.where(_idiv(tokc, ts) == b, jnp.exp(lg * (ts - 1.0 - kpos)), 0.0)
        st_ref[0, hh] = jnp.exp(jnp.full((1, DV_RET), lg * ts, F32)) * state + _mm((kt * deck).astype(BF16), v)
    ro_scr[pl.ds(r0, ts), :] = jnp.concatenate(r_parts, axis=1)

    m_parts = []
    for hh in range(H_MEM):
        cols = slice(hh * DH_MEM, (hh + 1) * DH_MEM)
        q = mq_scr[pl.ds(r0, ts), cols].astype(BF16)
        rows = pl.ds(hh, N_MEM, stride=H_MEM)
        s = _mm_nt(q, cmk_ref[0, rows, :].astype(BF16)) * (DH_MEM ** -0.5)
        p = jnp.exp(s - jnp.max(s, axis=-1, keepdims=True))
        l = jnp.sum(p, axis=-1, keepdims=True)
        m_parts.append(_mm(p.astype(BF16), cmv_ref[0, rows, :].astype(BF16)) / l)
    mo_scr[pl.ds(r0, ts), :] = jnp.concatenate(m_parts, axis=1)

    @pl.when(b == nb - 1)
    def _output_all():
        h = h_scr[...]
        a_in = (fo_scr[...] * _silu(_mm(h, w_ref[:, OFF_FG:OFF_FG + W_FOX]))).astype(BF16)
        merged = _gate(h, w_ref, bm_ref, 0) * _mm(a_in, wpf_ref[...])
        zrg = _mm(h, w_ref[:, OFF_RG:OFF_RG + W_RET_V])
        ro = ro_scr[...]
        r_parts = []
        for hh in range(H_RET):
            cols = slice(hh * DV_RET, (hh + 1) * DV_RET)
            r_parts.append(_group_norm(ro[:, cols]) * _silu(zrg[:, cols]))
        r_in = jnp.concatenate(r_parts, axis=1).astype(BF16)
        merged = merged + _gate(h, w_ref, bm_ref, 1) * _mm(r_in, wpr_ref[...])
        m_in = (mo_scr[...] * _silu(_mm(h, w_ref[:, OFF_MG:OFF_MG + W_MEM]))).astype(BF16)
        merged = merged + _gate(h, w_ref, bm_ref, 2) * _mm(m_in, wpm_ref[...])
        y_ref[...] = x_ref[...] + _mm(merged.astype(BF16), wo_ref[...])


def _sample_call(x2d, cos, sin, ckt, cvt, clf_t, st0, cmk, cmv, wf_t, bf_col, log_g, g_norm, b_merge, gfq, gfk, gmq, seg,
                 w_main, wpf, wpr, wpm, wo, nb, ts):
    n = nb * ts
    past = ckt.shape[2]
    kernel = functools.partial(_sample_kernel, nb=nb, ts=ts, past=past)

    def res(shape):
        nd = len(shape)
        return pl.BlockSpec(shape, lambda b: (0,) * nd, pipeline_mode=pl.Buffered(1))

    in_specs = [
        pl.BlockSpec(memory_space=pltpu.SMEM),
        res((n, D_MODEL)),
        res((n, DK_RET)),
        res((n, DK_RET)),
        pl.BlockSpec((1, W_FOX, past), lambda b: (b, 0, 0)),
        pl.BlockSpec((1, W_FOX, past), lambda b: (b, 0, 0)),
        pl.BlockSpec((1, H_FOX, past), lambda b: (b, 0, 0)),
        pl.BlockSpec((1, H_RET, DK_RET, DV_RET), lambda b: (b, 0, 0, 0)),
        pl.BlockSpec((1, N_MEM * H_MEM, DH_MEM), lambda b: (b, 0, 0)),
        pl.BlockSpec((1, N_MEM * H_MEM, DH_MEM), lambda b: (b, 0, 0)),
        res((1, D_MODEL)),
        res((F_ROWS, 1)),
        res((1, N_BRANCH * D_MODEL)),
        res((1, W_FOX)),
        res((1, W_FOX)),
        res((1, DH_MEM)),
        res((MXU_TILE, MXU_TILE)),
        res((D_MODEL, W_MAIN)),
        res((W_FOX, D_MODEL)),
        res((W_RET_V, D_MODEL)),
        res((W_MEM, D_MODEL)),
        res((D_MODEL, D_MODEL)),
        res((F_ROWS, D_MODEL)),
    ]
    full = lambda shape: pl.BlockSpec(shape, lambda b: (0,) * len(shape))
    out_specs = [
        full((n, D_MODEL)),
        full((n, W_FOX)),
        full((n, W_FOX)),
        full((H_FOX, n)),
        pl.BlockSpec((1, H_RET, DK_RET, DV_RET), lambda b: (b, 0, 0, 0)),
    ]
    out_shape = [
        jax.ShapeDtypeStruct((n, D_MODEL), F32),
        jax.ShapeDtypeStruct((n, W_FOX), F32),
        jax.ShapeDtypeStruct((n, W_FOX), F32),
        jax.ShapeDtypeStruct((H_FOX, n), F32),
        jax.ShapeDtypeStruct((nb, H_RET, DK_RET, DV_RET), F32),
    ]
    scratch = [
        pltpu.VMEM((n, D_MODEL), BF16),
        pltpu.VMEM((n, W_FOX), F32),
        pltpu.VMEM((W_FOX, n), BF16),
        pltpu.VMEM((n, W_FOX), BF16),
        pltpu.VMEM((F_ROWS, n), F32),
        pltpu.VMEM((n, W_RET_QK), F32),
        pltpu.VMEM((W_RET_QK, n), F32),
        pltpu.VMEM((n, W_RET_V), BF16),
        pltpu.VMEM((n, W_MEM), F32),
        pltpu.VMEM((n, W_FOX), F32),
        pltpu.VMEM((n, W_RET_V), F32),
        pltpu.VMEM((n, W_MEM), F32),
    ]
    return pl.pallas_call(
        kernel,
        grid=(nb,),
        in_specs=in_specs,
        out_specs=out_specs,
        out_shape=out_shape,
        scratch_shapes=scratch,
        compiler_params=pltpu.CompilerParams(
            dimension_semantics=("arbitrary",), vmem_limit_bytes=VMEM_LIMIT_BYTES),
        name="sample",
    )(log_g, x2d, cos, sin, ckt, cvt, clf_t, st0, cmk, cmv, g_norm, bf_col, b_merge, gfq, gfk, gmq, seg,
      w_main, wpf, wpr, wpm, wo, wf_t)


def _rope_tables(pos):
    half = DK_RET // 2
    inv = ROPE_BASE ** (-jnp.arange(half, dtype=F32) / half)
    ang = pos.astype(F32)[:, None] * inv[None, :]
    cos = jnp.cos(ang)
    sin = jnp.sin(ang)
    return jnp.concatenate([cos, cos], axis=1), jnp.concatenate([-sin, sin], axis=1)


def _reorder_w_in(w_t):
    assert REORDER_ROWS == 3 * W_FOX and W_MAIN % REORDER_ROWS == 0 and REORDER_ROWS % H_FOX == 0
    assert w_t.shape == (W_MAIN + H_FOX, D_MODEL)
    return pl.pallas_call(
        _reorder_kernel,
        grid=(W_MAIN // REORDER_ROWS,),
        in_specs=[
            pl.BlockSpec((REORDER_ROWS, D_MODEL), lambda c: (c, 0)),
            pl.BlockSpec((H_FOX, D_MODEL), lambda c: ((c + 1) * (REORDER_ROWS // H_FOX), 0)),
        ],
        out_specs=[
            pl.BlockSpec((D_MODEL, REORDER_ROWS), lambda c: (0, c)),
            pl.BlockSpec((3 * W_FOX + F_ROWS, D_MODEL), lambda c: (0, 0)),
        ],
        out_shape=[
            jax.ShapeDtypeStruct((D_MODEL, W_MAIN), BF16),
            jax.ShapeDtypeStruct((3 * W_FOX + F_ROWS, D_MODEL), BF16),
        ],
        compiler_params=pltpu.CompilerParams(
            dimension_semantics=("arbitrary",), vmem_limit_bytes=VMEM_LIMIT_BYTES),
        name="reorder_w_in",
    )(w_t, w_t)


def _reorder_kernel(a_ref, b_ref, o_ref, qkvf_ref):
    c = pl.program_id(0)
    a = a_ref[...]
    shifted = jnp.concatenate([a[H_FOX:], b_ref[...]], axis=0)
    src = jnp.where(c < (3 * W_FOX) // REORDER_ROWS, a, shifted)
    o_ref[...] = src.T.astype(BF16)

    @pl.when(c == 0)
    def _fox_rows():
        qkvf_ref[:3 * W_FOX, :] = a.astype(BF16)
        gates = jnp.concatenate([b_ref[...], jnp.zeros((F_ROWS - H_FOX, D_MODEL), F32)], axis=0)
        qkvf_ref[3 * W_FOX:, :] = gates.astype(BF16)


def kernel(x_prompt, x_sample, mem_prompt, cache_fox_k, cache_fox_v, cache_fox_logf, state_ret, cache_mem_k, cache_mem_v,
           g_norm, g_mem_norm, w_in, b_f, b_merge, g_fox_q, g_fox_k, g_mem_q, g_mem_k, w_mem_kv,
           w_p_fox, w_p_ret, w_p_mem, w_out):
    depth = w_in.shape[0]
    assert depth == 1, "single-layer kernel"
    B, T, _ = x_prompt.shape
    nb, ts, _ = x_sample.shape
    past = cache_fox_k.shape[2]

    log_g = jnp.log1p(-jnp.exp2(-5.0 - jnp.arange(H_RET, dtype=F32)))
    head_of_lane = jnp.arange(MXU_TILE) // DH_FOX
    seg = (head_of_lane[:, None] == head_of_lane[None, :]).astype(BF16)
    w_t = jnp.swapaxes(w_in[0], 0, 1)
    w_main, wqkvf_t = _reorder_w_in(w_t)
    gn = g_norm[0].reshape(1, D_MODEL)
    bm = b_merge[0].reshape(1, N_BRANCH * D_MODEL)
    gfq = jnp.tile(g_fox_q[0], H_FOX).reshape(1, W_FOX)
    gfk = jnp.tile(g_fox_k[0], H_FOX).reshape(1, W_FOX)
    gmq = g_mem_q[0].reshape(1, DH_MEM)
    wpf = w_p_fox[0].astype(BF16)
    wpr = w_p_ret[0].astype(BF16)
    wpm = w_p_mem[0].astype(BF16)
    wo = w_out[0].astype(BF16)
    shared = (gn, bm, gfq, gfk, gmq, seg, w_main, wpf, wpr, wpm, wo)

    mk, mv, mkt, mvb = _memkv_call(mem_prompt, g_mem_norm[0], w_mem_kv[0], g_mem_k[0])

    cos_p, sin_p = _rope_tables(jnp.arange(T, dtype=jnp.int32))
    bf_col = jnp.concatenate([b_f[0], jnp.zeros((F_ROWS - H_FOX,), F32)]).reshape(F_ROWS, 1)
    y_p, fkt_p, fvt_p, lft_p, st_p = _prompt_call(
        x_prompt, cos_p, sin_p, mkt, mvb, wqkvf_t, bf_col, gfq.reshape(W_FOX, 1), gfk.reshape(W_FOX, 1), log_g,
        gn, bm, gmq, w_main, wpf, wpr, wpm, wo)

    def time_major(a_t):
        return jnp.transpose(a_t.reshape(a_t.shape[0], H_FOX, DH_FOX, a_t.shape[2]), (0, 3, 1, 2))[None]

    def time_minor(a):
        return jnp.transpose(a, (0, 2, 3, 1)).reshape(a.shape[0], W_FOX, a.shape[1])

    fk_p = time_major(fkt_p)
    fv_p = time_major(fvt_p)
    lf_p = jnp.swapaxes(lft_p, 1, 2)[None]

    pos_s = past + jnp.arange(ts, dtype=jnp.int32)
    cos_s, sin_s = _rope_tables(jnp.tile(pos_s, nb))
    y_s, fk_s, fv_s, lft_s, st_s = _sample_call(
        x_sample.reshape(nb * ts, D_MODEL), cos_s, sin_s,
        time_minor(cache_fox_k[0]), time_minor(cache_fox_v[0]),
        jnp.swapaxes(cache_fox_logf[0], 1, 2), state_ret[0],
        cache_mem_k[0].reshape(nb, N_MEM * H_MEM, DH_MEM), cache_mem_v[0].reshape(nb, N_MEM * H_MEM, DH_MEM),
        wqkvf_t[3 * W_FOX:], bf_col, log_g, *shared, nb=nb, ts=ts)
    lf_s = jnp.transpose(lft_s.reshape(H_FOX, nb, ts), (1, 2, 0))[None]

    return (y_p, y_s.reshape(nb, ts, D_MODEL),
            fk_p, fv_p, lf_p,
            st_p[None], mk.reshape(1, B, N_MEM, H_MEM, DH_MEM), mv.reshape(1, B, N_MEM, H_MEM, DH_MEM),
            fk_s.reshape(1, nb, ts, H_FOX, DH_FOX), fv_s.reshape(1, nb, ts, H_FOX, DH_FOX),
            lf_s, st_s[None])
```

```python
import functools

import jax
import jax.numpy as jnp
from jax import lax
from jax.experimental import pallas as pl
from jax.experimental.pallas import tpu as pltpu

D_MODEL = 1024
N_MEM = 256
H_FOX = 8
DH_FOX = 64
H_RET = 4
DK_RET = 128
DV_RET = 256
H_MEM = 4
DH_MEM = 128
W_FOX = H_FOX * DH_FOX
W_RET_QK = H_RET * DK_RET
W_RET_V = H_RET * DV_RET
W_MEM = H_MEM * DH_MEM
N_BRANCH = 3
ROPE_BASE = 10000.0
EPS = 1e-6
LOG2E = 1.4426950408889634

LANES = 128
MXU_TILE = 256
BF16_SUBLANES = 16
V_ROWS = DH_FOX + BF16_SUBLANES
F_ROWS = BF16_SUBLANES

OFF_FQ = 0
OFF_FK = OFF_FQ + W_FOX
OFF_FV = OFF_FK + W_FOX
OFF_FG = OFF_FV + W_FOX
OFF_RQ = OFF_FG + W_FOX
OFF_RK = OFF_RQ + W_RET_QK
OFF_RV = OFF_RK + W_RET_QK
OFF_RG = OFF_RV + W_RET_V
OFF_MQ = OFF_RG + W_RET_V
OFF_MG = OFF_MQ + W_MEM
OFF_GL = OFF_MG + W_MEM
W_MAIN = OFF_GL + N_BRANCH * D_MODEL
REORDER_ROWS = 1536

PROMPT_BLOCK = 256
MERGE_PARTS = 2
VMEM_LIMIT_BYTES = 60 * 1024 * 1024

F32 = jnp.float32
BF16 = jnp.bfloat16


def _mm(a, b):
    return jnp.dot(a, b, preferred_element_type=F32)


def _mm_nt(a, b):
    return lax.dot_general(a, b, (((1,), (1,)), ((), ())), preferred_element_type=F32)


def _idiv(x, d):
    assert d & (d - 1) == 0
    return lax.shift_right_logical(x, d.bit_length() - 1)


def _imod(x, d):
    assert d & (d - 1) == 0
    return jnp.bitwise_and(x, d - 1)


def _rms_rows(x, g):
    ms = jnp.mean(x * x, axis=-1, keepdims=True)
    return x * lax.rsqrt(ms + EPS) * g


def _head_sumsq(z, seg_ref):
    sq = (z * z).astype(BF16)
    return jnp.concatenate(
        [_mm(sq[:, c:c + MXU_TILE], seg_ref[...]) for c in range(0, W_FOX, MXU_TILE)], axis=1)


def _head_norm64(z, g, seg_ref):
    return z * lax.rsqrt(_head_sumsq(z, seg_ref) * (1.0 / DH_FOX) + EPS) * g


def _log_sigmoid(u):
    return jnp.minimum(u, 0.0) - jnp.log1p(jnp.exp(-jnp.abs(u)))


def _silu(u):
    return u * jax.nn.sigmoid(u)


def _split3(a):
    hi = a.astype(BF16)
    r1 = a - hi.astype(F32)
    mid = r1.astype(BF16)
    lo = (r1 - mid.astype(F32)).astype(BF16)
    return hi, mid, lo


def _cumsum_lanes(triu, a):
    hi, mid, lo = _split3(a)
    return _mm(hi, triu) + _mm(mid, triu) + _mm(lo, triu)


def _rope(x, cos, sin_signed):
    return x * cos + pltpu.roll(x, DK_RET // 2, 1) * sin_signed


def _group_norm(o):
    mu = jnp.mean(o, axis=-1, keepdims=True)
    d = o - mu
    var = jnp.mean(d * d, axis=-1, keepdims=True)
    return d * lax.rsqrt(var + EPS)


def _gate(h, w_ref, bm_ref, i):
    z = _mm(h, w_ref[:, OFF_GL + i * D_MODEL:OFF_GL + (i + 1) * D_MODEL])
    return jax.nn.sigmoid(z + bm_ref[:, i * D_MODEL:(i + 1) * D_MODEL])


def _memkv_kernel(mem_ref, gmn_ref, w_ref, gmk_ref, mk_ref, mv_ref, mkt_ref, mvb_ref):
    for i in range(mem_ref.shape[0]):
        h = _rms_rows(mem_ref[i], gmn_ref[...]).astype(BF16)
        z = _mm(h, w_ref[...])
        ks = []
        for hh in range(H_MEM):
            ks.append(_rms_rows(z[:, hh * DH_MEM:(hh + 1) * DH_MEM], gmk_ref[...]))
        mk = jnp.concatenate(ks, axis=1)
        mv = z[:, W_MEM:]
        for hh in range(H_MEM):
            rows = pl.ds(hh, N_MEM, stride=H_MEM)
            mk_ref[i, rows, :] = ks[hh]
            mv_ref[i, rows, :] = mv[:, hh * DH_MEM:(hh + 1) * DH_MEM]
        mkt_ref[i] = mk.T.astype(BF16)
        mvb_ref[i] = mv.astype(BF16)


def _memkv_call(mem, g_mem_norm, w_mem_kv, g_mem_k):
    B = mem.shape[0]
    nbk = 2 if B % 2 == 0 else 1
    const = lambda b: (0, 0)
    per_b = lambda b: (b, 0, 0)
    return pl.pallas_call(
        _memkv_kernel,
        grid=(B // nbk,),
        in_specs=[
            pl.BlockSpec((nbk, N_MEM, D_MODEL), per_b),
            pl.BlockSpec((1, D_MODEL), const),
            pl.BlockSpec((D_MODEL, 2 * W_MEM), const),
            pl.BlockSpec((1, DH_MEM), const),
        ],
        out_specs=[
            pl.BlockSpec((nbk, N_MEM * H_MEM, DH_MEM), per_b),
            pl.BlockSpec((nbk, N_MEM * H_MEM, DH_MEM), per_b),
            pl.BlockSpec((nbk, W_MEM, N_MEM), per_b),
            pl.BlockSpec((nbk, N_MEM, W_MEM), per_b),
        ],
        out_shape=[
            jax.ShapeDtypeStruct((B, N_MEM * H_MEM, DH_MEM), F32),
            jax.ShapeDtypeStruct((B, N_MEM * H_MEM, DH_MEM), F32),
            jax.ShapeDtypeStruct((B, W_MEM, N_MEM), BF16),
            jax.ShapeDtypeStruct((B, N_MEM, W_MEM), BF16),
        ],
        compiler_params=pltpu.CompilerParams(dimension_semantics=("arbitrary",)),
        name="memkv",
    )(mem, g_mem_norm.reshape(1, D_MODEL), w_mem_kv.astype(BF16), g_mem_k.reshape(1, DH_MEM))


def _prompt_kernel(lg_ref, x_ref, cos_ref, sin_ref, mkt_ref, mvb_ref, gn_ref, bf_ref, bm_ref, gfq_ref, gfk_ref,
                   gmq_ref, w_ref, wpf_ref, wpr_ref, wpm_ref, wo_ref, wqkvf_ref,
                   y_ref, fkt_ref, fvt_ref, lft_ref, st_ref,
                   k_scr, vt_scr, qt_scr, m_scr, acc_scr, sa_scr, sb_scr, ccar_scr, dmask_scr, decq_scr, deck_scr,
                   ssc_scr, *, tb):
    b = pl.program_id(0)
    j = pl.program_id(1)
    t0 = pl.multiple_of(j * tb, tb)

    @pl.when(jnp.logical_and(b == 0, j == 0))
    def _init_tables():
        ri = lax.broadcasted_iota(jnp.int32, (tb, tb), 0)
        ci = lax.broadcasted_iota(jnp.int32, (tb, tb), 1)
        diff = (ri - ci).astype(F32)
        rowi = lax.broadcasted_iota(jnp.int32, (tb, 1), 0).astype(F32)
        coli = lax.broadcasted_iota(jnp.int32, (1, tb), 1).astype(F32)
        for hh in range(H_RET):
            lg = lg_ref[hh]
            dmask_scr[hh] = jnp.where(diff >= 0.0, jnp.exp(lg * jnp.maximum(diff, 0.0)), 0.0)
            decq_scr[hh] = jnp.exp(lg * (rowi + 1.0))
            deck_scr[hh] = jnp.exp(lg * (tb - 1.0 - coli))
            ssc_scr[hh] = jnp.exp(jnp.full((1, DV_RET), lg * tb, F32))

    @pl.when(j == 0)
    def _init_batch():
        ccar_scr[...] = jnp.zeros_like(ccar_scr)
        st_ref[...] = jnp.zeros_like(st_ref)

    h = _rms_rows(x_ref[0], gn_ref[...]).astype(BF16)

    def proj(off, width):
        return _mm(h, w_ref[:, off:off + width])

    part = D_MODEL // MERGE_PARTS

    def gate(i):
        out = []
        for c in range(MERGE_PARTS):
            lo = i * D_MODEL + c * part
            out.append(jax.nn.sigmoid(proj(OFF_GL + lo, part) + bm_ref[:, lo:lo + part]))
        return out

    def gated(gates, branch_in, wp_ref):
        return [gates[c] * _mm(branch_in, wp_ref[:, c * part:(c + 1) * part]) for c in range(MERGE_PARTS)]


    half = (3 * W_FOX + F_ROWS) // (2 * BF16_SUBLANES) * BF16_SUBLANES
    z_t = jnp.concatenate([_mm_nt(wqkvf_ref[:half, :], h), _mm_nt(wqkvf_ref[half:, :], h)], axis=0)
    zrq = proj(OFF_RQ, W_RET_QK)
    zrk = proj(OFF_RK, W_RET_QK)

    def head_norm_t(zh_t, g_col):
        parts = []
        for hh in range(H_FOX):
            zz = zh_t[hh * DH_FOX:(hh + 1) * DH_FOX]
            ms = jnp.sum(zz * zz, axis=0, keepdims=True) * (1.0 / DH_FOX)
            parts.append(zz * lax.rsqrt(ms + EPS))
        return jnp.concatenate(parts, axis=0) * g_col

    fq_t = head_norm_t(z_t[:W_FOX], gfq_ref[...]) * (DH_FOX ** -0.5 * LOG2E)
    fk_t = head_norm_t(z_t[W_FOX:2 * W_FOX], gfk_ref[...])
    fv_t = z_t[2 * W_FOX:3 * W_FOX]
    logf_t = _log_sigmoid(z_t[3 * W_FOX:] + bf_ref[...])
    ri = lax.broadcasted_iota(jnp.int32, (tb, tb), 0)
    ci = lax.broadcasted_iota(jnp.int32, (tb, tb), 1)
    key_le_query = ri <= ci
    c_t = _cumsum_lanes(jnp.where(key_le_query, 1.0, 0.0).astype(BF16), logf_t) + ccar_scr[...]
    ccar_scr[...] = c_t[:, tb - 1:tb]
    rv = proj(OFF_RV, W_RET_V).astype(BF16)
    fk = fk_t.T
    fkt_ref[0] = fk_t
    fvt_ref[0] = fv_t
    lft_ref[0] = logf_t[:H_FOX]
    hi, mid, lo = _split3(c_t[:H_FOX] * (-LOG2E))
    parts_t = jnp.concatenate(
        [hi.astype(F32), mid.astype(F32), lo.astype(F32), jnp.zeros((LANES - 3 * H_FOX, tb), F32)], axis=0)
    bias = pltpu.roll(parts_t.T, DH_FOX, 1)
    lane = lax.broadcasted_iota(jnp.int32, (tb, LANES), 1)
    vrow = lax.broadcasted_iota(jnp.int32, (V_ROWS - DH_FOX, tb), 0)
    ones_row = jnp.where(vrow == 0, 1.0, 0.0).astype(BF16)
    fv_tb = fv_t.astype(BF16)
    for hh in range(H_FOX):
        vt_scr[hh, :, pl.ds(t0, tb)] = jnp.concatenate([fv_tb[hh * DH_FOX:(hh + 1) * DH_FOX, :], ones_row], axis=0)
    qrow = lax.broadcasted_iota(jnp.int32, (LANES - DH_FOX, tb), 0)
    k_blk = []
    qt_blk = []
    for hh in range(H_FOX):
        pair = fk[:, (hh // 2) * LANES:(hh // 2 + 1) * LANES]
        if hh % 2:
            pair = pltpu.roll(pair, DH_FOX, 1)
        k_aug = jnp.where(lane < DH_FOX, pair, bias).astype(BF16)
        k_scr[hh, pl.ds(t0, tb), :] = k_aug
        k_blk.append(k_aug)
        ones = jnp.where(
            jnp.logical_or(qrow == hh, jnp.logical_or(qrow == H_FOX + hh, qrow == 2 * H_FOX + hh)), 1.0, 0.0)
        qt_aug = jnp.concatenate([fq_t[hh * DH_FOX:(hh + 1) * DH_FOX, :], ones], axis=0).astype(BF16)
        qt_scr[hh] = qt_aug
        qt_blk.append(qt_aug)

    s_all = [_mm(k_blk[hh], qt_blk[hh]) for hh in range(H_FOX)]
    zrg = proj(OFF_RG, W_RET_V)
    p_all = []
    for hh in range(H_FOX):
        s_t = jnp.where(key_le_query, s_all[hh], -jnp.inf)
        m = jnp.max(s_t, axis=0, keepdims=True)
        m_scr[hh] = m
        p_all.append(jnp.exp2((s_t - m).astype(BF16)))
    for hh in range(H_FOX):
        acc_scr[hh] = _mm(vt_scr[hh, :, pl.ds(t0, tb)], p_all[hh])

    def scores(buf, kb, hh):
        off = pl.multiple_of(kb * tb, tb)
        buf[hh] = _mm(k_scr[hh, pl.ds(off, tb), :], qt_scr[hh])

    def softmax(buf, hh):
        s_t = buf[hh]
        m_old = m_scr[hh]
        m_new = jnp.maximum(m_old, jnp.max(s_t, axis=0, keepdims=True))
        m_scr[hh] = m_new
        return hh, jnp.exp2((s_t - m_new).astype(BF16)), jnp.exp2(m_old - m_new)

    def accumulate(kb, pending):
        hh, p_t, alpha = pending
        off = pl.multiple_of(kb * tb, tb)
        acc_scr[hh] = alpha * acc_scr[hh] + _mm(vt_scr[hh, :, pl.ds(off, tb)], p_t)

    def absorb(buf, kb, also=None):
        pending = None
        for hh in range(H_FOX):
            cur = softmax(buf, hh)
            if also is not None:
                also(hh)
            if pending is not None:
                accumulate(kb, pending)
            pending = cur
        accumulate(kb, pending)

    def run_blocks(kb, count):
        bufs = (sa_scr, sb_scr)
        for hh in range(H_FOX):
            scores(bufs[0], kb, hh)
        for c in range(count):
            nxt = None
            if c + 1 < count:
                nxt = functools.partial(scores, bufs[(c + 1) % 2], kb + c + 1)
            absorb(bufs[c % 2], kb + c, also=nxt)

    def kv_quad(i, carry):
        run_blocks(4 * i, 4)
        return carry

    n_quads = lax.shift_right_logical(j, 2)
    lax.fori_loop(0, n_quads, kv_quad, 0)
    for rest in range(1, 4):
        pl.when(jnp.bitwise_and(j, 3) == rest)(functools.partial(run_blocks, 4 * n_quads, rest))

    zfg = proj(OFF_FG, W_FOX)
    gate0 = gate(0)
    fo = jnp.concatenate(
        [acc_scr[hh, :DH_FOX, :] / acc_scr[hh, DH_FOX:DH_FOX + 1, :] for hh in range(H_FOX)], axis=0).T
    a_in = (fo * _silu(zfg)).astype(BF16)

    cos = cos_ref[...]
    sin = sin_ref[...]
    rq, rkt, rvs, states = [], [], [], []
    for hh in range(H_RET):
        cols = slice(hh * DK_RET, (hh + 1) * DK_RET)
        rq.append(_rope(zrq[:, cols], cos, sin).astype(BF16))
        rkt.append((_rope(zrk[:, cols], cos, sin) * (DK_RET ** -0.5)).T)
        rvs.append(rv[:, hh * DV_RET:(hh + 1) * DV_RET])
        states.append(st_ref[0, hh])
    inter = [_mm(rq[hh], states[hh].astype(BF16)) for hh in range(H_RET)]
    scores = [_mm(rq[hh], rkt[hh].astype(BF16)) for hh in range(H_RET)]
    zmq = proj(OFF_MQ, W_MEM)
    merged = gated(gate0, a_in, wpf_ref)
    ret_o = [inter[hh] * decq_scr[hh] + _mm((scores[hh] * dmask_scr[hh]).astype(BF16), rvs[hh])
             for hh in range(H_RET)]
    for hh in range(H_RET):
        st_ref[0, hh] = ssc_scr[hh] * states[hh] + _mm((rkt[hh] * deck_scr[hh]).astype(BF16), rvs[hh])
    gate1 = gate(1)
    zmg = proj(OFF_MG, W_MEM)

    mem_s = []
    for hh in range(H_MEM):
        cols = slice(hh * DH_MEM, (hh + 1) * DH_MEM)
        q = _rms_rows(zmq[:, cols], gmq_ref[...]).astype(BF16)
        mem_s.append(_mm(q, mkt_ref[0, cols, :]) * (DH_MEM ** -0.5))
    r_in = jnp.concatenate(
        [_group_norm(ret_o[hh]) * _silu(zrg[:, hh * DV_RET:(hh + 1) * DV_RET]) for hh in range(H_RET)],
        axis=1).astype(BF16)
    merged = [m + t for m, t in zip(merged, gated(gate1, r_in, wpr_ref))]
    m_parts = []
    for hh in range(H_MEM):
        cols = slice(hh * DH_MEM, (hh + 1) * DH_MEM)
        p = jnp.exp(mem_s[hh] - jnp.max(mem_s[hh], axis=-1, keepdims=True))
        l = jnp.sum(p, axis=-1, keepdims=True)
        m_parts.append(_mm(p.astype(BF16), mvb_ref[0, :, cols]) / l)
    gate2 = gate(2)
    m_in = (jnp.concatenate(m_parts, axis=1) * _silu(zmg)).astype(BF16)
    merged = [m + t for m, t in zip(merged, gated(gate2, m_in, wpm_ref))]

    y = x_ref[0]
    for c in range(MERGE_PARTS):
        y = y + _mm(merged[c].astype(BF16), wo_ref[c * part:(c + 1) * part, :])
    y_ref[0] = y


def _resident(shape):
    nd = len(shape)
    return pl.BlockSpec(shape, lambda b, j: (0,) * nd, pipeline_mode=pl.Buffered(1))


def _prompt_call(x, cos, sin, mkt, mvb, wqkvf_t, bf_col, gfq_col, gfk_col, log_g, g_norm, b_merge, gmq, w_main, wpf,
                 wpr, wpm, wo, tb=PROMPT_BLOCK):
    B, T, _ = x.shape
    nt = T // tb
    kernel = functools.partial(_prompt_kernel, tb=tb)
    in_specs = [
        pl.BlockSpec(memory_space=pltpu.SMEM),
        pl.BlockSpec((1, tb, D_MODEL), lambda b, j: (b, j, 0)),
        pl.BlockSpec((tb, DK_RET), lambda b, j: (j, 0)),
        pl.BlockSpec((tb, DK_RET), lambda b, j: (j, 0)),
        pl.BlockSpec((1, W_MEM, N_MEM), lambda b, j: (b, 0, 0)),
        pl.BlockSpec((1, N_MEM, W_MEM), lambda b, j: (b, 0, 0)),
        _resident((1, D_MODEL)),
        _resident((F_ROWS, 1)),
        _resident((1, N_BRANCH * D_MODEL)),
        _resident((W_FOX, 1)),
        _resident((W_FOX, 1)),
        _resident((1, DH_MEM)),
        _resident((D_MODEL, W_MAIN)),
        _resident((W_FOX, D_MODEL)),
        _resident((W_RET_V, D_MODEL)),
        _resident((W_MEM, D_MODEL)),
        _resident((D_MODEL, D_MODEL)),
        _resident((3 * W_FOX + F_ROWS, D_MODEL)),
    ]
    out_specs = [
        pl.BlockSpec((1, tb, D_MODEL), lambda b, j: (b, j, 0)),
        pl.BlockSpec((1, W_FOX, tb), lambda b, j: (b, 0, j)),
        pl.BlockSpec((1, W_FOX, tb), lambda b, j: (b, 0, j)),
        pl.BlockSpec((1, H_FOX, tb), lambda b, j: (b, 0, j)),
        pl.BlockSpec((1, H_RET, DK_RET, DV_RET), lambda b, j: (b, 0, 0, 0)),
    ]
    out_shape = [
        jax.ShapeDtypeStruct((B, T, D_MODEL), F32),
        jax.ShapeDtypeStruct((B, W_FOX, T), F32),
        jax.ShapeDtypeStruct((B, W_FOX, T), F32),
        jax.ShapeDtypeStruct((B, H_FOX, T), F32),
        jax.ShapeDtypeStruct((B, H_RET, DK_RET, DV_RET), F32),
    ]
    scratch = [
        pltpu.VMEM((H_FOX, T, LANES), BF16),
        pltpu.VMEM((H_FOX, V_ROWS, T), BF16),
        pltpu.VMEM((H_FOX, LANES, tb), BF16),
        pltpu.VMEM((H_FOX, 1, tb), F32),
        pltpu.VMEM((H_FOX, V_ROWS, tb), F32),
        pltpu.VMEM((H_FOX, tb, tb), F32),
        pltpu.VMEM((H_FOX, tb, tb), F32),
        pltpu.VMEM((F_ROWS, 1), F32),
        pltpu.VMEM((H_RET, tb, tb), F32),
        pltpu.VMEM((H_RET, tb, 1), F32),
        pltpu.VMEM((H_RET, 1, tb), F32),
        pltpu.VMEM((H_RET, 1, DV_RET), F32),
    ]
    return pl.pallas_call(
        kernel,
        grid=(B, nt),
        in_specs=in_specs,
        out_specs=out_specs,
        out_shape=out_shape,
        scratch_shapes=scratch,
        compiler_params=pltpu.CompilerParams(
            dimension_semantics=("arbitrary", "arbitrary"), vmem_limit_bytes=VMEM_LIMIT_BYTES),
        name="prompt",
    )(log_g, x, cos, sin, mkt, mvb, g_norm, bf_col, b_merge, gfq_col, gfk_col, gmq, w_main, wpf, wpr, wpm, wo, wqkvf_t)


def _sample_kernel(lg_ref, x_ref, cos_ref, sin_ref, ckt_ref, cvt_ref, clf_ref, st0_ref, cmk_ref, cmv_ref,
                   gn_ref, bf_ref, bm_ref, gfq_ref, gfk_ref, gmq_ref, seg_ref, w_ref, wpf_ref, wpr_ref, wpm_ref,
                   wo_ref, wft_ref,
                   y_ref, fk_ref, fv_ref, lft_ref, st_ref,
                   h_scr, fq_scr, knt_scr, vn_scr, ncn_scr, rq_scr, rkt_scr, rv_scr, mq_scr, fo_scr, ro_scr, mo_scr,
                   *, nb, ts, past):
    b = pl.program_id(0)
    n = nb * ts
    r0 = pl.multiple_of(b * ts, ts)

    @pl.when(b == 0)
    def _project_all():
        h = _rms_rows(x_ref[...], gn_ref[...]).astype(BF16)
        h_scr[...] = h
        fq_scr[...] = _head_norm64(_mm(h, w_ref[:, OFF_FQ:OFF_FQ + W_FOX]), gfq_ref[...], seg_ref) * (DH_FOX ** -0.5)
        fk = _head_norm64(_mm(h, w_ref[:, OFF_FK:OFF_FK + W_FOX]), gfk_ref[...], seg_ref)
        fv = _mm(h, w_ref[:, OFF_FV:OFF_FV + W_FOX])
        logf_t = _log_sigmoid(_mm_nt(wft_ref[...], h) + bf_ref[...])
        fk_ref[...] = fk
        fv_ref[...] = fv
        lft_ref[...] = logf_t[:H_FOX]
        knt_scr[...] = fk.T.astype(BF16)
        vn_scr[...] = fv.astype(BF16)
        ri = lax.broadcasted_iota(jnp.int32, (n, n), 0)
        ci = lax.broadcasted_iota(jnp.int32, (n, n), 1)
        same = _idiv(ri, ts) == _idiv(ci, ts)
        triu = jnp.where(jnp.logical_and(same, ri <= ci), 1.0, 0.0).astype(BF16)
        ncn_scr[...] = -_cumsum_lanes(triu, logf_t)
        cos = cos_ref[...]
        sin = sin_ref[...]
        zrq = _mm(h, w_ref[:, OFF_RQ:OFF_RQ + W_RET_QK])
        zrk = _mm(h, w_ref[:, OFF_RK:OFF_RK + W_RET_QK])
        for hh in range(H_RET):
            cols = slice(hh * DK_RET, (hh + 1) * DK_RET)
            rq_scr[:, cols] = _rope(zrq[:, cols], cos, sin)
            rkt_scr[cols, :] = (_rope(zrk[:, cols], cos, sin) * (DK_RET ** -0.5)).T
        rv_scr[...] = _mm(h, w_ref[:, OFF_RV:OFF_RV + W_RET_V]).astype(BF16)
        zmq = _mm(h, w_ref[:, OFF_MQ:OFF_MQ + W_MEM])
        for hh in range(H_MEM):
            cols = slice(hh * DH_MEM, (hh + 1) * DH_MEM)
            mq_scr[:, cols] = _rms_rows(zmq[:, cols], gmq_ref[...])

    nr = H_FOX * ts
    fq_b = fq_scr[pl.ds(r0, ts), :]
    row_head = _idiv(lax.broadcasted_iota(jnp.int32, (nr, W_FOX), 0), ts)
    col_head = _idiv(lax.broadcasted_iota(jnp.int32, (nr, W_FOX), 1), DH_FOX)
    head_sel = row_head == col_head
    q_bd = jnp.where(head_sel, jnp.concatenate([fq_b] * H_FOX, axis=0), 0.0).astype(BF16)

    cc = 256
    ri = lax.broadcasted_iota(jnp.int32, (cc, cc), 0)
    ci = lax.broadcasted_iota(jnp.int32, (cc, cc), 1)
    triu = jnp.where(ri <= ci, 1.0, 0.0).astype(BF16)
    carry = jnp.zeros((H_FOX, 1), F32)
    c_chunks = []
    for c in range(past // cc):
        cch = _cumsum_lanes(triu, clf_ref[0, :, c * cc:(c + 1) * cc]) + carry
        carry = cch[:, cc - 1:cc]
        c_chunks.append(cch)
    c_past = jnp.concatenate(c_chunks, axis=1)
    ncp = jnp.concatenate([jnp.broadcast_to(-c_past[hh:hh + 1, :], (ts, past)) for hh in range(H_FOX)], axis=0)
    ncn = ncn_scr[:H_FOX, :] - carry
    ncn = jnp.concatenate([jnp.broadcast_to(ncn[hh:hh + 1, :], (ts, n)) for hh in range(H_FOX)], axis=0)

    s_p = _mm(q_bd, ckt_ref[0].astype(BF16)) + ncp
    s_n = _mm(q_bd, knt_scr[...]) + ncn
    tok = lax.broadcasted_iota(jnp.int32, (nr, n), 1)
    qt = _imod(lax.broadcasted_iota(jnp.int32, (nr, n), 0), ts)
    valid = jnp.logical_and(_idiv(tok, ts) == b, _imod(tok, ts) <= qt)
    s_n = jnp.where(valid, s_n, -jnp.inf)
    m = jnp.maximum(jnp.max(s_p, axis=-1, keepdims=True), jnp.max(s_n, axis=-1, keepdims=True))
    p_p = jnp.exp(s_p - m)
    p_n = jnp.exp(s_n - m)
    l = jnp.sum(p_p, axis=-1, keepdims=True) + jnp.sum(p_n, axis=-1, keepdims=True)
    o_past = _mm_nt(p_p.astype(BF16), cvt_ref[0].astype(BF16))
    o_bd = (o_past + _mm(p_n.astype(BF16), vn_scr[...])) / l
    o_bd = jnp.where(head_sel, o_bd, 0.0)
    fo = o_bd[0:ts, :]
    for hh in range(1, H_FOX):
        fo = fo + o_bd[hh * ts:(hh + 1) * ts, :]
    fo_scr[pl.ds(r0, ts), :] = fo

    tokr = lax.broadcasted_iota(jnp.int32, (ts, n), 1)
    qtr = lax.broadcasted_iota(jnp.int32, (ts, n), 0)
    in_b = _idiv(tokr, ts) == b
    dt = (qtr - _imod(tokr, ts)).astype(F32)
    tokc = lax.broadcasted_iota(jnp.int32, (1, n), 1)
    kpos = _imod(tokc, ts).astype(F32)
    rowi = lax.broadcasted_iota(jnp.int32, (ts, 1), 0).astype(F32)
    r_parts = []
    for hh in range(H_RET):
        lg = lg_ref[hh]
        q = rq_scr[pl.ds(r0, ts), hh * DK_RET:(hh + 1) * DK_RET].astype(BF16)
        kt = rkt_scr[hh * DK_RET:(hh + 1) * DK_RET, :]
        v = rv_scr[:, hh * DV_RET:(hh + 1) * DV_RET]
        state = st0_ref[0, hh]
        dmask = jnp.where(jnp.logical_and(in_b, dt >= 0.0), jnp.exp(lg * jnp.maximum(dt, 0.0)), 0.0)
        inter = _mm(q, state.astype(BF16)) * jnp.exp(lg * (rowi + 1.0))
        scores = _mm(q, kt.astype(BF16)) * dmask
        r_parts.append(inter + _mm(scores.astype(BF16), v))
        deck = jnp.where(_idiv(tokc, ts) == b, jnp.exp(lg * (ts - 1.0 - kpos)), 0.0)
        st_ref[0, hh] = jnp.exp(jnp.full((1, DV_RET), lg * ts, F32)) * state + _mm((kt * deck).astype(BF16), v)
    ro_scr[pl.ds(r0, ts), :] = jnp.concatenate(r_parts, axis=1)

    m_parts = []
    for hh in range(H_MEM):
        cols = slice(hh * DH_MEM, (hh + 1) * DH_MEM)
        q = mq_scr[pl.ds(r0, ts), cols].astype(BF16)
        rows = pl.ds(hh, N_MEM, stride=H_MEM)
        s = _mm_nt(q, cmk_ref[0, rows, :].astype(BF16)) * (DH_MEM ** -0.5)
        p = jnp.exp(s - jnp.max(s, axis=-1, keepdims=True))
        l = jnp.sum(p, axis=-1, keepdims=True)
        m_parts.append(_mm(p.astype(BF16), cmv_ref[0, rows, :].astype(BF16)) / l)
    mo_scr[pl.ds(r0, ts), :] = jnp.concatenate(m_parts, axis=1)

    @pl.when(b == nb - 1)
    def _output_all():
        h = h_scr[...]
        a_in = (fo_scr[...] * _silu(_mm(h, w_ref[:, OFF_FG:OFF_FG + W_FOX]))).astype(BF16)
        merged = _gate(h, w_ref, bm_ref, 0) * _mm(a_in, wpf_ref[...])
        zrg = _mm(h, w_ref[:, OFF_RG:OFF_RG + W_RET_V])
        ro = ro_scr[...]
        r_parts = []
        for hh in range(H_RET):
            cols = slice(hh * DV_RET, (hh + 1) * DV_RET)
            r_parts.append(_group_norm(ro[:, cols]) * _silu(zrg[:, cols]))
        r_in = jnp.concatenate(r_parts, axis=1).astype(BF16)
        merged = merged + _gate(h, w_ref, bm_ref, 1) * _mm(r_in, wpr_ref[...])
        m_in = (mo_scr[...] * _silu(_mm(h, w_ref[:, OFF_MG:OFF_MG + W_MEM]))).astype(BF16)
        merged = merged + _gate(h, w_ref, bm_ref, 2) * _mm(m_in, wpm_ref[...])
        y_ref[...] = x_ref[...] + _mm(merged.astype(BF16), wo_ref[...])


def _sample_call(x2d, cos, sin, ckt, cvt, clf_t, st0, cmk, cmv, wf_t, bf_col, log_g, g_norm, b_merge, gfq, gfk, gmq, seg,
                 w_main, wpf, wpr, wpm, wo, nb, ts):
    n = nb * ts
    past = ckt.shape[2]
    kernel = functools.partial(_sample_kernel, nb=nb, ts=ts, past=past)

    def res(shape):
        nd = len(shape)
        return pl.BlockSpec(shape, lambda b: (0,) * nd, pipeline_mode=pl.Buffered(1))

    in_specs = [
        pl.BlockSpec(memory_space=pltpu.SMEM),
        res((n, D_MODEL)),
        res((n, DK_RET)),
        res((n, DK_RET)),
        pl.BlockSpec((1, W_FOX, past), lambda b: (b, 0, 0)),
        pl.BlockSpec((1, W_FOX, past), lambda b: (b, 0, 0)),
        pl.BlockSpec((1, H_FOX, past), lambda b: (b, 0, 0)),
        pl.BlockSpec((1, H_RET, DK_RET, DV_RET), lambda b: (b, 0, 0, 0)),
        pl.BlockSpec((1, N_MEM * H_MEM, DH_MEM), lambda b: (b, 0, 0)),
        pl.BlockSpec((1, N_MEM * H_MEM, DH_MEM), lambda b: (b, 0, 0)),
        res((1, D_MODEL)),
        res((F_ROWS, 1)),
        res((1, N_BRANCH * D_MODEL)),
        res((1, W_FOX)),
        res((1, W_FOX)),
        res((1, DH_MEM)),
        res((MXU_TILE, MXU_TILE)),
        res((D_MODEL, W_MAIN)),
        res((W_FOX, D_MODEL)),
        res((W_RET_V, D_MODEL)),
        res((W_MEM, D_MODEL)),
        res((D_MODEL, D_MODEL)),
        res((F_ROWS, D_MODEL)),
    ]
    full = lambda shape: pl.BlockSpec(shape, lambda b: (0,) * len(shape))
    out_specs = [
        full((n, D_MODEL)),
        full((n, W_FOX)),
        full((n, W_FOX)),
        full((H_FOX, n)),
        pl.BlockSpec((1, H_RET, DK_RET, DV_RET), lambda b: (b, 0, 0, 0)),
    ]
    out_shape = [
        jax.ShapeDtypeStruct((n, D_MODEL), F32),
        jax.ShapeDtypeStruct((n, W_FOX), F32),
        jax.ShapeDtypeStruct((n, W_FOX), F32),
        jax.ShapeDtypeStruct((H_FOX, n), F32),
        jax.ShapeDtypeStruct((nb, H_RET, DK_RET, DV_RET), F32),
    ]
    scratch = [
        pltpu.VMEM((n, D_MODEL), BF16),
        pltpu.VMEM((n, W_FOX), F32),
        pltpu.VMEM((W_FOX, n), BF16),
        pltpu.VMEM((n, W_FOX), BF16),
        pltpu.VMEM((F_ROWS, n), F32),
        pltpu.VMEM((n, W_RET_QK), F32),
        pltpu.VMEM((W_RET_QK, n), F32),
        pltpu.VMEM((n, W_RET_V), BF16),
        pltpu.VMEM((n, W_MEM), F32),
        pltpu.VMEM((n, W_FOX), F32),
        pltpu.VMEM((n, W_RET_V), F32),
        pltpu.VMEM((n, W_MEM), F32),
    ]
    return pl.pallas_call(
        kernel,
        grid=(nb,),
        in_specs=in_specs,
        out_specs=out_specs,
        out_shape=out_shape,
        scratch_shapes=scratch,
        compiler_params=pltpu.CompilerParams(
            dimension_semantics=("arbitrary",), vmem_limit_bytes=VMEM_LIMIT_BYTES),
        name="sample",
    )(log_g, x2d, cos, sin, ckt, cvt, clf_t, st0, cmk, cmv, g_norm, bf_col, b_merge, gfq, gfk, gmq, seg,
      w_main, wpf, wpr, wpm, wo, wf_t)


def _rope_tables(pos):
    half = DK_RET // 2
    inv = ROPE_BASE ** (-jnp.arange(half, dtype=F32) / half)
    ang = pos.astype(F32)[:, None] * inv[None, :]
    cos = jnp.cos(ang)
    sin = jnp.sin(ang)
    return jnp.concatenate([cos, cos], axis=1), jnp.concatenate([-sin, sin], axis=1)


def _reorder_w_in(w_t):
    assert (3 * W_FOX) % REORDER_ROWS == 0 and W_MAIN % REORDER_ROWS == 0 and REORDER_ROWS % H_FOX == 0
    assert w_t.shape == (W_MAIN + H_FOX, D_MODEL)
    return pl.pallas_call(
        _reorder_kernel,
        grid=(W_MAIN // REORDER_ROWS,),
        in_specs=[
            pl.BlockSpec((REORDER_ROWS, D_MODEL), lambda c: (c, 0)),
            pl.BlockSpec((H_FOX, D_MODEL), lambda c: ((c + 1) * (REORDER_ROWS // H_FOX), 0)),
        ],
        out_specs=pl.BlockSpec((D_MODEL, REORDER_ROWS), lambda c: (0, c)),
        out_shape=jax.ShapeDtypeStruct((D_MODEL, W_MAIN), BF16),
        compiler_params=pltpu.CompilerParams(
            dimension_semantics=("arbitrary",), vmem_limit_bytes=VMEM_LIMIT_BYTES),
        name="reorder_w_in",
    )(w_t, w_t)


def _reorder_kernel(a_ref, b_ref, o_ref):
    c = pl.program_id(0)
    a = a_ref[...]
    shifted = jnp.concatenate([a[H_FOX:], b_ref[...]], axis=0)
    src = jnp.where(c < (3 * W_FOX) // REORDER_ROWS, a, shifted)
    o_ref[...] = src.T.astype(BF16)


def kernel(x_prompt, x_sample, mem_prompt, cache_fox_k, cache_fox_v, cache_fox_logf, state_ret, cache_mem_k, cache_mem_v,
           g_norm, g_mem_norm, w_in, b_f, b_merge, g_fox_q, g_fox_k, g_mem_q, g_mem_k, w_mem_kv,
           w_p_fox, w_p_ret, w_p_mem, w_out):
    depth = w_in.shape[0]
    assert depth == 1, "single-layer kernel"
    B, T, _ = x_prompt.shape
    nb, ts, _ = x_sample.shape
    past = cache_fox_k.shape[2]

    log_g = jnp.log1p(-jnp.exp2(-5.0 - jnp.arange(H_RET, dtype=F32)))
    head_of_lane = jnp.arange(MXU_TILE) // DH_FOX
    seg = (head_of_lane[:, None] == head_of_lane[None, :]).astype(BF16)
    w_t = jnp.swapaxes(w_in[0], 0, 1)
    w_main = _reorder_w_in(w_t)
    gn = g_norm[0].reshape(1, D_MODEL)
    bm = b_merge[0].reshape(1, N_BRANCH * D_MODEL)
    gfq = jnp.tile(g_fox_q[0], H_FOX).reshape(1, W_FOX)
    gfk = jnp.tile(g_fox_k[0], H_FOX).reshape(1, W_FOX)
    gmq = g_mem_q[0].reshape(1, DH_MEM)
    wpf = w_p_fox[0].astype(BF16)
    wpr = w_p_ret[0].astype(BF16)
    wpm = w_p_mem[0].astype(BF16)
    wo = w_out[0].astype(BF16)
    shared = (gn, bm, gfq, gfk, gmq, seg, w_main, wpf, wpr, wpm, wo)

    mk, mv, mkt, mvb = _memkv_call(mem_prompt, g_mem_norm[0], w_mem_kv[0], g_mem_k[0])

    cos_p, sin_p = _rope_tables(jnp.arange(T, dtype=jnp.int32))
    wqkvf_t = jnp.concatenate(
        [w_t[:3 * W_FOX + H_FOX], jnp.zeros((F_ROWS - H_FOX, D_MODEL), F32)], axis=0).astype(BF16)
    bf_col = jnp.concatenate([b_f[0], jnp.zeros((F_ROWS - H_FOX,), F32)]).reshape(F_ROWS, 1)
    y_p, fkt_p, fvt_p, lft_p, st_p = _prompt_call(
        x_prompt, cos_p, sin_p, mkt, mvb, wqkvf_t, bf_col, gfq.reshape(W_FOX, 1), gfk.reshape(W_FOX, 1), log_g,
        gn, bm, gmq, w_main, wpf, wpr, wpm, wo)

    def time_major(a_t):
        return jnp.transpose(a_t.reshape(a_t.shape[0], H_FOX, DH_FOX, a_t.shape[2]), (0, 3, 1, 2))[None]

    def time_minor(a):
        return jnp.transpose(a, (0, 2, 3, 1)).reshape(a.shape[0], W_FOX, a.shape[1])

    fk_p = time_major(fkt_p)
    fv_p = time_major(fvt_p)
    lf_p = jnp.swapaxes(lft_p, 1, 2)[None]

    pos_s = past + jnp.arange(ts, dtype=jnp.int32)
    cos_s, sin_s = _rope_tables(jnp.tile(pos_s, nb))
    y_s, fk_s, fv_s, lft_s, st_s = _sample_call(
        x_sample.reshape(nb * ts, D_MODEL), cos_s, sin_s,
        time_minor(cache_fox_k[0]), time_minor(cache_fox_v[0]),
        jnp.swapaxes(cache_fox_logf[0], 1, 2), state_ret[0],
        cache_mem_k[0].reshape(nb, N_MEM * H_MEM, DH_MEM), cache_mem_v[0].reshape(nb, N_MEM * H_MEM, DH_MEM),
        wqkvf_t[3 * W_FOX:], bf_col, log_g, *shared, nb=nb, ts=ts)
    lf_s = jnp.transpose(lft_s.reshape(H_FOX, nb, ts), (1, 2, 0))[None]

    return (y_p, y_s.reshape(nb, ts, D_MODEL),
            fk_p, fv_p, lf_p,
            st_p[None], mk.reshape(1, B, N_MEM, H_MEM, DH_MEM), mv.reshape(1, B, N_MEM, H_MEM, DH_MEM),
            fk_s.reshape(1, nb, ts, H_FOX, DH_FOX), fv_s.reshape(1, nb, ts, H_FOX, DH_FOX),
            lf_s, st_s[None])
```

```python
import functools

import jax
import jax.numpy as jnp
from jax import lax
from jax.experimental import pallas as pl
from jax.experimental.pallas import tpu as pltpu

D_MODEL = 1024
N_MEM = 256
H_FOX = 8
DH_FOX = 64
H_RET = 4
DK_RET = 128
DV_RET = 256
H_MEM = 4
DH_MEM = 128
W_FOX = H_FOX * DH_FOX
W_RET_QK = H_RET * DK_RET
W_RET_V = H_RET * DV_RET
W_MEM = H_MEM * DH_MEM
N_BRANCH = 3
ROPE_BASE = 10000.0
EPS = 1e-6
LOG2E = 1.4426950408889634

LANES = 128
MXU_TILE = 256
BF16_SUBLANES = 16
V_ROWS = DH_FOX + BF16_SUBLANES
F_ROWS = BF16_SUBLANES

OFF_FQ = 0
OFF_FK = OFF_FQ + W_FOX
OFF_FV = OFF_FK + W_FOX
OFF_FG = OFF_FV + W_FOX
OFF_RQ = OFF_FG + W_FOX
OFF_RK = OFF_RQ + W_RET_QK
OFF_RV = OFF_RK + W_RET_QK
OFF_RG = OFF_RV + W_RET_V
OFF_MQ = OFF_RG + W_RET_V
OFF_MG = OFF_MQ + W_MEM
OFF_GL = OFF_MG + W_MEM
W_MAIN = OFF_GL + N_BRANCH * D_MODEL
REORDER_ROWS = 1536

PROMPT_BLOCK = 256
MERGE_PARTS = 2
VMEM_LIMIT_BYTES = 60 * 1024 * 1024

F32 = jnp.float32
BF16 = jnp.bfloat16


def _mm(a, b):
    return jnp.dot(a, b, preferred_element_type=F32)


def _mm_nt(a, b):
    return lax.dot_general(a, b, (((1,), (1,)), ((), ())), preferred_element_type=F32)


def _idiv(x, d):
    assert d & (d - 1) == 0
    return lax.shift_right_logical(x, d.bit_length() - 1)


def _imod(x, d):
    assert d & (d - 1) == 0
    return jnp.bitwise_and(x, d - 1)


def _rms_rows(x, g):
    ms = jnp.mean(x * x, axis=-1, keepdims=True)
    return x * lax.rsqrt(ms + EPS) * g


def _head_sumsq(z, seg_ref):
    sq = (z * z).astype(BF16)
    return jnp.concatenate(
        [_mm(sq[:, c:c + MXU_TILE], seg_ref[...]) for c in range(0, W_FOX, MXU_TILE)], axis=1)


def _head_norm64(z, g, seg_ref):
    return z * lax.rsqrt(_head_sumsq(z, seg_ref) * (1.0 / DH_FOX) + EPS) * g


def _log_sigmoid(u):
    return jnp.minimum(u, 0.0) - jnp.log1p(jnp.exp(-jnp.abs(u)))


def _silu(u):
    return u * jax.nn.sigmoid(u)


def _split3(a):
    hi = a.astype(BF16)
    r1 = a - hi.astype(F32)
    mid = r1.astype(BF16)
    lo = (r1 - mid.astype(F32)).astype(BF16)
    return hi, mid, lo


def _cumsum_lanes(triu, a):
    hi, mid, lo = _split3(a)
    return _mm(hi, triu) + _mm(mid, triu) + _mm(lo, triu)


def _rope(x, cos, sin_signed):
    return x * cos + pltpu.roll(x, DK_RET // 2, 1) * sin_signed


def _group_norm(o):
    mu = jnp.mean(o, axis=-1, keepdims=True)
    d = o - mu
    var = jnp.mean(d * d, axis=-1, keepdims=True)
    return d * lax.rsqrt(var + EPS)


def _gate(h, w_ref, bm_ref, i):
    z = _mm(h, w_ref[:, OFF_GL + i * D_MODEL:OFF_GL + (i + 1) * D_MODEL])
    return jax.nn.sigmoid(z + bm_ref[:, i * D_MODEL:(i + 1) * D_MODEL])


def _memkv_kernel(mem_ref, gmn_ref, w_ref, gmk_ref, mk_ref, mv_ref, mkt_ref, mvb_ref):
    for i in range(mem_ref.shape[0]):
        h = _rms_rows(mem_ref[i], gmn_ref[...]).astype(BF16)
        z = _mm(h, w_ref[...])
        ks = []
        for hh in range(H_MEM):
            ks.append(_rms_rows(z[:, hh * DH_MEM:(hh + 1) * DH_MEM], gmk_ref[...]))
        mk = jnp.concatenate(ks, axis=1)
        mv = z[:, W_MEM:]
        for hh in range(H_MEM):
            rows = pl.ds(hh, N_MEM, stride=H_MEM)
            mk_ref[i, rows, :] = ks[hh]
            mv_ref[i, rows, :] = mv[:, hh * DH_MEM:(hh + 1) * DH_MEM]
        mkt_ref[i] = mk.T.astype(BF16)
        mvb_ref[i] = mv.astype(BF16)


def _memkv_call(mem, g_mem_norm, w_mem_kv, g_mem_k):
    B = mem.shape[0]
    nbk = 2 if B % 2 == 0 else 1
    const = lambda b: (0, 0)
    per_b = lambda b: (b, 0, 0)
    return pl.pallas_call(
        _memkv_kernel,
        grid=(B // nbk,),
        in_specs=[
            pl.BlockSpec((nbk, N_MEM, D_MODEL), per_b),
            pl.BlockSpec((1, D_MODEL), const),
            pl.BlockSpec((D_MODEL, 2 * W_MEM), const),
            pl.BlockSpec((1, DH_MEM), const),
        ],
        out_specs=[
            pl.BlockSpec((nbk, N_MEM * H_MEM, DH_MEM), per_b),
            pl.BlockSpec((nbk, N_MEM * H_MEM, DH_MEM), per_b),
            pl.BlockSpec((nbk, W_MEM, N_MEM), per_b),
            pl.BlockSpec((nbk, N_MEM, W_MEM), per_b),
        ],
        out_shape=[
            jax.ShapeDtypeStruct((B, N_MEM * H_MEM, DH_MEM), F32),
            jax.ShapeDtypeStruct((B, N_MEM * H_MEM, DH_MEM), F32),
            jax.ShapeDtypeStruct((B, W_MEM, N_MEM), BF16),
            jax.ShapeDtypeStruct((B, N_MEM, W_MEM), BF16),
        ],
        compiler_params=pltpu.CompilerParams(dimension_semantics=("arbitrary",)),
        name="memkv",
    )(mem, g_mem_norm.reshape(1, D_MODEL), w_mem_kv.astype(BF16), g_mem_k.reshape(1, DH_MEM))


def _prompt_kernel(lg_ref, x_ref, cos_ref, sin_ref, mkt_ref, mvb_ref, gn_ref, bf_ref, bm_ref, gfq_ref, gfk_ref,
                   gmq_ref, w_ref, wpf_ref, wpr_ref, wpm_ref, wo_ref, wqkvf_ref,
                   y_ref, fkt_ref, fvt_ref, lft_ref, st_ref,
                   k_scr, vt_scr, qt_scr, m_scr, acc_scr, sa_scr, sb_scr, ccar_scr, dmask_scr, decq_scr, deck_scr,
                   ssc_scr, *, tb):
    b = pl.program_id(0)
    j = pl.program_id(1)
    t0 = pl.multiple_of(j * tb, tb)

    @pl.when(jnp.logical_and(b == 0, j == 0))
    def _init_tables():
        ri = lax.broadcasted_iota(jnp.int32, (tb, tb), 0)
        ci = lax.broadcasted_iota(jnp.int32, (tb, tb), 1)
        diff = (ri - ci).astype(F32)
        rowi = lax.broadcasted_iota(jnp.int32, (tb, 1), 0).astype(F32)
        coli = lax.broadcasted_iota(jnp.int32, (1, tb), 1).astype(F32)
        for hh in range(H_RET):
            lg = lg_ref[hh]
            dmask_scr[hh] = jnp.where(diff >= 0.0, jnp.exp(lg * jnp.maximum(diff, 0.0)), 0.0)
            decq_scr[hh] = jnp.exp(lg * (rowi + 1.0))
            deck_scr[hh] = jnp.exp(lg * (tb - 1.0 - coli))
            ssc_scr[hh] = jnp.exp(jnp.full((1, DV_RET), lg * tb, F32))

    @pl.when(j == 0)
    def _init_batch():
        ccar_scr[...] = jnp.zeros_like(ccar_scr)
        st_ref[...] = jnp.zeros_like(st_ref)

    h = _rms_rows(x_ref[0], gn_ref[...]).astype(BF16)

    def proj(off, width):
        return _mm(h, w_ref[:, off:off + width])

    part = D_MODEL // MERGE_PARTS

    def gate(i):
        out = []
        for c in range(MERGE_PARTS):
            lo = i * D_MODEL + c * part
            out.append(jax.nn.sigmoid(proj(OFF_GL + lo, part) + bm_ref[:, lo:lo + part]))
        return out

    def gated(gates, branch_in, wp_ref):
        return [gates[c] * _mm(branch_in, wp_ref[:, c * part:(c + 1) * part]) for c in range(MERGE_PARTS)]


    half = (3 * W_FOX + F_ROWS) // (2 * BF16_SUBLANES) * BF16_SUBLANES
    z_t = jnp.concatenate([_mm_nt(wqkvf_ref[:half, :], h), _mm_nt(wqkvf_ref[half:, :], h)], axis=0)
    zrq = proj(OFF_RQ, W_RET_QK)
    zrk = proj(OFF_RK, W_RET_QK)

    def head_norm_t(zh_t, g_col):
        parts = []
        for hh in range(H_FOX):
            zz = zh_t[hh * DH_FOX:(hh + 1) * DH_FOX]
            ms = jnp.sum(zz * zz, axis=0, keepdims=True) * (1.0 / DH_FOX)
            parts.append(zz * lax.rsqrt(ms + EPS))
        return jnp.concatenate(parts, axis=0) * g_col

    fq_t = head_norm_t(z_t[:W_FOX], gfq_ref[...]) * (DH_FOX ** -0.5 * LOG2E)
    fk_t = head_norm_t(z_t[W_FOX:2 * W_FOX], gfk_ref[...])
    fv_t = z_t[2 * W_FOX:3 * W_FOX]
    logf_t = _log_sigmoid(z_t[3 * W_FOX:] + bf_ref[...])
    ri = lax.broadcasted_iota(jnp.int32, (tb, tb), 0)
    ci = lax.broadcasted_iota(jnp.int32, (tb, tb), 1)
    key_le_query = ri <= ci
    c_t = _cumsum_lanes(jnp.where(key_le_query, 1.0, 0.0).astype(BF16), logf_t) + ccar_scr[...]
    ccar_scr[...] = c_t[:, tb - 1:tb]
    rv = proj(OFF_RV, W_RET_V).astype(BF16)
    fk = fk_t.T
    fkt_ref[0] = fk_t
    fvt_ref[0] = fv_t
    lft_ref[0] = logf_t[:H_FOX]
    hi, mid, lo = _split3(c_t[:H_FOX] * (-LOG2E))
    parts_t = jnp.concatenate(
        [hi.astype(F32), mid.astype(F32), lo.astype(F32), jnp.zeros((LANES - 3 * H_FOX, tb), F32)], axis=0)
    bias = pltpu.roll(parts_t.T, DH_FOX, 1)
    lane = lax.broadcasted_iota(jnp.int32, (tb, LANES), 1)
    vrow = lax.broadcasted_iota(jnp.int32, (V_ROWS - DH_FOX, tb), 0)
    ones_row = jnp.where(vrow == 0, 1.0, 0.0).astype(BF16)
    fv_tb = fv_t.astype(BF16)
    for hh in range(H_FOX):
        vt_scr[hh, :, pl.ds(t0, tb)] = jnp.concatenate([fv_tb[hh * DH_FOX:(hh + 1) * DH_FOX, :], ones_row], axis=0)
    qrow = lax.broadcasted_iota(jnp.int32, (LANES - DH_FOX, tb), 0)
    k_blk = []
    qt_blk = []
    for hh in range(H_FOX):
        pair = fk[:, (hh // 2) * LANES:(hh // 2 + 1) * LANES]
        if hh % 2:
            pair = pltpu.roll(pair, DH_FOX, 1)
        k_aug = jnp.where(lane < DH_FOX, pair, bias).astype(BF16)
        k_scr[hh, pl.ds(t0, tb), :] = k_aug
        k_blk.append(k_aug)
        ones = jnp.where(
            jnp.logical_or(qrow == hh, jnp.logical_or(qrow == H_FOX + hh, qrow == 2 * H_FOX + hh)), 1.0, 0.0)
        qt_aug = jnp.concatenate([fq_t[hh * DH_FOX:(hh + 1) * DH_FOX, :], ones], axis=0).astype(BF16)
        qt_scr[hh] = qt_aug
        qt_blk.append(qt_aug)

    s_all = [_mm(k_blk[hh], qt_blk[hh]) for hh in range(H_FOX)]
    zrg = proj(OFF_RG, W_RET_V)
    p_all = []
    for hh in range(H_FOX):
        s_t = jnp.where(key_le_query, s_all[hh], -jnp.inf)
        m = jnp.max(s_t, axis=0, keepdims=True)
        m_scr[hh] = m
        p_all.append(jnp.exp2((s_t - m).astype(BF16)))
    for hh in range(H_FOX):
        acc_scr[hh] = _mm(vt_scr[hh, :, pl.ds(t0, tb)], p_all[hh])

    def scores(buf, kb, hh):
        off = pl.multiple_of(kb * tb, tb)
        buf[hh] = _mm(k_scr[hh, pl.ds(off, tb), :], qt_scr[hh])

    def softmax(buf, hh):
        s_t = buf[hh]
        m_old = m_scr[hh]
        m_new = jnp.maximum(m_old, jnp.max(s_t, axis=0, keepdims=True))
        m_scr[hh] = m_new
        return hh, jnp.exp2((s_t - m_new).astype(BF16)), jnp.exp2(m_old - m_new)

    def accumulate(kb, pending):
        hh, p_t, alpha = pending
        off = pl.multiple_of(kb * tb, tb)
        acc_scr[hh] = alpha * acc_scr[hh] + _mm(vt_scr[hh, :, pl.ds(off, tb)], p_t)

    def absorb(buf, kb, also=None):
        pending = None
        for hh in range(H_FOX):
            cur = softmax(buf, hh)
            if also is not None:
                also(hh)
            if pending is not None:
                accumulate(kb, pending)
            pending = cur
        accumulate(kb, pending)

    def run_blocks(kb, count):
        bufs = (sa_scr, sb_scr)
        for hh in range(H_FOX):
            scores(bufs[0], kb, hh)
        for c in range(count):
            nxt = None
            if c + 1 < count:
                nxt = functools.partial(scores, bufs[(c + 1) % 2], kb + c + 1)
            absorb(bufs[c % 2], kb + c, also=nxt)

    def kv_quad(i, carry):
        run_blocks(4 * i, 4)
        return carry

    n_quads = lax.shift_right_logical(j, 2)
    lax.fori_loop(0, n_quads, kv_quad, 0)
    for rest in range(1, 4):
        pl.when(jnp.bitwise_and(j, 3) == rest)(functools.partial(run_blocks, 4 * n_quads, rest))

    zfg = proj(OFF_FG, W_FOX)
    gate0 = gate(0)
    fo = jnp.concatenate(
        [acc_scr[hh, :DH_FOX, :] / acc_scr[hh, DH_FOX:DH_FOX + 1, :] for hh in range(H_FOX)], axis=0).T
    a_in = (fo * _silu(zfg)).astype(BF16)

    cos = cos_ref[...]
    sin = sin_ref[...]
    rq, rkt, rvs, states = [], [], [], []
    for hh in range(H_RET):
        cols = slice(hh * DK_RET, (hh + 1) * DK_RET)
        rq.append(_rope(zrq[:, cols], cos, sin).astype(BF16))
        rkt.append((_rope(zrk[:, cols], cos, sin) * (DK_RET ** -0.5)).T)
        rvs.append(rv[:, hh * DV_RET:(hh + 1) * DV_RET])
        states.append(st_ref[0, hh])
    inter = [_mm(rq[hh], states[hh].astype(BF16)) for hh in range(H_RET)]
    scores = [_mm(rq[hh], rkt[hh].astype(BF16)) for hh in range(H_RET)]
    zmq = proj(OFF_MQ, W_MEM)
    merged = gated(gate0, a_in, wpf_ref)
    ret_o = [inter[hh] * decq_scr[hh] + _mm((scores[hh] * dmask_scr[hh]).astype(BF16), rvs[hh])
             for hh in range(H_RET)]
    for hh in range(H_RET):
        st_ref[0, hh] = ssc_scr[hh] * states[hh] + _mm((rkt[hh] * deck_scr[hh]).astype(BF16), rvs[hh])
    gate1 = gate(1)
    zmg = proj(OFF_MG, W_MEM)

    mem_s = []
    for hh in range(H_MEM):
        cols = slice(hh * DH_MEM, (hh + 1) * DH_MEM)
        q = _rms_rows(zmq[:, cols], gmq_ref[...]).astype(BF16)
        mem_s.append(_mm(q, mkt_ref[0, cols, :]) * (DH_MEM ** -0.5))
    r_in = jnp.concatenate(
        [_group_norm(ret_o[hh]) * _silu(zrg[:, hh * DV_RET:(hh + 1) * DV_RET]) for hh in range(H_RET)],
        axis=1).astype(BF16)
    merged = [m + t for m, t in zip(merged, gated(gate1, r_in, wpr_ref))]
    m_parts = []
    for hh in range(H_MEM):
        cols = slice(hh * DH_MEM, (hh + 1) * DH_MEM)
        p = jnp.exp(mem_s[hh] - jnp.max(mem_s[hh], axis=-1, keepdims=True))
        l = jnp.sum(p, axis=-1, keepdims=True)
        m_parts.append(_mm(p.astype(BF16), mvb_ref[0, :, cols]) / l)
    gate2 = gate(2)
    m_in = (jnp.concatenate(m_parts, axis=1) * _silu(zmg)).astype(BF16)
    merged = [m + t for m, t in zip(merged, gated(gate2, m_in, wpm_ref))]

    y = x_ref[0]
    for c in range(MERGE_PARTS):
        y = y + _mm(merged[c].astype(BF16), wo_ref[c * part:(c + 1) * part, :])
    y_ref[0] = y


def _resident(shape):
    nd = len(shape)
    return pl.BlockSpec(shape, lambda b, j: (0,) * nd, pipeline_mode=pl.Buffered(1))


def _prompt_call(x, cos, sin, mkt, mvb, wqkvf_t, bf_col, gfq_col, gfk_col, log_g, g_norm, b_merge, gmq, w_main, wpf,
                 wpr, wpm, wo, tb=PROMPT_BLOCK):
    B, T, _ = x.shape
    nt = T // tb
    kernel = functools.partial(_prompt_kernel, tb=tb)
    in_specs = [
        pl.BlockSpec(memory_space=pltpu.SMEM),
        pl.BlockSpec((1, tb, D_MODEL), lambda b, j: (b, j, 0)),
        pl.BlockSpec((tb, DK_RET), lambda b, j: (j, 0)),
        pl.BlockSpec((tb, DK_RET), lambda b, j: (j, 0)),
        pl.BlockSpec((1, W_MEM, N_MEM), lambda b, j: (b, 0, 0)),
        pl.BlockSpec((1, N_MEM, W_MEM), lambda b, j: (b, 0, 0)),
        _resident((1, D_MODEL)),
        _resident((F_ROWS, 1)),
        _resident((1, N_BRANCH * D_MODEL)),
        _resident((W_FOX, 1)),
        _resident((W_FOX, 1)),
        _resident((1, DH_MEM)),
        _resident((D_MODEL, W_MAIN)),
        _resident((W_FOX, D_MODEL)),
        _resident((W_RET_V, D_MODEL)),
        _resident((W_MEM, D_MODEL)),
        _resident((D_MODEL, D_MODEL)),
        _resident((3 * W_FOX + F_ROWS, D_MODEL)),
    ]
    out_specs = [
        pl.BlockSpec((1, tb, D_MODEL), lambda b, j: (b, j, 0)),
        pl.BlockSpec((1, W_FOX, tb), lambda b, j: (b, 0, j)),
        pl.BlockSpec((1, W_FOX, tb), lambda b, j: (b, 0, j)),
        pl.BlockSpec((1, H_FOX, tb), lambda b, j: (b, 0, j)),
        pl.BlockSpec((1, H_RET, DK_RET, DV_RET), lambda b, j: (b, 0, 0, 0)),
    ]
    out_shape = [
        jax.ShapeDtypeStruct((B, T, D_MODEL), F32),
        jax.ShapeDtypeStruct((B, W_FOX, T), F32),
        jax.ShapeDtypeStruct((B, W_FOX, T), F32),
        jax.ShapeDtypeStruct((B, H_FOX, T), F32),
        jax.ShapeDtypeStruct((B, H_RET, DK_RET, DV_RET), F32),
    ]
    scratch = [
        pltpu.VMEM((H_FOX, T, LANES), BF16),
        pltpu.VMEM((H_FOX, V_ROWS, T), BF16),
        pltpu.VMEM((H_FOX, LANES, tb), BF16),
        pltpu.VMEM((H_FOX, 1, tb), F32),
        pltpu.VMEM((H_FOX, V_ROWS, tb), F32),
        pltpu.VMEM((H_FOX, tb, tb), F32),
        pltpu.VMEM((H_FOX, tb, tb), F32),
        pltpu.VMEM((F_ROWS, 1), F32),
        pltpu.VMEM((H_RET, tb, tb), F32),
        pltpu.VMEM((H_RET, tb, 1), F32),
        pltpu.VMEM((H_RET, 1, tb), F32),
        pltpu.VMEM((H_RET, 1, DV_RET), F32),
    ]
    return pl.pallas_call(
        kernel,
        grid=(B, nt),
        in_specs=in_specs,
        out_specs=out_specs,
        out_shape=out_shape,
        scratch_shapes=scratch,
        compiler_params=pltpu.CompilerParams(
            dimension_semantics=("arbitrary", "arbitrary"), vmem_limit_bytes=VMEM_LIMIT_BYTES),
        name="prompt",
    )(log_g, x, cos, sin, mkt, mvb, g_norm, bf_col, b_merge, gfq_col, gfk_col, gmq, w_main, wpf, wpr, wpm, wo, wqkvf_t)


def _sample_kernel(lg_ref, x_ref, cos_ref, sin_ref, ckt_ref, cvt_ref, clf_ref, st0_ref, cmk_ref, cmv_ref,
                   gn_ref, bf_ref, bm_ref, gfq_ref, gfk_ref, gmq_ref, seg_ref, w_ref, wpf_ref, wpr_ref, wpm_ref,
                   wo_ref, wft_ref,
                   y_ref, fk_ref, fv_ref, lft_ref, st_ref,
                   h_scr, fq_scr, knt_scr, vn_scr, ncn_scr, rq_scr, rkt_scr, rv_scr, mq_scr, fo_scr, ro_scr, mo_scr,
                   *, nb, ts, past):
    b = pl.program_id(0)
    n = nb * ts
    r0 = pl.multiple_of(b * ts, ts)

    @pl.when(b == 0)
    def _project_all():
        h = _rms_rows(x_ref[...], gn_ref[...]).astype(BF16)
        h_scr[...] = h
        fq_scr[...] = _head_norm64(_mm(h, w_ref[:, OFF_FQ:OFF_FQ + W_FOX]), gfq_ref[...], seg_ref) * (DH_FOX ** -0.5)
        fk = _head_norm64(_mm(h, w_ref[:, OFF_FK:OFF_FK + W_FOX]), gfk_ref[...], seg_ref)
        fv = _mm(h, w_ref[:, OFF_FV:OFF_FV + W_FOX])
        logf_t = _log_sigmoid(_mm_nt(wft_ref[...], h) + bf_ref[...])
        fk_ref[...] = fk
        fv_ref[...] = fv
        lft_ref[...] = logf_t[:H_FOX]
        knt_scr[...] = fk.T.astype(BF16)
        vn_scr[...] = fv.astype(BF16)
        ri = lax.broadcasted_iota(jnp.int32, (n, n), 0)
        ci = lax.broadcasted_iota(jnp.int32, (n, n), 1)
        same = _idiv(ri, ts) == _idiv(ci, ts)
        triu = jnp.where(jnp.logical_and(same, ri <= ci), 1.0, 0.0).astype(BF16)
        ncn_scr[...] = -_cumsum_lanes(triu, logf_t)
        cos = cos_ref[...]
        sin = sin_ref[...]
        zrq = _mm(h, w_ref[:, OFF_RQ:OFF_RQ + W_RET_QK])
        zrk = _mm(h, w_ref[:, OFF_RK:OFF_RK + W_RET_QK])
        for hh in range(H_RET):
            cols = slice(hh * DK_RET, (hh + 1) * DK_RET)
            rq_scr[:, cols] = _rope(zrq[:, cols], cos, sin)
            rkt_scr[cols, :] = (_rope(zrk[:, cols], cos, sin) * (DK_RET ** -0.5)).T
        rv_scr[...] = _mm(h, w_ref[:, OFF_RV:OFF_RV + W_RET_V]).astype(BF16)
        zmq = _mm(h, w_ref[:, OFF_MQ:OFF_MQ + W_MEM])
        for hh in range(H_MEM):
            cols = slice(hh * DH_MEM, (hh + 1) * DH_MEM)
            mq_scr[:, cols] = _rms_rows(zmq[:, cols], gmq_ref[...])

    nr = H_FOX * ts
    fq_b = fq_scr[pl.ds(r0, ts), :]
    row_head = _idiv(lax.broadcasted_iota(jnp.int32, (nr, W_FOX), 0), ts)
    col_head = _idiv(lax.broadcasted_iota(jnp.int32, (nr, W_FOX), 1), DH_FOX)
    head_sel = row_head == col_head
    q_bd = jnp.where(head_sel, jnp.concatenate([fq_b] * H_FOX, axis=0), 0.0).astype(BF16)

    cc = 256
    ri = lax.broadcasted_iota(jnp.int32, (cc, cc), 0)
    ci = lax.broadcasted_iota(jnp.int32, (cc, cc), 1)
    triu = jnp.where(ri <= ci, 1.0, 0.0).astype(BF16)
    carry = jnp.zeros((H_FOX, 1), F32)
    c_chunks = []
    for c in range(past // cc):
        cch = _cumsum_lanes(triu, clf_ref[0, :, c * cc:(c + 1) * cc]) + carry
        carry = cch[:, cc - 1:cc]
        c_chunks.append(cch)
    c_past = jnp.concatenate(c_chunks, axis=1)
    ncp = jnp.concatenate([jnp.broadcast_to(-c_past[hh:hh + 1, :], (ts, past)) for hh in range(H_FOX)], axis=0)
    ncn = ncn_scr[:H_FOX, :] - carry
    ncn = jnp.concatenate([jnp.broadcast_to(ncn[hh:hh + 1, :], (ts, n)) for hh in range(H_FOX)], axis=0)

    s_p = _mm(q_bd, ckt_ref[0].astype(BF16)) + ncp
    s_n = _mm(q_bd, knt_scr[...]) + ncn
    tok = lax.broadcasted_iota(jnp.int32, (nr, n), 1)
    qt = _imod(lax.broadcasted_iota(jnp.int32, (nr, n), 0), ts)
    valid = jnp.logical_and(_idiv(tok, ts) == b, _imod(tok, ts) <= qt)
    s_n = jnp.where(valid, s_n, -jnp.inf)
    m = jnp.maximum(jnp.max(s_p, axis=-1, keepdims=True), jnp.max(s_n, axis=-1, keepdims=True))
    p_p = jnp.exp(s_p - m)
    p_n = jnp.exp(s_n - m)
    l = jnp.sum(p_p, axis=-1, keepdims=True) + jnp.sum(p_n, axis=-1, keepdims=True)
    o_past = _mm_nt(p_p.astype(BF16), cvt_ref[0].astype(BF16))
    o_bd = (o_past + _mm(p_n.astype(BF16), vn_scr[...])) / l
    o_bd = jnp.where(head_sel, o_bd, 0.0)
    fo = o_bd[0:ts, :]
    for hh in range(1, H_FOX):
        fo = fo + o_bd[hh * ts:(hh + 1) * ts, :]
    fo_scr[pl.ds(r0, ts), :] = fo

    tokr = lax.broadcasted_iota(jnp.int32, (ts, n), 1)
    qtr = lax.broadcasted_iota(jnp.int32, (ts, n), 0)
    in_b = _idiv(tokr, ts) == b
    dt = (qtr - _imod(tokr, ts)).astype(F32)
    tokc = lax.broadcasted_iota(jnp.int32, (1, n), 1)
    kpos = _imod(tokc, ts).astype(F32)
    rowi = lax.broadcasted_iota(jnp.int32, (ts, 1), 0).astype(F32)
    r_parts = []
    for hh in range(H_RET):
        lg = lg_ref[hh]
        q = rq_scr[pl.ds(r0, ts), hh * DK_RET:(hh + 1) * DK_RET].astype(BF16)
        kt = rkt_scr[hh * DK_RET:(hh + 1) * DK_RET, :]
        v = rv_scr[:, hh * DV_RET:(hh + 1) * DV_RET]
        state = st0_ref[0, hh]
        dmask = jnp.where(jnp.logical_and(in_b, dt >= 0.0), jnp.exp(lg * jnp.maximum(dt, 0.0)), 0.0)
        inter = _mm(q, state.astype(BF16)) * jnp.exp(lg * (rowi + 1.0))
        scores = _mm(q, kt.astype(BF16)) * dmask
        r_parts.append(inter + _mm(scores.astype(BF16), v))
        deck = jnp.where(_idiv(tokc, ts) == b, jnp.exp(lg * (ts - 1.0 - kpos)), 0.0)
        st_ref[0, hh] = jnp.exp(jnp.full((1, DV_RET), lg * ts, F32)) * state + _mm((kt * deck).astype(BF16), v)
    ro_scr[pl.ds(r0, ts), :] = jnp.concatenate(r_parts, axis=1)

    m_parts = []
    for hh in range(H_MEM):
        cols = slice(hh * DH_MEM, (hh + 1) * DH_MEM)
        q = mq_scr[pl.ds(r0, ts), cols].astype(BF16)
        rows = pl.ds(hh, N_MEM, stride=H_MEM)
        s = _mm_nt(q, cmk_ref[0, rows, :].astype(BF16)) * (DH_MEM ** -0.5)
        p = jnp.exp(s - jnp.max(s, axis=-1, keepdims=True))
        l = jnp.sum(p, axis=-1, keepdims=True)
        m_parts.append(_mm(p.astype(BF16), cmv_ref[0, rows, :].astype(BF16)) / l)
    mo_scr[pl.ds(r0, ts), :] = jnp.concatenate(m_parts, axis=1)

    @pl.when(b == nb - 1)
    def _output_all():
        h = h_scr[...]
        a_in = (fo_scr[...] * _silu(_mm(h, w_ref[:, OFF_FG:OFF_FG + W_FOX]))).astype(BF16)
        merged = _gate(h, w_ref, bm_ref, 0) * _mm(a_in, wpf_ref[...])
        zrg = _mm(h, w_ref[:, OFF_RG:OFF_RG + W_RET_V])
        ro = ro_scr[...]
        r_parts = []
        for hh in range(H_RET):
            cols = slice(hh * DV_RET, (hh + 1) * DV_RET)
            r_parts.append(_group_norm(ro[:, cols]) * _silu(zrg[:, cols]))
        r_in = jnp.concatenate(r_parts, axis=1).astype(BF16)
        merged = merged + _gate(h, w_ref, bm_ref, 1) * _mm(r_in, wpr_ref[...])
        m_in = (mo_scr[...] * _silu(_mm(h, w_ref[:, OFF_MG:OFF_MG + W_MEM]))).astype(BF16)
        merged = merged + _gate(h, w_ref, bm_ref, 2) * _mm(m_in, wpm_ref[...])
        y_ref[...] = x_ref[...] + _mm(merged.astype(BF16), wo_ref[...])


def _sample_call(x2d, cos, sin, ckt, cvt, clf_t, st0, cmk, cmv, wf_t, bf_col, log_g, g_norm, b_merge, gfq, gfk, gmq, seg,
                 w_main, wpf, wpr, wpm, wo, nb, ts):
    n = nb * ts
    past = ckt.shape[2]
    kernel = functools.partial(_sample_kernel, nb=nb, ts=ts, past=past)

    def res(shape):
        nd = len(shape)
        return pl.BlockSpec(shape, lambda b: (0,) * nd, pipeline_mode=pl.Buffered(1))

    in_specs = [
        pl.BlockSpec(memory_space=pltpu.SMEM),
        res((n, D_MODEL)),
        res((n, DK_RET)),
        res((n, DK_RET)),
        pl.BlockSpec((1, W_FOX, past), lambda b: (b, 0, 0)),
        pl.BlockSpec((1, W_FOX, past), lambda b: (b, 0, 0)),
        pl.BlockSpec((1, H_FOX, past), lambda b: (b, 0, 0)),
        pl.BlockSpec((1, H_RET, DK_RET, DV_RET), lambda b: (b, 0, 0, 0)),
        pl.BlockSpec((1, N_MEM * H_MEM, DH_MEM), lambda b: (b, 0, 0)),
        pl.BlockSpec((1, N_MEM * H_MEM, DH_MEM), lambda b: (b, 0, 0)),
        res((1, D_MODEL)),
        res((F_ROWS, 1)),
        res((1, N_BRANCH * D_MODEL)),
        res((1, W_FOX)),
        res((1, W_FOX)),
        res((1, DH_MEM)),
        res((MXU_TILE, MXU_TILE)),
        res((D_MODEL, W_MAIN)),
        res((W_FOX, D_MODEL)),
        res((W_RET_V, D_MODEL)),
        res((W_MEM, D_MODEL)),
        res((D_MODEL, D_MODEL)),
        res((F_ROWS, D_MODEL)),
    ]
    full = lambda shape: pl.BlockSpec(shape, lambda b: (0,) * len(shape))
    out_specs = [
        full((n, D_MODEL)),
        full((n, W_FOX)),
        full((n, W_FOX)),
        full((H_FOX, n)),
        pl.BlockSpec((1, H_RET, DK_RET, DV_RET), lambda b: (b, 0, 0, 0)),
    ]
    out_shape = [
        jax.ShapeDtypeStruct((n, D_MODEL), F32),
        jax.ShapeDtypeStruct((n, W_FOX), F32),
        jax.ShapeDtypeStruct((n, W_FOX), F32),
        jax.ShapeDtypeStruct((H_FOX, n), F32),
        jax.ShapeDtypeStruct((nb, H_RET, DK_RET, DV_RET), F32),
    ]
    scratch = [
        pltpu.VMEM((n, D_MODEL), BF16),
        pltpu.VMEM((n, W_FOX), F32),
        pltpu.VMEM((W_FOX, n), BF16),
        pltpu.VMEM((n, W_FOX), BF16),
        pltpu.VMEM((F_ROWS, n), F32),
        pltpu.VMEM((n, W_RET_QK), F32),
        pltpu.VMEM((W_RET_QK, n), F32),
        pltpu.VMEM((n, W_RET_V), BF16),
        pltpu.VMEM((n, W_MEM), F32),
        pltpu.VMEM((n, W_FOX), F32),
        pltpu.VMEM((n, W_RET_V), F32),
        pltpu.VMEM((n, W_MEM), F32),
    ]
    return pl.pallas_call(
        kernel,
        grid=(nb,),
        in_specs=in_specs,
        out_specs=out_specs,
        out_shape=out_shape,
        scratch_shapes=scratch,
        compiler_params=pltpu.CompilerParams(
            dimension_semantics=("arbitrary",), vmem_limit_bytes=VMEM_LIMIT_BYTES),
        name="sample",
    )(log_g, x2d, cos, sin, ckt, cvt, clf_t, st0, cmk, cmv, g_norm, bf_col, b_merge, gfq, gfk, gmq, seg,
      w_main, wpf, wpr, wpm, wo, wf_t)


def _rope_tables(pos):
    half = DK_RET // 2
    inv = ROPE_BASE ** (-jnp.arange(half, dtype=F32) / half)
    ang = pos.astype(F32)[:, None] * inv[None, :]
    cos = jnp.cos(ang)
    sin = jnp.sin(ang)
    return jnp.concatenate([cos, cos], axis=1), jnp.concatenate([-sin, sin], axis=1)


def _reorder_w_in(w_t):
    assert REORDER_ROWS == 3 * W_FOX and W_MAIN % REORDER_ROWS == 0 and REORDER_ROWS % H_FOX == 0
    assert w_t.shape == (W_MAIN + H_FOX, D_MODEL)
    return pl.pallas_call(
        _reorder_kernel,
        grid=(W_MAIN // REORDER_ROWS,),
        in_specs=[
            pl.BlockSpec((REORDER_ROWS, D_MODEL), lambda c: (c, 0)),
            pl.BlockSpec((H_FOX, D_MODEL), lambda c: ((c + 1) * (REORDER_ROWS // H_FOX), 0)),
        ],
        out_specs=[
            pl.BlockSpec((D_MODEL, REORDER_ROWS), lambda c: (0, c)),
            pl.BlockSpec((3 * W_FOX + F_ROWS, D_MODEL), lambda c: (0, 0)),
        ],
        out_shape=[
            jax.ShapeDtypeStruct((D_MODEL, W_MAIN), BF16),
            jax.ShapeDtypeStruct((3 * W_FOX + F_ROWS, D_MODEL), BF16),
        ],
        compiler_params=pltpu.CompilerParams(
            dimension_semantics=("arbitrary",), vmem_limit_bytes=VMEM_LIMIT_BYTES),
        name="reorder_w_in",
    )(w_t, w_t)


def _reorder_kernel(a_ref, b_ref, o_ref, qkvf_ref):
    c = pl.program_id(0)
    a = a_ref[...]
    shifted = jnp.concatenate([a[H_FOX:], b_ref[...]], axis=0)
    src = jnp.where(c < (3 * W_FOX) // REORDER_ROWS, a, shifted)
    o_ref[...] = src.T.astype(BF16)

    @pl.when(c == 0)
    def _fox_rows():
        qkvf_ref[:3 * W_FOX, :] = a.astype(BF16)
        gates = jnp.concatenate([b_ref[...], jnp.zeros((F_ROWS - H_FOX, D_MODEL), F32)], axis=0)
        qkvf_ref[3 * W_FOX:, :] = gates.astype(BF16)


def kernel(x_prompt, x_sample, mem_prompt, cache_fox_k, cache_fox_v, cache_fox_logf, state_ret, cache_mem_k, cache_mem_v,
           g_norm, g_mem_norm, w_in, b_f, b_merge, g_fox_q, g_fox_k, g_mem_q, g_mem_k, w_mem_kv,
           w_p_fox, w_p_ret, w_p_mem, w_out):
    depth = w_in.shape[0]
    assert depth == 1, "single-layer kernel"
    B, T, _ = x_prompt.shape
    nb, ts, _ = x_sample.shape
    past = cache_fox_k.shape[2]

    log_g = jnp.log1p(-jnp.exp2(-5.0 - jnp.arange(H_RET, dtype=F32)))
    head_of_lane = jnp.arange(MXU_TILE) // DH_FOX
    seg = (head_of_lane[:, None] == head_of_lane[None, :]).astype(BF16)
    w_t = jnp.swapaxes(w_in[0], 0, 1)
    w_main, wqkvf_t = _reorder_w_in(w_t)
    gn = g_norm[0].reshape(1, D_MODEL)
    bm = b_merge[0].reshape(1, N_BRANCH * D_MODEL)
    gfq = jnp.tile(g_fox_q[0], H_FOX).reshape(1, W_FOX)
    gfk = jnp.tile(g_fox_k[0], H_FOX).reshape(1, W_FOX)
    gmq = g_mem_q[0].reshape(1, DH_MEM)
    wpf = w_p_fox[0].astype(BF16)
    wpr = w_p_ret[0].astype(BF16)
    wpm = w_p_mem[0].astype(BF16)
    wo = w_out[0].astype(BF16)
    shared = (gn, bm, gfq, gfk, gmq, seg, w_main, wpf, wpr, wpm, wo)

    mk, mv, mkt, mvb = _memkv_call(mem_prompt, g_mem_norm[0], w_mem_kv[0], g_mem_k[0])

    cos_p, sin_p = _rope_tables(jnp.arange(T, dtype=jnp.int32))
    bf_col = jnp.concatenate([b_f[0], jnp.zeros((F_ROWS - H_FOX,), F32)]).reshape(F_ROWS, 1)
    y_p, fkt_p, fvt_p, lft_p, st_p = _prompt_call(
        x_prompt, cos_p, sin_p, mkt, mvb, wqkvf_t, bf_col, gfq.reshape(W_FOX, 1), gfk.reshape(W_FOX, 1), log_g,
        gn, bm, gmq, w_main, wpf, wpr, wpm, wo)

    def time_major(a_t):
        return jnp.transpose(a_t.reshape(a_t.shape[0], H_FOX, DH_FOX, a_t.shape[2]), (0, 3, 1, 2))[None]

    def time_minor(a):
        return jnp.transpose(a, (0, 2, 3, 1)).reshape(a.shape[0], W_FOX, a.shape[1])

    fk_p = time_major(fkt_p)
    fv_p = time_major(fvt_p)
    lf_p = jnp.swapaxes(lft_p, 1, 2)[None]

    pos_s = past + jnp.arange(ts, dtype=jnp.int32)
    cos_s, sin_s = _rope_tables(jnp.tile(pos_s, nb))
    y_s, fk_s, fv_s, lft_s, st_s = _sample_call(
        x_sample.reshape(nb * ts, D_MODEL), cos_s, sin_s,
        time_minor(cache_fox_k[0]), time_minor(cache_fox_v[0]),
        jnp.swapaxes(cache_fox_logf[0], 1, 2), state_ret[0],
        cache_mem_k[0].reshape(nb, N_MEM * H_MEM, DH_MEM), cache_mem_v[0].reshape(nb, N_MEM * H_MEM, DH_MEM),
        wqkvf_t[3 * W_FOX:], bf_col, log_g, *shared, nb=nb, ts=ts)
    lf_s = jnp.transpose(lft_s.reshape(H_FOX, nb, ts), (1, 2, 0))[None]

    return (y_p, y_s.reshape(nb, ts, D_MODEL),
            fk_p, fv_p, lf_p,
            st_p[None], mk.reshape(1, B, N_MEM, H_MEM, DH_MEM), mv.reshape(1, B, N_MEM, H_MEM, DH_MEM),
            fk_s.reshape(1, nb, ts, H_FOX, DH_FOX), fv_s.reshape(1, nb, ts, H_FOX, DH_FOX),
            lf_s, st_s[None])
```
